```python
import math
import jax
import jax.numpy as jnp
from jax import lax
import numpy as np

D_MODEL = 1024
BATCH = 4
SEQ = 4096
DEPTH = 4

GRID_W = 64
CTX_LEN = 256
DA_HEADS = 4
DA_HEAD_DIM = 64
GQ_HEADS = 8
GQ_KV_HEADS = 2
GQ_HEAD_DIM = 64
DN_HEADS = 4
DN_HEAD_DIM = 128
DN_CHUNK = 64
DN_CONV_W = 3
N_BRANCH = 3
D_FF = 2816
FFN_CONV_W = 3
Q_BLOCK = 128
ROPE_DIM = 64
ROPE_THETA = 10000.0
NORM_EPS = 1e-6

DA_QK_COLS = DA_HEADS * 2 * DA_HEAD_DIM
DA_V_COLS = DA_HEADS * 2 * DA_HEAD_DIM
GQ_Q_COLS = GQ_HEADS * GQ_HEAD_DIM
GQ_KV_COLS = GQ_KV_HEADS * GQ_HEAD_DIM
DN_WIDTH = DN_HEADS * DN_HEAD_DIM
DN_DIR_COLS = 2 * DN_HEADS
GATE_COLS = N_BRANCH * D_MODEL
IN_SPLITS = (DA_QK_COLS, DA_QK_COLS, DA_V_COLS, GQ_Q_COLS, GQ_KV_COLS, GQ_KV_COLS, 3 * DN_WIDTH, DN_DIR_COLS, DN_DIR_COLS, DN_WIDTH, GATE_COLS)
N_IN = sum(IN_SPLITS)

kernel_name = 'hybrid_diffattn_gqa_deltanet_prefix_trunk'


def rms_norm(x, g):
    xf = x.astype(jnp.float32)
    y = xf * lax.rsqrt(jnp.mean(xf * xf, axis=-1, keepdims=True) + NORM_EPS)
    return (y * g.astype(jnp.float32)).astype(x.dtype)


def l2_normalize(x):
    xf = x.astype(jnp.float32)
    return xf * lax.rsqrt(jnp.sum(xf * xf, axis=-1, keepdims=True) + NORM_EPS)


def modulate(x, g, shift, scale):
    return rms_norm(x, g) * (1.0 + scale) + shift


def split_in_proj(y):
    offsets = np.cumsum(np.array(IN_SPLITS))[:-1].tolist()
    return jnp.split(y, offsets, axis=-1)


def axial_rope_tables(rows):
    row = jnp.repeat(jnp.arange(rows, dtype=jnp.int32), GRID_W).astype(jnp.float32)
    col = jnp.tile(jnp.arange(GRID_W, dtype=jnp.int32), rows).astype(jnp.float32)
    d_axis = ROPE_DIM // 2
    inv_freq = ROPE_THETA ** (-jnp.arange(0, d_axis, 2, dtype=jnp.float32) / d_axis)
    ang_r = row[:, None] * inv_freq[None, :]
    ang_c = col[:, None] * inv_freq[None, :]
    return (jnp.cos(ang_r), jnp.sin(ang_r), jnp.cos(ang_c), jnp.sin(ang_c))


def _rotate(x, cos, sin):
    x1, x2 = jnp.split(x, 2, axis=-1)
    return jnp.concatenate([x1 * cos - x2 * sin, x1 * sin + x2 * cos], axis=-1)


def apply_axial_rope(x, tables):
    shape = (x.shape[1],) + (1,) * (x.ndim - 3) + (-1,)
    cr, sr, cc, sc = (t.reshape(shape) for t in tables)
    xr, xcol = jnp.split(x.astype(jnp.float32), 2, axis=-1)
    return jnp.concatenate([_rotate(xr, cr, sr), _rotate(xcol, cc, sc)], axis=-1).astype(x.dtype)


def dwconv_centred(x, w):
    width = w.shape[0]
    pad = width // 2
    t = x.shape[1]
    xp = jnp.pad(x, ((0, 0), (pad, pad), (0, 0)))
    y = xp[:, 0:t] * w[0]
    for j in range(1, width):
        y = y + xp[:, j:j + t] * w[j]
    return y


def sweep_query_blocks(attend, queries):
    b, t = queries[0].shape[:2]
    nb = t // Q_BLOCK

    def to_blocks(a):
        return jnp.moveaxis(a.reshape((b, nb, Q_BLOCK) + a.shape[2:]), 1, 0)

    out = lax.map(lambda blk: attend(*blk), tuple(to_blocks(a) for a in queries))
    out = jnp.moveaxis(out, 0, 1)
    return out.reshape((b, t) + out.shape[3:])


def diff_heads(q, k, v, qn_g, kn_g, rope):
    b, t = q.shape[:2]
    q = rms_norm(q.reshape(b, t, DA_HEADS, 2, DA_HEAD_DIM), qn_g)
    k = rms_norm(k.reshape(b, t, DA_HEADS, 2, DA_HEAD_DIM), kn_g)
    if rope is not None:
        q = apply_axial_rope(q, rope)
        k = apply_axial_rope(k, rope)
    v = v.reshape(b, t, DA_HEADS, 2 * DA_HEAD_DIM)
    return (q[..., 0, :], q[..., 1, :], k[..., 0, :], k[..., 1, :], v)


def diff_lambda(lp, lam_init):
    lp = lp.astype(jnp.float32)
    return jnp.exp(jnp.sum(lp[0] * lp[1])) - jnp.exp(jnp.sum(lp[2] * lp[3])) + lam_init


def diff_attend(q1, q2, k1, k2, v, lam):
    scale = DA_HEAD_DIM ** -0.5
    p1 = jax.nn.softmax(jnp.einsum('bqhd,bshd->bhqs', q1, k1).astype(jnp.float32) * scale, axis=-1)
    p2 = jax.nn.softmax(jnp.einsum('bqhd,bshd->bhqs', q2, k2).astype(jnp.float32) * scale, axis=-1)
    return jnp.einsum('bhqs,bshe->bqhe', (p1 - lam * p2).astype(v.dtype), v)


def diff_out(o, subln_g, lam_init):
    b, t = o.shape[:2]
    return (rms_norm(o, subln_g) * (1.0 - lam_init)).reshape(b, t, DA_V_COLS)


def gqa_heads(q, k, v, qn_g, kn_g, rope):
    b, t = q.shape[:2]
    q = rms_norm(q.reshape(b, t, GQ_HEADS, GQ_HEAD_DIM), qn_g)
    k = rms_norm(k.reshape(b, t, GQ_KV_HEADS, GQ_HEAD_DIM), kn_g)
    if rope is not None:
        q = apply_axial_rope(q, rope)
        k = apply_axial_rope(k, rope)
    v = v.reshape(b, t, GQ_KV_HEADS, GQ_HEAD_DIM)
    return (q, k, v)


def gqa_attend(q, k, v):
    b, nq = q.shape[:2]
    group = GQ_HEADS // GQ_KV_HEADS
    qg = q.reshape(b, nq, GQ_KV_HEADS, group, GQ_HEAD_DIM)
    s = jnp.einsum('bqhgd,bshd->bhgqs', qg, k).astype(jnp.float32) * (GQ_HEAD_DIM ** -0.5)
    p = jax.nn.softmax(s, axis=-1).astype(v.dtype)
    o = jnp.einsum('bhgqs,bshd->bqhgd', p, v)
    return o.reshape(b, nq, GQ_HEADS, GQ_HEAD_DIM)


def deltanet_inputs(qkv, b_raw, a_raw, conv_w, a_log, dt_bias):
    b, t = qkv.shape[:2]
    qkv = jax.nn.silu(dwconv_centred(qkv, conv_w))
    q, k, v = jnp.split(qkv, 3, axis=-1)
    q = l2_normalize(q.reshape(b, t, DN_HEADS, DN_HEAD_DIM))
    k = l2_normalize(k.reshape(b, t, DN_HEADS, DN_HEAD_DIM))
    v = v.reshape(b, t, DN_HEADS, DN_HEAD_DIM)
    beta = jax.nn.sigmoid(b_raw.astype(jnp.float32)).reshape(b, t, 2, DN_HEADS)
    g = -jnp.exp(a_log.astype(jnp.float32)) * jax.nn.softplus(
        a_raw.astype(jnp.float32).reshape(b, t, 2, DN_HEADS) + dt_bias.astype(jnp.float32))
    return (q, k, v, g, beta)


def gated_delta_chunked(q, k, v, g, beta, s0, with_output):
    dtype = v.dtype
    b, t, h, dk = k.shape
    dv = v.shape[-1]
    n = t // DN_CHUNK

    def to_chunks(a):
        a = jnp.moveaxis(a.astype(jnp.float32), 2, 1)
        return a.reshape((b, h, n, DN_CHUNK) + a.shape[3:])

    kc, vc, gch, bc = to_chunks(k), to_chunks(v), to_chunks(g), to_chunks(beta)
    gcum = jnp.cumsum(gch, axis=-1)
    idx = jnp.arange(DN_CHUNK)
    incl = idx[:, None] >= idx[None, :]
    strict = idx[:, None] > idx[None, :]
    decay = jnp.where(incl, jnp.exp(jnp.where(incl, gcum[..., :, None] - gcum[..., None, :], 0.0)), 0.0)
    kb = kc * bc[..., None]
    a_low = jnp.where(strict, jnp.einsum('bhncd,bhnsd->bhncs', kb, kc) * decay, 0.0)
    eye = jnp.eye(DN_CHUNK, dtype=jnp.float32)
    tmat = lax.linalg.triangular_solve(a_low + eye, jnp.broadcast_to(eye, a_low.shape),
                                       left_side=True, lower=True, unit_diagonal=True)
    u = tmat @ (vc * bc[..., None])
    w = tmat @ (kb * jnp.exp(gcum)[..., None])
    g_last = gcum[..., -1]
    k_tail = kc * jnp.exp(g_last[..., None] - gcum)[..., None]
    xs = [u, w, k_tail, jnp.exp(g_last)]
    if with_output:
        qc = to_chunks(q) * (dk ** -0.5)
        xs += [jnp.einsum('bhncd,bhnsd->bhncs', qc, kc) * decay, qc * jnp.exp(gcum)[..., None]]
    xs = tuple(jnp.moveaxis(a_, 2, 0) for a_ in xs)

    def step(s, inp):
        u_i, w_i, kt_i, gl_i = inp[:4]
        v_new = u_i - jnp.einsum('bhcd,bhde->bhce', w_i, s)
        s_next = s * gl_i[..., None, None] + jnp.einsum('bhcd,bhce->bhde', kt_i, v_new)
        if with_output:
            qk_i, qd_i = inp[4], inp[5]
            o_i = jnp.einsum('bhcd,bhde->bhce', qd_i, s) + jnp.einsum('bhcs,bhse->bhce', qk_i, v_new)
            return s_next, o_i
        return s_next, None

    s_fin, o = lax.scan(step, s0.astype(jnp.float32), xs)
    if not with_output:
        return None, s_fin
    o = jnp.transpose(o, (1, 0, 3, 2, 4)).reshape(b, t, h, dv)
    return o.astype(dtype), s_fin


def bidir_delta(q, k, v, g, beta, s0_fwd, s0_bwd, with_output):
    flip = lambda a: jnp.flip(a, axis=1)
    o_f, s_f = gated_delta_chunked(q, k, v, g[:, :, 0], beta[:, :, 0], s0_fwd, with_output)
    o_b, s_b = gated_delta_chunked(flip(q), flip(k), flip(v), flip(g[:, :, 1]), flip(beta[:, :, 1]), s0_bwd, with_output)
    o = o_f + flip(o_b) if with_output else None
    return o, s_f, s_b


def deltanet_out(o, z, g):
    b, t = z.shape[:2]
    z = z.reshape(b, t, DN_HEADS, DN_HEAD_DIM)
    return (rms_norm(o, g) * jax.nn.silu(z)).reshape(b, t, DN_WIDTH)


def merge_branches(gate_cols, o_a, o_b, o_c, w_a, w_b, w_c, w_out):
    g_a, g_b, g_c = jnp.split(jax.nn.sigmoid(gate_cols), 3, axis=-1)
    return (g_a * (o_a @ w_a) + g_b * (o_b @ w_b) + g_c * (o_c @ w_c)) @ w_out


def conv_glu(h, w1, conv_w, conv_b, w2):
    a, u = jnp.split(h @ w1, 2, axis=-1)
    a = dwconv_centred(a, conv_w) + conv_b
    return (jax.nn.silu(a) * u) @ w2


def setup_inputs(seed: int = 0) -> dict:
    key = jax.random.key(seed)
    ks = jax.random.split(key, 32)
    nrm = lambda k, shape, s: jax.random.normal(k, shape, jnp.float32) * s
    gain = lambda k, shape: 1.0 + 0.02 * jax.random.normal(k, shape, jnp.float32)
    dt = jnp.exp(jax.random.uniform(ks[15], (DEPTH, 2, DN_HEADS), jnp.float32, math.log(1e-3), math.log(1e-1)))
    return {
        'x': nrm(ks[0], (BATCH, SEQ, D_MODEL), 1.0),
        'c': nrm(ks[1], (BATCH, D_MODEL), 1.0),
        'ctx': nrm(ks[2], (BATCH, CTX_LEN, D_MODEL), 1.0),
        'c_ctx': nrm(ks[3], (D_MODEL,), 1.0),
        'w_mod': nrm(ks[4], (DEPTH, D_MODEL, 6 * D_MODEL), 0.5 * D_MODEL ** -0.5),
        'b_mod': nrm(ks[5], (DEPTH, 6 * D_MODEL), 0.01),
        'norm1_g': gain(ks[6], (DEPTH, D_MODEL)),
        'w_in': nrm(ks[7], (DEPTH, D_MODEL, N_IN), D_MODEL ** -0.5),
        'da_qn_g': gain(ks[8], (DEPTH, DA_HEAD_DIM)),
        'da_kn_g': gain(ks[9], (DEPTH, DA_HEAD_DIM)),
        'da_lambda': nrm(ks[10], (DEPTH, 4, DA_HEAD_DIM), 0.1),
        'da_subln_g': gain(ks[11], (DEPTH, 2 * DA_HEAD_DIM)),
        'gq_qn_g': gain(ks[12], (DEPTH, GQ_HEAD_DIM)),
        'gq_kn_g': gain(ks[13], (DEPTH, GQ_HEAD_DIM)),
        'dn_conv_w': nrm(ks[14], (DEPTH, DN_CONV_W, 3 * DN_WIDTH), DN_CONV_W ** -0.5),
        'dn_a_log': jnp.log(jax.random.uniform(ks[16], (DEPTH, 2, DN_HEADS), jnp.float32, 1.0, 16.0)),
        'dn_dt_bias': dt + jnp.log(-jnp.expm1(-dt)),
        'dn_norm_g': gain(ks[17], (DEPTH, DN_HEAD_DIM)),
        'w_br_a': nrm(ks[18], (DEPTH, DA_V_COLS, D_MODEL), DA_V_COLS ** -0.5),
        'w_br_b': nrm(ks[19], (DEPTH, GQ_Q_COLS, D_MODEL), GQ_Q_COLS ** -0.5),
        'w_br_c': nrm(ks[20], (DEPTH, DN_WIDTH, D_MODEL), DN_WIDTH ** -0.5),
        'w_o': nrm(ks[21], (DEPTH, D_MODEL, D_MODEL), D_MODEL ** -0.5),
        'norm2_g': gain(ks[22], (DEPTH, D_MODEL)),
        'ffn_w1': nrm(ks[23], (DEPTH, D_MODEL, 2 * D_FF), D_MODEL ** -0.5),
        'ffn_conv_w': nrm(ks[24], (DEPTH, FFN_CONV_W, D_FF), FFN_CONV_W ** -0.5),
        'ffn_conv_b': nrm(ks[25], (DEPTH, D_FF), 0.01),
        'ffn_w2': nrm(ks[26], (DEPTH, D_FF, D_MODEL), D_FF ** -0.5),
    }


def reference(x, c, ctx, c_ctx, w_mod, b_mod, norm1_g, w_in, da_qn_g, da_kn_g, da_lambda, da_subln_g,
              gq_qn_g, gq_kn_g, dn_conv_w, dn_a_log, dn_dt_bias, dn_norm_g, w_br_a, w_br_b, w_br_c, w_o,
              norm2_g, ffn_w1, ffn_conv_w, ffn_conv_b, ffn_w2):
    b, n_lat = x.shape[0], x.shape[1]
    n_ctx = ctx.shape[1]
    rows = n_lat // GRID_W
    rope = axial_rope_tables(rows)
    cond_lat = jax.nn.silu(c)[:, None, :]
    cond_ctx = jax.nn.silu(c_ctx)[None, None, :]
    s_zero = jnp.zeros((b, DN_HEADS, DN_HEAD_DIM, DN_HEAD_DIM), jnp.float32)
    xl, xc = x, ctx
    for l in range(DEPTH):
        last = l == DEPTH - 1
        lam_init = 0.8 - 0.6 * math.exp(-0.3 * l)
        mod_l = jnp.split(cond_lat @ w_mod[l] + b_mod[l], 6, axis=-1)
        mod_c = jnp.split(cond_ctx @ w_mod[l] + b_mod[l], 6, axis=-1)

        hl = modulate(xl, norm1_g[l], mod_l[0], mod_l[1])
        hc = modulate(xc, norm1_g[l], mod_c[0], mod_c[1])
        pl = split_in_proj(hl @ w_in[l])
        pc = split_in_proj(hc @ w_in[l])

        lam = diff_lambda(da_lambda[l], lam_init)
        al = diff_heads(pl[0], pl[1], pl[2], da_qn_g[l], da_kn_g[l], rope)
        ac = diff_heads(pc[0], pc[1], pc[2], da_qn_g[l], da_kn_g[l], None)
        k1_all = jnp.concatenate([ac[2], al[2]], axis=1)
        k2_all = jnp.concatenate([ac[3], al[3]], axis=1)
        va_all = jnp.concatenate([ac[4], al[4]], axis=1)
        oa_l = sweep_query_blocks(lambda q1, q2: diff_attend(q1, q2, k1_all, k2_all, va_all, lam), (al[0], al[1]))
        oa_l = diff_out(oa_l, da_subln_g[l], lam_init)

        bl = gqa_heads(pl[3], pl[4], pl[5], gq_qn_g[l], gq_kn_g[l], rope)
        bc = gqa_heads(pc[3], pc[4], pc[5], gq_qn_g[l], gq_kn_g[l], None)
        kb_all = jnp.concatenate([bc[1], bl[1]], axis=1)
        vb_all = jnp.concatenate([bc[2], bl[2]], axis=1)
        ob_l = sweep_query_blocks(lambda q: gqa_attend(q, kb_all, vb_all), (bl[0],))
        ob_l = ob_l.reshape(b, n_lat, GQ_Q_COLS)

        dl = deltanet_inputs(pl[6], pl[7], pl[8], dn_conv_w[l], dn_a_log[l], dn_dt_bias[l])
        dc = deltanet_inputs(pc[6], pc[7], pc[8], dn_conv_w[l], dn_a_log[l], dn_dt_bias[l])
        oc_ctx, s_fwd, s_bwd = bidir_delta(dc[0], dc[1], dc[2], dc[3], dc[4], s_zero, s_zero, not last)
        oc_l, _, _ = bidir_delta(dl[0], dl[1], dl[2], dl[3], dl[4], s_fwd, s_bwd, True)
        oc_l = deltanet_out(oc_l, pl[9], dn_norm_g[l])

        xl = xl + mod_l[2] * merge_branches(pl[10], oa_l, ob_l, oc_l, w_br_a[l], w_br_b[l], w_br_c[l], w_o[l])
        if not last:
            oa_c = diff_out(diff_attend(ac[0], ac[1], ac[2], ac[3], ac[4], lam), da_subln_g[l], lam_init)
            ob_c = gqa_attend(bc[0], bc[1], bc[2]).reshape(b, n_ctx, GQ_Q_COLS)
            oc_c = deltanet_out(oc_ctx, pc[9], dn_norm_g[l])
            xc = xc + mod_c[2] * merge_branches(pc[10], oa_c, ob_c, oc_c, w_br_a[l], w_br_b[l], w_br_c[l], w_o[l])

        hl = modulate(xl, norm2_g[l], mod_l[3], mod_l[4])
        xl = xl + mod_l[5] * conv_glu(hl, ffn_w1[l], ffn_conv_w[l], ffn_conv_b[l], ffn_w2[l])
        if not last:
            hc = modulate(xc, norm2_g[l], mod_c[3], mod_c[4])
            xc = xc + mod_c[5] * conv_glu(hc, ffn_w1[l], ffn_conv_w[l], ffn_conv_b[l], ffn_w2[l])
    return xl
```

```python
import functools
import math

import jax
import jax.numpy as jnp
from jax import lax
from jax.experimental import pallas as pl
from jax.experimental.pallas import tpu as pltpu

F32 = jnp.float32
BF16 = jnp.bfloat16

TILE = 256
CHUNK = 64
GRID_W = 64
ROPE_THETA = 10000.0
NORM_EPS = 1e-6
DA_HEADS, DA_HEAD_DIM = 4, 64
GQ_HEADS, GQ_KV_HEADS, GQ_HEAD_DIM = 8, 2, 64
DN_HEADS, DN_HEAD_DIM = 4, 128
LOG2E = 1.4426950408889634
NEG_BIG = -1e30
VMEM_LIMIT = 48 * 1024 * 1024

DA_COLS = DA_HEADS * 2 * DA_HEAD_DIM
GQ_Q_COLS = GQ_HEADS * GQ_HEAD_DIM
GQ_KV_COLS = GQ_KV_HEADS * GQ_HEAD_DIM
DN_WIDTH = DN_HEADS * DN_HEAD_DIM
ATTN_ROWS = 3 * DA_COLS + GQ_Q_COLS + 2 * GQ_KV_COLS
NAT_GATE_OFF, NAT_QKV_OFF, NAT_Z_OFF, NAT_BA_OFF = 0, 3072, 4608, 5120
NAT_COLS = 5376


def _params(*sem):
    return pltpu.CompilerParams(dimension_semantics=sem, vmem_limit_bytes=VMEM_LIMIT)


def _sigmoid(x):
    return 1.0 / (1.0 + jnp.exp(-x))


def _silu(x):
    return x * _sigmoid(x)


def _dot(a, b):
    return jnp.dot(a, b, preferred_element_type=F32)


def _dot_nt(a, b):
    return lax.dot_general(a, b, (((1,), (1,)), ((), ())), preferred_element_type=F32)


def _dot_tn(a, b):
    return lax.dot_general(a, b, (((0,), (0,)), ((), ())), preferred_element_type=F32)


def _mod_kernel(c_ref, w_ref, b_ref, o_ref):
    cond = _silu(c_ref[...])
    o_ref[0] = jnp.dot(cond, w_ref[0], preferred_element_type=F32,
                       precision=lax.Precision.HIGHEST) + b_ref[0]


def _modulation(cond, w_mod, b_mod):
    depth, d, n = w_mod.shape
    tn = 1536
    return pl.pallas_call(
        _mod_kernel,
        grid=(depth, n // tn),
        in_specs=[pl.BlockSpec((8, d), lambda l, j: (0, 0)),
                  pl.BlockSpec((1, d, tn), lambda l, j: (l, 0, j)),
                  pl.BlockSpec((1, 1, tn), lambda l, j: (l, 0, j))],
        out_specs=pl.BlockSpec((1, 8, tn), lambda l, j: (l, 0, j)),
        out_shape=jax.ShapeDtypeStruct((depth, 8, n), F32),
        compiler_params=_params("parallel", "parallel"),
    )(cond, w_mod, b_mod.reshape(depth, 1, n))


def _norm_mod_kernel(x_ref, g_ref, shift_ref, scale_ref, o_ref):
    x = x_ref[...]
    ms = jnp.mean(x * x, axis=-1, keepdims=True)
    y = x * lax.rsqrt(ms + NORM_EPS) * g_ref[...]
    o_ref[...] = (y * (1.0 + scale_ref[0]) + shift_ref[0]).astype(o_ref.dtype)


def _norm_mod(x, g, shift_tab, scale_tab):
    m, d = x.shape
    return pl.pallas_call(
        _norm_mod_kernel,
        grid=(m // TILE,),
        in_specs=[pl.BlockSpec((TILE, d), lambda s: (s, 0)),
                  pl.BlockSpec((1, d), lambda s: (0, 0)),
                  pl.BlockSpec((1, 1, d), lambda s: (s, 0, 0)),
                  pl.BlockSpec((1, 1, d), lambda s: (s, 0, 0))],
        out_specs=pl.BlockSpec((TILE, d), lambda s: (s, 0)),
        out_shape=jax.ShapeDtypeStruct((m, d), BF16),
        compiler_params=_params("parallel"),
    )(x, g.reshape(1, d), shift_tab, scale_tab)


def _mm_nn_kernel(a_ref, b_ref, o_ref):
    o_ref[...] = _dot(a_ref[...], b_ref[...]).astype(o_ref.dtype)


def _mm_nt_kernel(wt_ref, a_ref, o_ref):
    o_ref[...] = _dot_nt(wt_ref[...], a_ref[...]).astype(o_ref.dtype)


def _row_tile(m):
    for t in (1024, 512, 256):
        if m % t == 0:
            return t
    raise ValueError(m)


def _matmul_nn(a, b, tn, out_dtype):
    m, k = a.shape
    n = b.shape[1]
    tm = _row_tile(m)
    return pl.pallas_call(
        _mm_nn_kernel,
        grid=(m // tm, n // tn),
        in_specs=[pl.BlockSpec((tm, k), lambda i, j: (i, 0)),
                  pl.BlockSpec((k, tn), lambda i, j: (0, j))],
        out_specs=pl.BlockSpec((tm, tn), lambda i, j: (i, j)),
        out_shape=jax.ShapeDtypeStruct((m, n), out_dtype),
        compiler_params=_params("parallel", "parallel"),
    )(a, b)


def _matmul_nt(wt, a, tn, out_dtype):
    m, k = a.shape
    n = wt.shape[0]
    tm = _row_tile(m)
    return pl.pallas_call(
        _mm_nt_kernel,
        grid=(m // tm, n // tn),
        in_specs=[pl.BlockSpec((tn, k), lambda i, j: (j, 0)),
                  pl.BlockSpec((tm, k), lambda i, j: (i, 0))],
        out_specs=pl.BlockSpec((tn, tm), lambda i, j: (j, i)),
        out_shape=jax.ShapeDtypeStruct((n, m), out_dtype),
        compiler_params=_params("parallel", "parallel"),
    )(wt, a)


def _norm_rope_t(x, gain, cos, sin, scale):
    ms = jnp.mean(x * x, axis=0, keepdims=True)
    xn = x * lax.rsqrt(ms + NORM_EPS) * gain
    sw = jnp.concatenate([xn[16:32], xn[0:16], xn[48:64], xn[32:48]], axis=0)
    out = xn * cos + sw * sin
    return out * scale if scale != 1.0 else out


def _attn_prep_kernel(y_ref, cos_ref, sin_ref, gain_ref,
                      qda_ref, kda_ref, vda_ref, qgq_ref, kgq_ref, vgq_ref):
    cos = cos_ref[...]
    sin = sin_ref[...]
    qscale = (DA_HEAD_DIM ** -0.5) * LOG2E
    hd = DA_HEAD_DIM
    for h in range(DA_HEADS):
        halves_q, halves_k = [], []
        for half in range(2):
            r = h * 2 * hd + half * hd
            halves_q.append(_norm_rope_t(y_ref[r:r + hd, :], gain_ref[0], cos, sin, qscale))
            halves_k.append(_norm_rope_t(y_ref[DA_COLS + r:DA_COLS + r + hd, :], gain_ref[1], cos, sin, 1.0))
        qda_ref[0, h] = jnp.concatenate(halves_q, axis=0).astype(BF16)
        kda_ref[0, h] = jnp.concatenate(halves_k, axis=0).T.astype(BF16)
        r = 2 * DA_COLS + h * 2 * hd
        vda_ref[0, h] = y_ref[r:r + 2 * hd, :].astype(BF16)
    base = 3 * DA_COLS
    gscale = (GQ_HEAD_DIM ** -0.5) * LOG2E
    for h in range(GQ_HEADS):
        r = base + h * GQ_HEAD_DIM
        qgq_ref[0, h] = _norm_rope_t(y_ref[r:r + GQ_HEAD_DIM, :], gain_ref[2], cos, sin, gscale).astype(BF16)
    base += GQ_Q_COLS
    ks = [_norm_rope_t(y_ref[base + h * GQ_HEAD_DIM:base + (h + 1) * GQ_HEAD_DIM, :], gain_ref[3], cos, sin, 1.0)
          for h in range(GQ_KV_HEADS)]
    kgq_ref[0] = jnp.concatenate(ks, axis=0).T.astype(BF16)
    base += GQ_KV_COLS
    for h in range(GQ_KV_HEADS):
        r = base + h * GQ_HEAD_DIM
        vgq_ref[0, h] = y_ref[r:r + GQ_HEAD_DIM, :].astype(BF16)


def _attn_prep(y_t, cos_t, sin_t, gains, n_t):
    m = y_t.shape[1]
    s_tiles = m // TILE
    out_shape = (
        jax.ShapeDtypeStruct((s_tiles, DA_HEADS, 2 * DA_HEAD_DIM, TILE), BF16),
        jax.ShapeDtypeStruct((s_tiles, DA_HEADS, TILE, 2 * DA_HEAD_DIM), BF16),
        jax.ShapeDtypeStruct((s_tiles, DA_HEADS, 2 * DA_HEAD_DIM, TILE), BF16),
        jax.ShapeDtypeStruct((s_tiles, GQ_HEADS, GQ_HEAD_DIM, TILE), BF16),
        jax.ShapeDtypeStruct((s_tiles, TILE, GQ_KV_COLS), BF16),
        jax.ShapeDtypeStruct((s_tiles, GQ_KV_HEADS, GQ_HEAD_DIM, TILE), BF16),
    )
    blk4 = lambda sds: pl.BlockSpec((1,) + sds.shape[1:], lambda s: (s,) + (0,) * (len(sds.shape) - 1))
    return pl.pallas_call(
        _attn_prep_kernel,
        grid=(s_tiles,),
        in_specs=[pl.BlockSpec((ATTN_ROWS, TILE), lambda s: (0, s)),
                  pl.BlockSpec((GQ_HEAD_DIM, TILE), lambda s: (0, s % n_t)),
                  pl.BlockSpec((GQ_HEAD_DIM, TILE), lambda s: (0, s % n_t)),
                  pl.BlockSpec((4, GQ_HEAD_DIM, TILE), lambda s: (0, 0, 0))],
        out_specs=tuple(blk4(s) for s in out_shape),
        out_shape=out_shape,
        compiler_params=_params("parallel"),
    )(y_t, cos_t, sin_t, gains)


def _softmax_step(s, m_old, l_old):
    m_new = jnp.maximum(m_old, jnp.max(s, axis=0, keepdims=True))
    alpha = jnp.exp2(m_old - m_new)
    p = jnp.exp2(s - m_new)
    l_new = alpha * l_old + jnp.sum(p, axis=0, keepdims=True)
    return p, alpha, m_new, l_new


def _flash_da_kernel(q_ref, k_ref, v_ref, lam_ref, g_ref, o_ref, acc1_ref, acc2_ref, *, n_t, lam_init):
    qi = pl.program_id(2)
    n_kv = jnp.where(qi == 0, 1, n_t)
    q = q_ref[0, 0]
    top = lax.broadcasted_iota(jnp.int32, q.shape, 0) < DA_HEAD_DIM
    zero = jnp.zeros_like(q)
    q1 = jnp.where(top, q, zero)
    q2 = jnp.where(top, zero, q)
    acc1_ref[...] = jnp.zeros_like(acc1_ref)
    acc2_ref[...] = jnp.zeros_like(acc2_ref)
    stat = jnp.full((1, TILE), NEG_BIG, F32)
    zl = jnp.zeros((1, TILE), F32)

    def body(j, carry):
        m1, l1, m2, l2 = carry
        k = k_ref[j, 0]
        vt = v_ref[j, 0]
        p1, a1, m1, l1 = _softmax_step(_dot(k, q1), m1, l1)
        p2, a2, m2, l2 = _softmax_step(_dot(k, q2), m2, l2)
        acc1_ref[...] = acc1_ref[...] * a1 + _dot(vt, p1.astype(BF16))
        acc2_ref[...] = acc2_ref[...] * a2 + _dot(vt, p2.astype(BF16))
        return m1, l1, m2, l2

    _, l1, _, l2 = lax.fori_loop(0, n_kv, body, (stat, zl, stat, zl))
    lp = lam_ref[...]
    lam = (jnp.exp(jnp.sum(lp[0:1] * lp[1:2], keepdims=True))
           - jnp.exp(jnp.sum(lp[2:3] * lp[3:4], keepdims=True)) + lam_init)
    o = acc1_ref[...] / l1 - lam * (acc2_ref[...] / l2)
    ms = jnp.mean(o * o, axis=0, keepdims=True)
    o = o * lax.rsqrt(ms + NORM_EPS) * g_ref[...] * (1.0 - lam_init)
    o_ref[...] = o.T.astype(o_ref.dtype)


def _flash_da(qda, kda, vda, lam_p, subln_g, batch, n_t, lam_init):
    dv = 2 * DA_HEAD_DIM
    m = batch * n_t * TILE
    kern = functools.partial(_flash_da_kernel, n_t=n_t, lam_init=lam_init)
    return pl.pallas_call(
        kern,
        grid=(batch, DA_HEADS, n_t),
        in_specs=[pl.BlockSpec((1, 1, dv, TILE), lambda b, h, i: (b * n_t + i, h, 0, 0)),
                  pl.BlockSpec((n_t, 1, TILE, dv), lambda b, h, i: (b, h, 0, 0)),
                  pl.BlockSpec((n_t, 1, dv, TILE), lambda b, h, i: (b, h, 0, 0)),
                  pl.BlockSpec((4, DA_HEAD_DIM), lambda b, h, i: (0, 0)),
                  pl.BlockSpec((dv, TILE), lambda b, h, i: (0, 0))],
        out_specs=pl.BlockSpec((TILE, dv), lambda b, h, i: (b * n_t + i, h)),
        out_shape=jax.ShapeDtypeStruct((m, DA_HEADS * dv), BF16),
        scratch_shapes=[pltpu.VMEM((dv, TILE), F32), pltpu.VMEM((dv, TILE), F32)],
        compiler_params=_params("parallel", "parallel", "arbitrary"),
    )(qda, kda, vda, lam_p, subln_g)


def _flash_gq_kernel(q_ref, k_ref, v_ref, o_ref, acc_ref, *, n_t):
    hk = pl.program_id(1)
    qi = pl.program_id(2)
    n_kv = jnp.where(qi == 0, 1, n_t)
    group = GQ_HEADS // GQ_KV_HEADS
    first = hk == 0
    qs = []
    for g in range(group):
        q = q_ref[0, g]
        zero = jnp.zeros_like(q)
        qs.append(jnp.concatenate([jnp.where(first, q, zero), jnp.where(first, zero, q)], axis=0))
    acc_ref[...] = jnp.zeros_like(acc_ref)
    stat = jnp.full((1, TILE), NEG_BIG, F32)
    zl = jnp.zeros((1, TILE), F32)

    def body(j, carry):
        k = k_ref[j]
        vt = v_ref[j, 0]
        out = []
        for g in range(group):
            m_g, l_g = carry[2 * g], carry[2 * g + 1]
            p, a, m_g, l_g = _softmax_step(_dot(k, qs[g]), m_g, l_g)
            acc_ref[g] = acc_ref[g] * a + _dot(vt, p.astype(BF16))
            out += [m_g, l_g]
        return tuple(out)

    fin = lax.fori_loop(0, n_kv, body, (stat, zl) * group)
    o = jnp.concatenate([acc_ref[g] / fin[2 * g + 1] for g in range(group)], axis=0)
    o_ref[...] = o.T.astype(o_ref.dtype)


def _flash_gq(qgq, kgq, vgq, batch, n_t):
    group = GQ_HEADS // GQ_KV_HEADS
    m = batch * n_t * TILE
    kern = functools.partial(_flash_gq_kernel, n_t=n_t)
    return pl.pallas_call(
        kern,
        grid=(batch, GQ_KV_HEADS, n_t),
        in_specs=[pl.BlockSpec((1, group, GQ_HEAD_DIM, TILE), lambda b, h, i: (b * n_t + i, h, 0, 0)),
                  pl.BlockSpec((n_t, TILE, GQ_KV_COLS), lambda b, h, i: (b, 0, 0)),
                  pl.BlockSpec((n_t, 1, GQ_HEAD_DIM, TILE), lambda b, h, i: (b, h, 0, 0))],
        out_specs=pl.BlockSpec((TILE, group * GQ_HEAD_DIM), lambda b, h, i: (b * n_t + i, h)),
        out_shape=jax.ShapeDtypeStruct((m, GQ_Q_COLS), BF16),
        scratch_shapes=[pltpu.VMEM((group, GQ_HEAD_DIM, TILE), F32)],
        compiler_params=_params("parallel", "parallel", "arbitrary"),
    )(qgq, kgq, vgq)


def _conv3(x, prev_row, next_row, w_ref):
    n = x.shape[0]
    rid = lax.broadcasted_iota(jnp.int32, x.shape, 0)
    x_m1 = jnp.where(rid == 0, prev_row, pltpu.roll(x, 1, 0))
    x_p1 = jnp.where(rid == n - 1, next_row, pltpu.roll(x, n - 1, 0))
    return x_m1 * w_ref[0:1, :] + x * w_ref[1:2, :] + x_p1 * w_ref[2:3, :]


def _halo_rows(prev_ref, next_ref, j, n_t):
    has_prev = (j >= 2).astype(F32)
    has_next = jnp.logical_and(j >= 1, j <= n_t - 2).astype(F32)
    prev_row = prev_ref[7:8, :].astype(F32) * has_prev
    next_row = next_ref[0:1, :].astype(F32) * has_next
    return prev_row, next_row


def _halo_specs(cols, col_blk, n_rows8):
    prev = pl.BlockSpec((8, cols), lambda s: (jnp.maximum(s * (TILE // 8) - 1, 0), col_blk))
    nxt = pl.BlockSpec((8, cols), lambda s: (jnp.minimum((s + 1) * (TILE // 8), n_rows8 - 1), col_blk))
    return prev, nxt


def _dn_prep_kernel(x_ref, prev_ref, next_ref, ba_ref, cw_ref, alog_ref, dtb_ref,
                    q_ref, k_ref, v_ref, gb_ref, *, n_t):
    j = pl.program_id(0) % n_t
    prev_row, next_row = _halo_rows(prev_ref, next_ref, j, n_t)
    y = _silu(_conv3(x_ref[...], prev_row, next_row, cw_ref))
    for h in range(DN_HEADS):
        c = h * DN_HEAD_DIM
        q = y[:, c:c + DN_HEAD_DIM]
        k = y[:, DN_WIDTH + c:DN_WIDTH + c + DN_HEAD_DIM]
        q_ref[:, c:c + DN_HEAD_DIM] = q * (lax.rsqrt(jnp.sum(q * q, axis=-1, keepdims=True) + NORM_EPS)
                                           * (DN_HEAD_DIM ** -0.5))
        k_ref[:, c:c + DN_HEAD_DIM] = k * lax.rsqrt(jnp.sum(k * k, axis=-1, keepdims=True) + NORM_EPS)
    v_ref[...] = y[:, 2 * DN_WIDTH:]
    ba = ba_ref[...]
    beta = _sigmoid(ba)
    z = ba + dtb_ref[...]
    softplus = jnp.maximum(z, 0.0) + jnp.log(1.0 + jnp.exp(-jnp.abs(z)))
    g = -jnp.exp(alog_ref[...]) * softplus
    col = lax.broadcasted_iota(jnp.int32, ba.shape, 1)
    gb_ref[...] = jnp.where(col < 2 * DN_HEADS, beta, g)


def _dn_prep(y_nat, conv_w, alog_tab, dtb_tab, n_t):
    m = y_nat.shape[0]
    w3 = 3 * DN_WIDTH
    prev, nxt = _halo_specs(w3, NAT_QKV_OFF // w3, m // 8)
    tok = lambda c: pl.BlockSpec((TILE, c), lambda s: (s, 0))
    return pl.pallas_call(
        functools.partial(_dn_prep_kernel, n_t=n_t),
        grid=(m // TILE,),
        in_specs=[pl.BlockSpec((TILE, w3), lambda s: (s, NAT_QKV_OFF // w3)), prev, nxt,
                  pl.BlockSpec((TILE, 128), lambda s: (s, NAT_BA_OFF // 128)),
                  pl.BlockSpec((3, w3), lambda s: (0, 0)),
                  pl.BlockSpec((1, 128), lambda s: (0, 0)),
                  pl.BlockSpec((1, 128), lambda s: (0, 0))],
        out_specs=(tok(DN_WIDTH), tok(DN_WIDTH), tok(DN_WIDTH), tok(128)),
        out_shape=(jax.ShapeDtypeStruct((m, DN_WIDTH), F32),) * 3 + (jax.ShapeDtypeStruct((m, 128), F32),),
        compiler_params=_params("parallel"),
    )(y_nat, y_nat, y_nat, y_nat, conv_w, alog_tab, dtb_tab)


def _dn_scan_kernel(q_ref, k_ref, v_ref, gb_ref, o_ref, s_ref, vnew_ref, *, reverse):
    t = pl.program_id(1)

    @pl.when(t == 0)
    def _():
        s_ref[...] = jnp.zeros_like(s_ref)

    d = 1 if reverse else 0
    n_chunk = TILE // CHUNK
    shift = CHUNK.bit_length() - 1
    ii = lax.broadcasted_iota(jnp.int32, (TILE, TILE), 0)
    jj = lax.broadcasted_iota(jnp.int32, (TILE, TILE), 1)
    same = (ii >> shift) == (jj >> shift)
    if reverse:
        incl = jnp.logical_and(same, ii <= jj)
        strict = jnp.logical_and(same, ii < jj)
    else:
        incl = jnp.logical_and(same, ii >= jj)
        strict = jnp.logical_and(same, ii > jj)
    gb = gb_ref[...]
    hi = lax.Precision.HIGHEST
    gcum = jnp.dot(incl.astype(F32), gb, preferred_element_type=F32, precision=hi)
    gtot = jnp.dot(same.astype(F32), gb, preferred_element_type=F32, precision=hi)
    gcum_t = gcum.T
    eye = (ii == jj).astype(F32)
    order = range(n_chunk - 1, -1, -1) if reverse else range(n_chunk)
    for h in range(DN_HEADS):
        cb = d * DN_HEADS + h
        cg = 2 * DN_HEADS + cb
        lanes = slice(h * DN_HEAD_DIM, (h + 1) * DN_HEAD_DIM)
        q = q_ref[:, lanes]
        k = k_ref[:, lanes]
        v = v_ref[:, lanes]
        beta = gb[:, cb:cb + 1]
        gc = gcum[:, cg:cg + 1]
        gr = gcum_t[cg:cg + 1, :]
        gl = gtot[:, cg:cg + 1]
        decay = jnp.where(incl, jnp.exp(jnp.where(incl, gc - gr, 0.0)), 0.0)
        kb = k * beta
        k16 = k.astype(BF16)
        a = jnp.where(strict, _dot_nt(kb.astype(BF16), k16) * decay, 0.0)
        tinv = eye - jnp.where((ii >> 1) == (jj >> 1), a, 0.0)
        lvl = 1
        while (2 << lvl) <= CHUNK:
            off = jnp.where(jnp.logical_and((ii >> (lvl + 1)) == (jj >> (lvl + 1)), (ii >> lvl) != (jj >> lvl)), a, 0.0)
            t16 = tinv.astype(BF16)
            tinv = tinv - _dot(t16, _dot(off.astype(BF16), t16).astype(BF16))
            lvl += 1
        egc = jnp.exp(gc)
        rhs = jnp.concatenate([v * beta, kb * egc], axis=1).astype(BF16)
        uw = _dot(tinv.astype(BF16), rhs)
        ktail = (k * jnp.exp(gl - gc)).astype(BF16)
        qk = jnp.where(incl, _dot_nt(q.astype(BF16), k16) * decay, 0.0)
        qd = (q * egc).astype(BF16)
        state = s_ref[h]
        inter = []
        for c in order:
            rows = slice(c * CHUNK, (c + 1) * CHUNK)
            s16 = state.astype(BF16)
            v_new = uw[rows, 0:DN_HEAD_DIM] - _dot(uw[rows, DN_HEAD_DIM:].astype(BF16), s16)
            vnew_ref[h, rows, :] = v_new
            inter.append((c, _dot(qd[rows], s16)))
            egl = jnp.exp(gl[c * CHUNK:c * CHUNK + 1, :])
            state = state * egl + _dot_tn(ktail[rows], v_new.astype(BF16))
        s_ref[h] = state
        inter = jnp.concatenate([x for _, x in sorted(inter, key=lambda cx: cx[0])], axis=0)
        o_ref[:, lanes] = inter + _dot(qk.astype(BF16), vnew_ref[h].astype(BF16))


def _dn_scan(q, k, v, gb, batch, n_t, reverse):
    m = q.shape[0]
    if reverse:
        tile = lambda b, t: (b * n_t + jnp.where(t == 0, 0, n_t - t), 0)
    else:
        tile = lambda b, t: (b * n_t + t, 0)
    tok = lambda c: pl.BlockSpec((TILE, c), tile)
    return pl.pallas_call(
        functools.partial(_dn_scan_kernel, reverse=reverse),
        grid=(batch, n_t),
        in_specs=[tok(DN_WIDTH), tok(DN_WIDTH), tok(DN_WIDTH), tok(128)],
        out_specs=tok(DN_WIDTH),
        out_shape=jax.ShapeDtypeStruct((m, DN_WIDTH), F32),
        scratch_shapes=[pltpu.VMEM((DN_HEADS, DN_HEAD_DIM, DN_HEAD_DIM), F32),
                        pltpu.VMEM((DN_HEADS, TILE, DN_HEAD_DIM), F32)],
        compiler_params=_params("parallel", "arbitrary"),
    )(q, k, v, gb)


def _merge_kernel(x_ref, oa_ref, ob_ref, ocf_ref, ocb_ref, z_ref, gate_ref, dng_ref, mod_ref,
                  wa_ref, wb_ref, wc_ref, wo_ref, o_ref):
    d = x_ref.shape[1]
    oc = ocf_ref[...] + ocb_ref[...]
    z = z_ref[...]
    parts = []
    for h in range(DN_HEADS):
        lanes = slice(h * DN_HEAD_DIM, (h + 1) * DN_HEAD_DIM)
        o = oc[:, lanes]
        ms = jnp.mean(o * o, axis=-1, keepdims=True)
        parts.append(o * lax.rsqrt(ms + NORM_EPS) * dng_ref[...] * _silu(z[:, lanes]))
    oc_n = jnp.concatenate(parts, axis=1).astype(BF16)
    gates = _sigmoid(gate_ref[...])
    mix = (gates[:, 0:d] * _dot(oa_ref[...], wa_ref[...])
           + gates[:, d:2 * d] * _dot(ob_ref[...], wb_ref[...])
           + gates[:, 2 * d:3 * d] * _dot(oc_n, wc_ref[...]))
    o_ref[...] = x_ref[...] + mod_ref[0] * _dot(mix.astype(BF16), wo_ref[...])


def _merge(x, oa, ob, ocf, ocb, y_nat, dn_g, mod_tab, wa, wb, wc, wo):
    m, d = x.shape
    tok = lambda c, cb=0: pl.BlockSpec((TILE, c), lambda s: (s, cb))
    full = lambda a: pl.BlockSpec(a.shape, lambda s: (0,) * a.ndim)
    return pl.pallas_call(
        _merge_kernel,
        grid=(m // TILE,),
        in_specs=[tok(d), tok(DA_COLS), tok(GQ_Q_COLS), tok(DN_WIDTH), tok(DN_WIDTH),
                  tok(DN_WIDTH, NAT_Z_OFF // DN_WIDTH), tok(3 * d, NAT_GATE_OFF),
                  full(dn_g), pl.BlockSpec((1, 1, d), lambda s: (s, 0, 0)),
                  full(wa), full(wb), full(wc), full(wo)],
        out_specs=tok(d),
        out_shape=jax.ShapeDtypeStruct((m, d), F32),
        compiler_params=_params("parallel"),
    )(x, oa, ob, ocf, ocb, y_nat, y_nat, dn_g, mod_tab, wa, wb, wc, wo)


def _ffn_out_kernel(x_ref, a_ref, prev_ref, next_ref, u_ref, cw_ref, cb_ref, mod_ref, w2_ref, o_ref, *, n_t):
    j = pl.program_id(0) % n_t
    prev_row, next_row = _halo_rows(prev_ref, next_ref, j, n_t)
    a = _conv3(a_ref[...].astype(F32), prev_row, next_row, cw_ref) + cb_ref[...]
    act = (_silu(a) * u_ref[...].astype(F32)).astype(BF16)
    o_ref[...] = x_ref[...] + mod_ref[0] * _dot(act, w2_ref[...])


def _ffn_out(x, au, conv_w, conv_b, mod_tab, w2, n_t):
    m, d = x.shape
    dff = w2.shape[0]
    prev, nxt = _halo_specs(dff, 0, m // 8)
    return pl.pallas_call(
        functools.partial(_ffn_out_kernel, n_t=n_t),
        grid=(m // TILE,),
        in_specs=[pl.BlockSpec((TILE, d), lambda s: (s, 0)),
                  pl.BlockSpec((TILE, dff), lambda s: (s, 0)), prev, nxt,
                  pl.BlockSpec((TILE, dff), lambda s: (s, 1)),
                  pl.BlockSpec((3, dff), lambda s: (0, 0)),
                  pl.BlockSpec((1, dff), lambda s: (0, 0)),
                  pl.BlockSpec((1, 1, d), lambda s: (s, 0, 0)),
                  pl.BlockSpec((dff, d), lambda s: (0, 0))],
        out_specs=pl.BlockSpec((TILE, d), lambda s: (s, 0)),
        out_shape=jax.ShapeDtypeStruct((m, d), F32),
        compiler_params=_params("parallel"),
    )(x, au, au, au, au, conv_w, conv_b.reshape(1, dff), mod_tab, w2)


def _rope_tables(n_lat):
    t = jnp.arange(n_lat, dtype=jnp.int32)
    row = (t // GRID_W).astype(F32)
    col = (t % GRID_W).astype(F32)
    d_axis = 32
    inv_freq = ROPE_THETA ** (-jnp.arange(0, d_axis, 2, dtype=F32) / d_axis)
    ang_r = row[None, :] * inv_freq[:, None]
    ang_c = col[None, :] * inv_freq[:, None]
    cr, sr, cc, sc = jnp.cos(ang_r), jnp.sin(ang_r), jnp.cos(ang_c), jnp.sin(ang_c)
    cos_lat = jnp.concatenate([cr, cr, cc, cc], axis=0)
    sin_lat = jnp.concatenate([-sr, sr, -sc, sc], axis=0)
    cos_t = jnp.concatenate([jnp.ones((64, TILE), F32), cos_lat], axis=1)
    sin_t = jnp.concatenate([jnp.zeros((64, TILE), F32), sin_lat], axis=1)
    return cos_t, sin_t


def kernel(x, c, ctx, c_ctx, w_mod, b_mod, norm1_g, w_in, da_qn_g, da_kn_g, da_lambda, da_subln_g, gq_qn_g, gq_kn_g, dn_conv_w, dn_a_log, dn_dt_bias, dn_norm_g, w_br_a, w_br_b, w_br_c, w_o, norm2_g, ffn_w1, ffn_conv_w, ffn_conv_b, ffn_w2):
    batch, n_lat, d = x.shape
    depth = w_mod.shape[0]
    dff = ffn_w2.shape[1]
    assert ctx.shape[1] == TILE and n_lat % TILE == 0 and n_lat % GRID_W == 0 and batch + 1 <= 8
    n_t = 1 + n_lat // TILE
    s_tiles = batch * n_t
    m = s_tiles * TILE

    xs = jnp.concatenate([ctx, x], axis=1).reshape(m, d)

    cond = jnp.zeros((8, d), F32).at[:batch].set(c).at[batch].set(c_ctx)
    mod = _modulation(cond, w_mod, b_mod)
    tile_id = jnp.arange(s_tiles)
    row_of_tile = jnp.where(tile_id % n_t == 0, batch, tile_id // n_t)
    mod_tabs = mod[:, row_of_tile, :].reshape(depth, s_tiles, 6, 1, d)

    cos_t, sin_t = _rope_tables(n_lat)
    bcast = lambda g: jnp.broadcast_to(g[:, None], (g.shape[0], TILE))

    o_qkv = ATTN_ROWS
    o_b = o_qkv + 3 * DN_WIDTH
    o_z = o_b + 4 * DN_HEADS
    o_g = o_z + DN_WIDTH
    wt_attn = jnp.swapaxes(w_in[:, :, :ATTN_ROWS], 1, 2).astype(BF16)
    w_nat = jnp.concatenate(
        [w_in[:, :, o_g:], w_in[:, :, o_qkv:o_b], w_in[:, :, o_z:o_g], w_in[:, :, o_b:o_z],
         jnp.zeros((depth, d, NAT_COLS - NAT_BA_OFF - 4 * DN_HEADS), F32)], axis=2).astype(BF16)
    pad_tab = lambda a: jnp.zeros((1, 128), F32).at[0, 2 * DN_HEADS:4 * DN_HEADS].set(a.reshape(-1))

    for l in range(depth):
        lam_init = 0.8 - 0.6 * math.exp(-0.3 * l)
        tab = lambda i: mod_tabs[l, :, i]

        h1 = _norm_mod(xs, norm1_g[l], tab(0), tab(1))
        y_t = _matmul_nt(wt_attn[l], h1, 768, F32)
        y_nat = _matmul_nn(h1, w_nat[l], 768, F32)

        gains = jnp.stack([bcast(da_qn_g[l]), bcast(da_kn_g[l]), bcast(gq_qn_g[l]), bcast(gq_kn_g[l])])
        qda, kda, vda, qgq, kgq, vgq = _attn_prep(y_t, cos_t, sin_t, gains, n_t)
        oa = _flash_da(qda, kda, vda, da_lambda[l], bcast(da_subln_g[l]), batch, n_t, lam_init)
        ob = _flash_gq(qgq, kgq, vgq, batch, n_t)

        dq, dk, dv, gb = _dn_prep(y_nat, dn_conv_w[l], pad_tab(dn_a_log[l]), pad_tab(dn_dt_bias[l]), n_t)
        ocf = _dn_scan(dq, dk, dv, gb, batch, n_t, False)
        ocb = _dn_scan(dq, dk, dv, gb, batch, n_t, True)

        xs = _merge(xs, oa, ob, ocf, ocb, y_nat, dn_norm_g[l].reshape(1, -1), tab(2),
                    w_br_a[l].astype(BF16), w_br_b[l].astype(BF16), w_br_c[l].astype(BF16), w_o[l].astype(BF16))

        h2 = _norm_mod(xs, norm2_g[l], tab(3), tab(4))
        au = _matmul_nn(h2, ffn_w1[l].astype(BF16), 512, BF16)
        xs = _ffn_out(xs, au, ffn_conv_w[l], ffn_conv_b[l], tab(5), ffn_w2[l].astype(BF16), n_t)

    return xs.reshape(batch, n_t * TILE, d)[:, TILE:, :]
```

```python
import functools
import math

import jax
import jax.numpy as jnp
from jax import lax
from jax.experimental import pallas as pl
from jax.experimental.pallas import tpu as pltpu

F32 = jnp.float32
BF16 = jnp.bfloat16

TILE = 256
CHUNK = 64
GRID_W = 64
ROPE_THETA = 10000.0
NORM_EPS = 1e-6
DA_HEADS, DA_HEAD_DIM = 4, 64
GQ_HEADS, GQ_KV_HEADS, GQ_HEAD_DIM = 8, 2, 64
DN_HEADS, DN_HEAD_DIM = 4, 128
LOG2E = 1.4426950408889634
NEG_BIG = -1e30
VMEM_LIMIT = 48 * 1024 * 1024

DA_COLS = DA_HEADS * 2 * DA_HEAD_DIM
GQ_Q_COLS = GQ_HEADS * GQ_HEAD_DIM
GQ_KV_COLS = GQ_KV_HEADS * GQ_HEAD_DIM
DN_WIDTH = DN_HEADS * DN_HEAD_DIM
ATTN_ROWS = 3 * DA_COLS + GQ_Q_COLS + 2 * GQ_KV_COLS
NAT_GATE_OFF, NAT_QKV_OFF, NAT_Z_OFF, NAT_BA_OFF = 0, 3072, 4608, 5120
NAT_COLS = 5376


def _params(*sem):
    return pltpu.CompilerParams(dimension_semantics=sem, vmem_limit_bytes=VMEM_LIMIT)


def _sigmoid(x):
    return 1.0 / (1.0 + jnp.exp(-x))


def _silu(x):
    return x * _sigmoid(x)


def _dot(a, b):
    return jnp.dot(a, b, preferred_element_type=F32)


def _dot_nt(a, b):
    return lax.dot_general(a, b, (((1,), (1,)), ((), ())), preferred_element_type=F32)


def _dot_tn(a, b):
    return lax.dot_general(a, b, (((0,), (0,)), ((), ())), preferred_element_type=F32)


def _mod_kernel(c_ref, w_ref, b_ref, o_ref):
    cond = _silu(c_ref[...])
    o_ref[0] = jnp.dot(cond, w_ref[0], preferred_element_type=F32,
                       precision=lax.Precision.HIGHEST) + b_ref[0]


def _modulation(cond, w_mod, b_mod):
    depth, d, n = w_mod.shape
    tn = 1536
    return pl.pallas_call(
        _mod_kernel,
        grid=(depth, n // tn),
        in_specs=[pl.BlockSpec((8, d), lambda l, j: (0, 0)),
                  pl.BlockSpec((1, d, tn), lambda l, j: (l, 0, j)),
                  pl.BlockSpec((1, 1, tn), lambda l, j: (l, 0, j))],
        out_specs=pl.BlockSpec((1, 8, tn), lambda l, j: (l, 0, j)),
        out_shape=jax.ShapeDtypeStruct((depth, 8, n), F32),
        compiler_params=_params("parallel", "parallel"),
        name="modulation",
    )(cond, w_mod, b_mod.reshape(depth, 1, n))


def _norm_mod_kernel(x_ref, g_ref, shift_ref, scale_ref, o_ref):
    x = x_ref[...]
    ms = jnp.mean(x * x, axis=-1, keepdims=True)
    y = x * lax.rsqrt(ms + NORM_EPS) * g_ref[...]
    o_ref[...] = (y * (1.0 + scale_ref[0]) + shift_ref[0]).astype(o_ref.dtype)


def _norm_mod(x, g, shift_tab, scale_tab):
    m, d = x.shape
    return pl.pallas_call(
        _norm_mod_kernel,
        grid=(m // TILE,),
        in_specs=[pl.BlockSpec((TILE, d), lambda s: (s, 0)),
                  pl.BlockSpec((1, d), lambda s: (0, 0)),
                  pl.BlockSpec((1, 1, d), lambda s: (s, 0, 0)),
                  pl.BlockSpec((1, 1, d), lambda s: (s, 0, 0))],
        out_specs=pl.BlockSpec((TILE, d), lambda s: (s, 0)),
        out_shape=jax.ShapeDtypeStruct((m, d), BF16),
        compiler_params=_params("parallel"),
        name="norm_mod",
    )(x, g.reshape(1, d), shift_tab, scale_tab)


def _mm_nn_kernel(a_ref, b_ref, o_ref):
    o_ref[...] = _dot(a_ref[...], b_ref[...]).astype(o_ref.dtype)


def _mm_nt_kernel(wt_ref, a_ref, o_ref):
    o_ref[...] = _dot_nt(wt_ref[...], a_ref[...]).astype(o_ref.dtype)


def _row_tile(m):
    for t in (1024, 512, 256):
        if m % t == 0:
            return t
    raise ValueError(m)


def _matmul_nn(a, b, tn, out_dtype):
    m, k = a.shape
    n = b.shape[1]
    tm = _row_tile(m)
    return pl.pallas_call(
        _mm_nn_kernel,
        grid=(m // tm, n // tn),
        in_specs=[pl.BlockSpec((tm, k), lambda i, j: (i, 0)),
                  pl.BlockSpec((k, tn), lambda i, j: (0, j))],
        out_specs=pl.BlockSpec((tm, tn), lambda i, j: (i, j)),
        out_shape=jax.ShapeDtypeStruct((m, n), out_dtype),
        compiler_params=_params("parallel", "parallel"),
        name="matmul_nn",
    )(a, b)


def _matmul_nt(wt, a, tn, out_dtype):
    m, k = a.shape
    n = wt.shape[0]
    tm = _row_tile(m)
    return pl.pallas_call(
        _mm_nt_kernel,
        grid=(m // tm, n // tn),
        in_specs=[pl.BlockSpec((tn, k), lambda i, j: (j, 0)),
                  pl.BlockSpec((tm, k), lambda i, j: (i, 0))],
        out_specs=pl.BlockSpec((tn, tm), lambda i, j: (j, i)),
        out_shape=jax.ShapeDtypeStruct((n, m), out_dtype),
        compiler_params=_params("parallel", "parallel"),
        name="matmul_nt",
    )(wt, a)


def _norm_rope_t(x, gain, cos, sin, scale):
    ms = jnp.mean(x * x, axis=0, keepdims=True)
    xn = x * lax.rsqrt(ms + NORM_EPS) * gain
    sw = jnp.concatenate([xn[16:32], xn[0:16], xn[48:64], xn[32:48]], axis=0)
    out = xn * cos + sw * sin
    return out * scale if scale != 1.0 else out


def _attn_prep_kernel(y_ref, cos_ref, sin_ref, gain_ref,
                      qda_ref, kda_ref, vda_ref, qgq_ref, kgq_ref, vgq_ref):
    cos = cos_ref[...]
    sin = sin_ref[...]
    qscale = (DA_HEAD_DIM ** -0.5) * LOG2E
    hd = DA_HEAD_DIM
    for h in range(DA_HEADS):
        halves_q, halves_k = [], []
        for half in range(2):
            r = h * 2 * hd + half * hd
            halves_q.append(_norm_rope_t(y_ref[r:r + hd, :], gain_ref[0], cos, sin, qscale))
            halves_k.append(_norm_rope_t(y_ref[DA_COLS + r:DA_COLS + r + hd, :], gain_ref[1], cos, sin, 1.0))
        qda_ref[0, h] = jnp.concatenate(halves_q, axis=0).astype(BF16)
        kda_ref[0, h] = jnp.concatenate(halves_k, axis=0).T.astype(BF16)
        r = 2 * DA_COLS + h * 2 * hd
        vda_ref[0, h] = y_ref[r:r + 2 * hd, :].astype(BF16)
    base = 3 * DA_COLS
    gscale = (GQ_HEAD_DIM ** -0.5) * LOG2E
    for h in range(GQ_HEADS):
        r = base + h * GQ_HEAD_DIM
        qgq_ref[0, h] = _norm_rope_t(y_ref[r:r + GQ_HEAD_DIM, :], gain_ref[2], cos, sin, gscale).astype(BF16)
    base += GQ_Q_COLS
    ks = [_norm_rope_t(y_ref[base + h * GQ_HEAD_DIM:base + (h + 1) * GQ_HEAD_DIM, :], gain_ref[3], cos, sin, 1.0)
          for h in range(GQ_KV_HEADS)]
    kgq_ref[0] = jnp.concatenate(ks, axis=0).T.astype(BF16)
    base += GQ_KV_COLS
    for h in range(GQ_KV_HEADS):
        r = base + h * GQ_HEAD_DIM
        vgq_ref[0, h] = y_ref[r:r + GQ_HEAD_DIM, :].astype(BF16)


def _attn_prep(y_t, cos_t, sin_t, gains, n_t):
    m = y_t.shape[1]
    s_tiles = m // TILE
    out_shape = (
        jax.ShapeDtypeStruct((s_tiles, DA_HEADS, 2 * DA_HEAD_DIM, TILE), BF16),
        jax.ShapeDtypeStruct((s_tiles, DA_HEADS, TILE, 2 * DA_HEAD_DIM), BF16),
        jax.ShapeDtypeStruct((s_tiles, DA_HEADS, 2 * DA_HEAD_DIM, TILE), BF16),
        jax.ShapeDtypeStruct((s_tiles, GQ_HEADS, GQ_HEAD_DIM, TILE), BF16),
        jax.ShapeDtypeStruct((s_tiles, TILE, GQ_KV_COLS), BF16),
        jax.ShapeDtypeStruct((s_tiles, GQ_KV_HEADS, GQ_HEAD_DIM, TILE), BF16),
    )
    blk4 = lambda sds: pl.BlockSpec((1,) + sds.shape[1:], lambda s: (s,) + (0,) * (len(sds.shape) - 1))
    return pl.pallas_call(
        _attn_prep_kernel,
        grid=(s_tiles,),
        in_specs=[pl.BlockSpec((ATTN_ROWS, TILE), lambda s: (0, s)),
                  pl.BlockSpec((GQ_HEAD_DIM, TILE), lambda s: (0, s % n_t)),
                  pl.BlockSpec((GQ_HEAD_DIM, TILE), lambda s: (0, s % n_t)),
                  pl.BlockSpec((4, GQ_HEAD_DIM, TILE), lambda s: (0, 0, 0))],
        out_specs=tuple(blk4(s) for s in out_shape),
        out_shape=out_shape,
        compiler_params=_params("parallel"),
        name="attn_prep",
    )(y_t, cos_t, sin_t, gains)


def _softmax_update(s_ref, idx, m_tile, m_old, l_old):
    m_new = jnp.maximum(m_old, m_tile)
    alpha = jnp.exp2(m_old - m_new)
    p = jnp.exp2(s_ref[idx] - m_new)
    l_new = alpha * l_old + jnp.sum(p, axis=0, keepdims=True)
    return p.astype(BF16), alpha, m_new, l_new


def _pipelined_kv_loop(n_kv, n_mat, scores, consume):
    stat = jnp.full((1, TILE), NEG_BIG, F32)
    zl = jnp.zeros((1, TILE), F32)
    stats0 = (stat, zl) * n_mat
    mt0 = scores(0, 0)

    def pair(i, carry):
        mt_a, stats = carry[:n_mat], carry[n_mat:]
        mt_b = scores(2 * i + 1, 1)
        stats = consume(2 * i, 0, mt_a, stats)
        mt_a = scores(2 * i + 2, 0)
        stats = consume(2 * i + 1, 1, mt_b, stats)
        return tuple(mt_a) + tuple(stats)

    carry = lax.fori_loop(0, (n_kv - 1) // 2, pair, tuple(mt0) + stats0)
    return consume(n_kv - 1, 0, carry[:n_mat], carry[n_mat:])


def _flash_da_kernel(q_ref, k_ref, v_ref, lam_ref, g_ref, o_ref, acc_ref, sa_ref, sb_ref, *, n_t, lam_init):
    qi = pl.program_id(2)
    n_kv = jnp.where(qi == 0, 1, n_t)
    q = q_ref[0, 0]
    top = lax.broadcasted_iota(jnp.int32, q.shape, 0) < DA_HEAD_DIM
    zero = jnp.zeros_like(q)
    qs = (jnp.where(top, q, zero), jnp.where(top, zero, q))
    acc_ref[...] = jnp.zeros_like(acc_ref)
    bufs = (sa_ref, sb_ref)

    def scores(j, buf):
        k = k_ref[j, 0]
        out = []
        for i in range(2):
            s = _dot(k, qs[i])
            bufs[buf][i] = s
            out.append(jnp.max(s, axis=0, keepdims=True))
        return out

    def consume(j, buf, mts, stats):
        vt = v_ref[j, 0]
        out = []
        for i in range(2):
            p, alpha, m_new, l_new = _softmax_update(bufs[buf], i, mts[i], stats[2 * i], stats[2 * i + 1])
            acc_ref[i] = acc_ref[i] * alpha + _dot(vt, p)
            out += [m_new, l_new]
        return tuple(out)

    _, l1, _, l2 = _pipelined_kv_loop(n_kv, 2, scores, consume)
    lp = lam_ref[...]
    lam = (jnp.exp(jnp.sum(lp[0:1] * lp[1:2], keepdims=True))
           - jnp.exp(jnp.sum(lp[2:3] * lp[3:4], keepdims=True)) + lam_init)
    o = acc_ref[0] / l1 - lam * (acc_ref[1] / l2)
    ms = jnp.mean(o * o, axis=0, keepdims=True)
    o = o * lax.rsqrt(ms + NORM_EPS) * g_ref[...] * (1.0 - lam_init)
    o_ref[...] = o.T.astype(o_ref.dtype)


def _flash_da(qda, kda, vda, lam_p, subln_g, batch, n_t, lam_init):
    dv = 2 * DA_HEAD_DIM
    m = batch * n_t * TILE
    kern = functools.partial(_flash_da_kernel, n_t=n_t, lam_init=lam_init)
    return pl.pallas_call(
        kern,
        grid=(batch, DA_HEADS, n_t),
        in_specs=[pl.BlockSpec((1, 1, dv, TILE), lambda b, h, i: (b * n_t + i, h, 0, 0)),
                  pl.BlockSpec((n_t, 1, TILE, dv), lambda b, h, i: (b, h, 0, 0)),
                  pl.BlockSpec((n_t, 1, dv, TILE), lambda b, h, i: (b, h, 0, 0)),
                  pl.BlockSpec((4, DA_HEAD_DIM), lambda b, h, i: (0, 0)),
                  pl.BlockSpec((dv, TILE), lambda b, h, i: (0, 0))],
        out_specs=pl.BlockSpec((TILE, dv), lambda b, h, i: (b * n_t + i, h)),
        out_shape=jax.ShapeDtypeStruct((m, DA_HEADS * dv), BF16),
        scratch_shapes=[pltpu.VMEM((2, dv, TILE), F32),
                        pltpu.VMEM((2, TILE, TILE), F32), pltpu.VMEM((2, TILE, TILE), F32)],
        compiler_params=_params("parallel", "parallel", "arbitrary"),
        name="flash_da",
    )(qda, kda, vda, lam_p, subln_g)


def _flash_gq_kernel(q_ref, k_ref, v_ref, o_ref, acc_ref, sa_ref, sb_ref, *, n_t):
    hk = pl.program_id(1)
    qi = pl.program_id(2)
    n_kv = jnp.where(qi == 0, 1, n_t)
    group = GQ_HEADS // GQ_KV_HEADS
    first = hk == 0
    qs = []
    for g in range(group):
        q = q_ref[0, g]
        zero = jnp.zeros_like(q)
        qs.append(jnp.concatenate([jnp.where(first, q, zero), jnp.where(first, zero, q)], axis=0))
    acc_ref[...] = jnp.zeros_like(acc_ref)
    bufs = (sa_ref, sb_ref)

    def scores(j, buf):
        k = k_ref[j]
        out = []
        for g in range(group):
            s = _dot(k, qs[g])
            bufs[buf][g] = s
            out.append(jnp.max(s, axis=0, keepdims=True))
        return out

    def consume(j, buf, mts, stats):
        vt = v_ref[j, 0]
        out = []
        for g in range(group):
            p, alpha, m_new, l_new = _softmax_update(bufs[buf], g, mts[g], stats[2 * g], stats[2 * g + 1])
            acc_ref[g] = acc_ref[g] * alpha + _dot(vt, p)
            out += [m_new, l_new]
        return tuple(out)

    fin = _pipelined_kv_loop(n_kv, group, scores, consume)
    o = jnp.concatenate([acc_ref[g] / fin[2 * g + 1] for g in range(group)], axis=0)
    o_ref[...] = o.T.astype(o_ref.dtype)


def _flash_gq(qgq, kgq, vgq, batch, n_t):
    group = GQ_HEADS // GQ_KV_HEADS
    m = batch * n_t * TILE
    kern = functools.partial(_flash_gq_kernel, n_t=n_t)
    return pl.pallas_call(
        kern,
        grid=(batch, GQ_KV_HEADS, n_t),
        in_specs=[pl.BlockSpec((1, group, GQ_HEAD_DIM, TILE), lambda b, h, i: (b * n_t + i, h, 0, 0)),
                  pl.BlockSpec((n_t, TILE, GQ_KV_COLS), lambda b, h, i: (b, 0, 0)),
                  pl.BlockSpec((n_t, 1, GQ_HEAD_DIM, TILE), lambda b, h, i: (b, h, 0, 0))],
        out_specs=pl.BlockSpec((TILE, group * GQ_HEAD_DIM), lambda b, h, i: (b * n_t + i, h)),
        out_shape=jax.ShapeDtypeStruct((m, GQ_Q_COLS), BF16),
        scratch_shapes=[pltpu.VMEM((group, GQ_HEAD_DIM, TILE), F32),
                        pltpu.VMEM((group, TILE, TILE), F32), pltpu.VMEM((group, TILE, TILE), F32)],
        compiler_params=_params("parallel", "parallel", "arbitrary"),
        name="flash_gq",
    )(qgq, kgq, vgq)


def _conv3(x, prev_row, next_row, w_ref):
    n = x.shape[0]
    rid = lax.broadcasted_iota(jnp.int32, x.shape, 0)
    x_m1 = jnp.where(rid == 0, prev_row, pltpu.roll(x, 1, 0))
    x_p1 = jnp.where(rid == n - 1, next_row, pltpu.roll(x, n - 1, 0))
    return x_m1 * w_ref[0:1, :] + x * w_ref[1:2, :] + x_p1 * w_ref[2:3, :]


def _halo_rows(prev_ref, next_ref, j, n_t):
    has_prev = (j >= 2).astype(F32)
    has_next = jnp.logical_and(j >= 1, j <= n_t - 2).astype(F32)
    prev_row = prev_ref[7:8, :].astype(F32) * has_prev
    next_row = next_ref[0:1, :].astype(F32) * has_next
    return prev_row, next_row


def _halo_specs(cols, col_blk, n_rows8):
    prev = pl.BlockSpec((8, cols), lambda s: (jnp.maximum(s * (TILE // 8) - 1, 0), col_blk))
    nxt = pl.BlockSpec((8, cols), lambda s: (jnp.minimum((s + 1) * (TILE // 8), n_rows8 - 1), col_blk))
    return prev, nxt


def _dn_prep_kernel(x_ref, prev_ref, next_ref, ba_ref, cw_ref, alog_ref, dtb_ref,
                    q_ref, k_ref, v_ref, gb_ref, *, n_t):
    j = pl.program_id(0) % n_t
    prev_row, next_row = _halo_rows(prev_ref, next_ref, j, n_t)
    y = _silu(_conv3(x_ref[...], prev_row, next_row, cw_ref))
    for h in range(DN_HEADS):
        c = h * DN_HEAD_DIM
        q = y[:, c:c + DN_HEAD_DIM]
        k = y[:, DN_WIDTH + c:DN_WIDTH + c + DN_HEAD_DIM]
        q_ref[:, c:c + DN_HEAD_DIM] = q * (lax.rsqrt(jnp.sum(q * q, axis=-1, keepdims=True) + NORM_EPS)
                                           * (DN_HEAD_DIM ** -0.5))
        k_ref[:, c:c + DN_HEAD_DIM] = k * lax.rsqrt(jnp.sum(k * k, axis=-1, keepdims=True) + NORM_EPS)
    v_ref[...] = y[:, 2 * DN_WIDTH:]
    ba = ba_ref[...]
    beta = _sigmoid(ba)
    z = ba + dtb_ref[...]
    softplus = jnp.maximum(z, 0.0) + jnp.log(1.0 + jnp.exp(-jnp.abs(z)))
    g = -jnp.exp(alog_ref[...]) * softplus
    col = lax.broadcasted_iota(jnp.int32, ba.shape, 1)
    gb_ref[...] = jnp.where(col < 2 * DN_HEADS, beta, g)


def _dn_prep(y_nat, conv_w, alog_tab, dtb_tab, n_t):
    m = y_nat.shape[0]
    w3 = 3 * DN_WIDTH
    prev, nxt = _halo_specs(w3, NAT_QKV_OFF // w3, m // 8)
    tok = lambda c: pl.BlockSpec((TILE, c), lambda s: (s, 0))
    return pl.pallas_call(
        functools.partial(_dn_prep_kernel, n_t=n_t),
        grid=(m // TILE,),
        in_specs=[pl.BlockSpec((TILE, w3), lambda s: (s, NAT_QKV_OFF // w3)), prev, nxt,
                  pl.BlockSpec((TILE, 128), lambda s: (s, NAT_BA_OFF // 128)),
                  pl.BlockSpec((3, w3), lambda s: (0, 0)),
                  pl.BlockSpec((1, 128), lambda s: (0, 0)),
                  pl.BlockSpec((1, 128), lambda s: (0, 0))],
        out_specs=(tok(DN_WIDTH), tok(DN_WIDTH), tok(DN_WIDTH), tok(128)),
        out_shape=(jax.ShapeDtypeStruct((m, DN_WIDTH), F32),) * 3 + (jax.ShapeDtypeStruct((m, 128), F32),),
        compiler_params=_params("parallel"),
        name="dn_prep",
    )(y_nat, y_nat, y_nat, y_nat, conv_w, alog_tab, dtb_tab)


def _dn_scan_kernel(qf_ref, kf_ref, vf_ref, gf_ref, qb_ref, kb_ref, vb_ref, gb_ref, of_ref, ob_ref,
                    s_ref, a16_ref, tinv_ref, t16_ref, w16_ref, qk16_ref, rhs16_ref, u_ref, uw16_ref,
                    kt16_ref, qd16_ref, kwb_ref, sc16_ref, vnew16_ref, inter_ref):
    t = pl.program_id(1)

    @pl.when(t == 0)
    def _():
        s_ref[...] = jnp.zeros_like(s_ref)

    n_chunk = TILE // CHUNK
    shift = CHUNK.bit_length() - 1
    dh = DN_HEAD_DIM
    ii = lax.broadcasted_iota(jnp.int32, (TILE, TILE), 0)
    jj = lax.broadcasted_iota(jnp.int32, (TILE, TILE), 1)
    same = (ii >> shift) == (jj >> shift)
    eye = (ii == jj).astype(F32)
    pair0 = (ii >> 1) == (jj >> 1)
    hi = lax.Precision.HIGHEST
    ins = ((qf_ref, kf_ref, vf_ref, gf_ref, of_ref), (qb_ref, kb_ref, vb_ref, gb_ref, ob_ref))
    chains = [(d, h) for d in range(2) for h in range(DN_HEADS)]

    gls = []
    for d in range(2):
        if d:
            incl = jnp.logical_and(same, ii <= jj)
            strict = jnp.logical_and(same, ii < jj)
        else:
            incl = jnp.logical_and(same, ii >= jj)
            strict = jnp.logical_and(same, ii > jj)
        gb = ins[d][3][...]
        gcum = jnp.dot(incl.astype(F32), gb, preferred_element_type=F32, precision=hi)
        gtot = jnp.dot(same.astype(F32), gb, preferred_element_type=F32, precision=hi)
        gcum_t = gcum.T
        for h in range(DN_HEADS):
            ch = d * DN_HEADS + h
            cg = 2 * DN_HEADS + ch
            lanes = slice(h * dh, (h + 1) * dh)
            q = ins[d][0][:, lanes]
            k = ins[d][1][:, lanes]
            v = ins[d][2][:, lanes]
            beta = gb[:, ch:ch + 1]
            gc = gcum[:, cg:cg + 1]
            gr = gcum_t[cg:cg + 1, :]
            gl = gtot[:, cg:cg + 1]
            gls.append(gl)
            decay = jnp.where(incl, jnp.exp(jnp.where(incl, gc - gr, 0.0)), 0.0)
            kb = k * beta
            k16 = k.astype(BF16)
            a = jnp.where(strict, _dot_nt(kb.astype(BF16), k16) * decay, 0.0)
            a16_ref[ch] = a.astype(BF16)
            t0 = eye - jnp.where(pair0, a, 0.0)
            tinv_ref[ch] = t0
            t16_ref[ch] = t0.astype(BF16)
            qk16_ref[ch] = jnp.where(incl, _dot_nt(q.astype(BF16), k16) * decay, 0.0).astype(BF16)
            egc = jnp.exp(gc)
            rhs16_ref[ch] = jnp.concatenate([v * beta, kb * egc], axis=1).astype(BF16)
            kt16_ref[ch] = (k * jnp.exp(gl - gc)).astype(BF16)
            qd16_ref[ch] = (q * egc).astype(BF16)

    lvl = 1
    while (2 << lvl) <= CHUNK:
        sib = jnp.logical_and((ii >> (lvl + 1)) == (jj >> (lvl + 1)), (ii >> lvl) != (jj >> lvl))
        for ch in range(len(chains)):
            w16_ref[ch] = _dot(a16_ref[ch], t16_ref[ch]).astype(BF16)
        for ch in range(len(chains)):
            tn = jnp.where(sib, -_dot(t16_ref[ch], w16_ref[ch]), tinv_ref[ch])
            tinv_ref[ch] = tn
            t16_ref[ch] = tn.astype(BF16)
        lvl += 1

    for ch in range(len(chains)):
        uw = _dot(t16_ref[ch], rhs16_ref[ch])
        u_ref[ch] = uw[:, 0:dh]
        uw16_ref[ch] = uw.astype(BF16)

    for ch in range(len(chains)):
        for c in range(n_chunk):
            rows = slice(c * CHUNK, (c + 1) * CHUNK)
            kwb_ref[ch, c] = _dot_tn(kt16_ref[ch, rows, :], uw16_ref[ch, rows, :])

    states = [s_ref[ch] for ch in range(len(chains))]
    for step in range(n_chunk):
        for ch, (d, h) in enumerate(chains):
            c = n_chunk - 1 - step if d else step
            s16 = states[ch].astype(BF16)
            sc16_ref[ch, c] = s16
            egl = jnp.exp(gls[ch][c * CHUNK:c * CHUNK + 1, :])
            states[ch] = (states[ch] * egl - _dot(kwb_ref[ch, c, :, dh:].astype(BF16), s16)
                          + kwb_ref[ch, c, :, 0:dh])
    for ch in range(len(chains)):
        s_ref[ch] = states[ch]

    for ch in range(len(chains)):
        for c in range(n_chunk):
            rows = slice(c * CHUNK, (c + 1) * CHUNK)
            s16 = sc16_ref[ch, c]
            vnew16_ref[ch, rows, :] = (u_ref[ch, rows, :] - _dot(uw16_ref[ch, rows, dh:], s16)).astype(BF16)
            inter_ref[ch, rows, :] = _dot(qd16_ref[ch, rows, :], s16)
    for ch, (d, h) in enumerate(chains):
        ins[d][4][:, h * dh:(h + 1) * dh] = inter_ref[ch] + _dot(qk16_ref[ch], vnew16_ref[ch])


def _dn_scan(q, k, v, gb, batch, n_t):
    m = q.shape[0]
    fwd = lambda b, t: (b * n_t + t, 0)
    bwd = lambda b, t: (b * n_t + jnp.where(t == 0, 0, n_t - t), 0)
    specs = lambda tile: [pl.BlockSpec((TILE, DN_WIDTH), tile)] * 3 + [pl.BlockSpec((TILE, 128), tile)]
    nc = 2 * DN_HEADS
    dh = DN_HEAD_DIM
    big = lambda dt: pltpu.VMEM((nc, TILE, TILE), dt)
    half = lambda dt: pltpu.VMEM((nc, TILE, dh), dt)
    return pl.pallas_call(
        _dn_scan_kernel,
        grid=(batch, n_t),
        in_specs=specs(fwd) + specs(bwd),
        out_specs=(pl.BlockSpec((TILE, DN_WIDTH), fwd), pl.BlockSpec((TILE, DN_WIDTH), bwd)),
        out_shape=(jax.ShapeDtypeStruct((m, DN_WIDTH), F32),) * 2,
        scratch_shapes=[pltpu.VMEM((nc, dh, dh), F32),
                        big(BF16), big(F32), big(BF16), big(BF16), big(BF16), big(BF16),
                        half(F32), big(BF16), half(BF16), half(BF16),
                        pltpu.VMEM((nc, TILE // CHUNK, dh, 2 * dh), F32),
                        pltpu.VMEM((nc, TILE // CHUNK, dh, dh), BF16),
                        half(BF16), half(F32)],
        compiler_params=_params("parallel", "arbitrary"),
        name="dn_scan",
    )(q, k, v, gb, q, k, v, gb)


def _merge_kernel(x_ref, oa_ref, ob_ref, ocf_ref, ocb_ref, z_ref, gate_ref, dng_ref, mod_ref,
                  wa_ref, wb_ref, wc_ref, wo_ref, o_ref):
    d = x_ref.shape[1]
    oc = ocf_ref[...] + ocb_ref[...]
    z = z_ref[...]
    parts = []
    for h in range(DN_HEADS):
        lanes = slice(h * DN_HEAD_DIM, (h + 1) * DN_HEAD_DIM)
        o = oc[:, lanes]
        ms = jnp.mean(o * o, axis=-1, keepdims=True)
        parts.append(o * lax.rsqrt(ms + NORM_EPS) * dng_ref[...] * _silu(z[:, lanes]))
    oc_n = jnp.concatenate(parts, axis=1).astype(BF16)
    gates = _sigmoid(gate_ref[...])
    mix = (gates[:, 0:d] * _dot(oa_ref[...], wa_ref[...])
           + gates[:, d:2 * d] * _dot(ob_ref[...], wb_ref[...])
           + gates[:, 2 * d:3 * d] * _dot(oc_n, wc_ref[...]))
    o_ref[...] = x_ref[...] + mod_ref[0] * _dot(mix.astype(BF16), wo_ref[...])


def _merge(x, oa, ob, ocf, ocb, y_nat, dn_g, mod_tab, wa, wb, wc, wo):
    m, d = x.shape
    tok = lambda c, cb=0: pl.BlockSpec((TILE, c), lambda s: (s, cb))
    full = lambda a: pl.BlockSpec(a.shape, lambda s: (0,) * a.ndim)
    return pl.pallas_call(
        _merge_kernel,
        grid=(m // TILE,),
        in_specs=[tok(d), tok(DA_COLS), tok(GQ_Q_COLS), tok(DN_WIDTH), tok(DN_WIDTH),
                  tok(DN_WIDTH, NAT_Z_OFF // DN_WIDTH), tok(3 * d, NAT_GATE_OFF),
                  full(dn_g), pl.BlockSpec((1, 1, d), lambda s: (s, 0, 0)),
                  full(wa), full(wb), full(wc), full(wo)],
        out_specs=tok(d),
        out_shape=jax.ShapeDtypeStruct((m, d), F32),
        compiler_params=_params("parallel"),
        name="merge",
    )(x, oa, ob, ocf, ocb, y_nat, y_nat, dn_g, mod_tab, wa, wb, wc, wo)


def _ffn_out_kernel(x_ref, a_ref, prev_ref, next_ref, u_ref, cw_ref, cb_ref, mod_ref, w2_ref, o_ref, *, n_t):
    j = pl.program_id(0) % n_t
    prev_row, next_row = _halo_rows(prev_ref, next_ref, j, n_t)
    a = _conv3(a_ref[...].astype(F32), prev_row, next_row, cw_ref) + cb_ref[...]
    act = (_silu(a) * u_ref[...].astype(F32)).astype(BF16)
    o_ref[...] = x_ref[...] + mod_ref[0] * _dot(act, w2_ref[...])


def _ffn_out(x, au, conv_w, conv_b, mod_tab, w2, n_t):
    m, d = x.shape
    dff = w2.shape[0]
    prev, nxt = _halo_specs(dff, 0, m // 8)
    return pl.pallas_call(
        functools.partial(_ffn_out_kernel, n_t=n_t),
        grid=(m // TILE,),
        in_specs=[pl.BlockSpec((TILE, d), lambda s: (s, 0)),
                  pl.BlockSpec((TILE, dff), lambda s: (s, 0)), prev, nxt,
                  pl.BlockSpec((TILE, dff), lambda s: (s, 1)),
                  pl.BlockSpec((3, dff), lambda s: (0, 0)),
                  pl.BlockSpec((1, dff), lambda s: (0, 0)),
                  pl.BlockSpec((1, 1, d), lambda s: (s, 0, 0)),
                  pl.BlockSpec((dff, d), lambda s: (0, 0))],
        out_specs=pl.BlockSpec((TILE, d), lambda s: (s, 0)),
        out_shape=jax.ShapeDtypeStruct((m, d), F32),
        compiler_params=_params("parallel"),
        name="ffn_out",
    )(x, au, au, au, au, conv_w, conv_b.reshape(1, dff), mod_tab, w2)


def _rope_tables(n_lat):
    t = jnp.arange(n_lat, dtype=jnp.int32)
    row = (t // GRID_W).astype(F32)
    col = (t % GRID_W).astype(F32)
    d_axis = 32
    inv_freq = ROPE_THETA ** (-jnp.arange(0, d_axis, 2, dtype=F32) / d_axis)
    ang_r = row[None, :] * inv_freq[:, None]
    ang_c = col[None, :] * inv_freq[:, None]
    cr, sr, cc, sc = jnp.cos(ang_r), jnp.sin(ang_r), jnp.cos(ang_c), jnp.sin(ang_c)
    cos_lat = jnp.concatenate([cr, cr, cc, cc], axis=0)
    sin_lat = jnp.concatenate([-sr, sr, -sc, sc], axis=0)
    cos_t = jnp.concatenate([jnp.ones((64, TILE), F32), cos_lat], axis=1)
    sin_t = jnp.concatenate([jnp.zeros((64, TILE), F32), sin_lat], axis=1)
    return cos_t, sin_t


def kernel(x, c, ctx, c_ctx, w_mod, b_mod, norm1_g, w_in, da_qn_g, da_kn_g, da_lambda, da_subln_g, gq_qn_g, gq_kn_g, dn_conv_w, dn_a_log, dn_dt_bias, dn_norm_g, w_br_a, w_br_b, w_br_c, w_o, norm2_g, ffn_w1, ffn_conv_w, ffn_conv_b, ffn_w2):
    batch, n_lat, d = x.shape
    depth = w_mod.shape[0]
    dff = ffn_w2.shape[1]
    assert ctx.shape[1] == TILE and n_lat % TILE == 0 and n_lat % GRID_W == 0 and batch + 1 <= 8
    n_t = 1 + n_lat // TILE
    assert n_t % 2 == 1, "the attention key loop handles key tiles in pairs plus one"
    s_tiles = batch * n_t
    m = s_tiles * TILE

    xs = jnp.concatenate([ctx, x], axis=1).reshape(m, d)

    cond = jnp.zeros((8, d), F32).at[:batch].set(c).at[batch].set(c_ctx)
    mod = _modulation(cond, w_mod, b_mod)
    tile_id = jnp.arange(s_tiles)
    row_of_tile = jnp.where(tile_id % n_t == 0, batch, tile_id // n_t)
    mod_tabs = mod[:, row_of_tile, :].reshape(depth, s_tiles, 6, 1, d)

    cos_t, sin_t = _rope_tables(n_lat)
    bcast = lambda g: jnp.broadcast_to(g[:, None], (g.shape[0], TILE))

    o_qkv = ATTN_ROWS
    o_b = o_qkv + 3 * DN_WIDTH
    o_z = o_b + 4 * DN_HEADS
    o_g = o_z + DN_WIDTH
    wt_attn = jnp.swapaxes(w_in[:, :, :ATTN_ROWS], 1, 2).astype(BF16)
    w_nat = jnp.concatenate(
        [w_in[:, :, o_g:], w_in[:, :, o_qkv:o_b], w_in[:, :, o_z:o_g], w_in[:, :, o_b:o_z],
         jnp.zeros((depth, d, NAT_COLS - NAT_BA_OFF - 4 * DN_HEADS), F32)], axis=2).astype(BF16)
    pad_tab = lambda a: jnp.zeros((1, 128), F32).at[0, 2 * DN_HEADS:4 * DN_HEADS].set(a.reshape(-1))

    for l in range(depth):
        lam_init = 0.8 - 0.6 * math.exp(-0.3 * l)
        tab = lambda i: mod_tabs[l, :, i]

        h1 = _norm_mod(xs, norm1_g[l], tab(0), tab(1))
        y_t = _matmul_nt(wt_attn[l], h1, 768, F32)
        y_nat = _matmul_nn(h1, w_nat[l], 768, F32)

        gains = jnp.stack([bcast(da_qn_g[l]), bcast(da_kn_g[l]), bcast(gq_qn_g[l]), bcast(gq_kn_g[l])])
        qda, kda, vda, qgq, kgq, vgq = _attn_prep(y_t, cos_t, sin_t, gains, n_t)
        oa = _flash_da(qda, kda, vda, da_lambda[l], bcast(da_subln_g[l]), batch, n_t, lam_init)
        ob = _flash_gq(qgq, kgq, vgq, batch, n_t)

        dq, dk, dv, gb = _dn_prep(y_nat, dn_conv_w[l], pad_tab(dn_a_log[l]), pad_tab(dn_dt_bias[l]), n_t)
        ocf, ocb = _dn_scan(dq, dk, dv, gb, batch, n_t)

        xs = _merge(xs, oa, ob, ocf, ocb, y_nat, dn_norm_g[l].reshape(1, -1), tab(2),
                    w_br_a[l].astype(BF16), w_br_b[l].astype(BF16), w_br_c[l].astype(BF16), w_o[l].astype(BF16))

        h2 = _norm_mod(xs, norm2_g[l], tab(3), tab(4))
        au = _matmul_nn(h2, ffn_w1[l].astype(BF16), 512, BF16)
        xs = _ffn_out(xs, au, ffn_conv_w[l], ffn_conv_b[l], tab(5), ffn_w2[l].astype(BF16), n_t)

    return xs.reshape(batch, n_t * TILE, d)[:, TILE:, :]
```

```python
import functools
import math

import jax
import jax.numpy as jnp
from jax import lax
from jax.experimental import pallas as pl
from jax.experimental.pallas import tpu as pltpu

F32 = jnp.float32
BF16 = jnp.bfloat16

TILE = 256
CHUNK = 64
GRID_W = 64
ROPE_THETA = 10000.0
NORM_EPS = 1e-6
DA_HEADS, DA_HEAD_DIM = 4, 64
GQ_HEADS, GQ_KV_HEADS, GQ_HEAD_DIM = 8, 2, 64
DN_HEADS, DN_HEAD_DIM = 4, 128
LOG2E = 1.4426950408889634
NEG_BIG = -1e30
ONES_ROWS = 16
VMEM_LIMIT = 48 * 1024 * 1024

DA_COLS = DA_HEADS * 2 * DA_HEAD_DIM
GQ_Q_COLS = GQ_HEADS * GQ_HEAD_DIM
GQ_KV_COLS = GQ_KV_HEADS * GQ_HEAD_DIM
DN_WIDTH = DN_HEADS * DN_HEAD_DIM
ATTN_ROWS = 3 * DA_COLS + GQ_Q_COLS + 2 * GQ_KV_COLS
NAT_GATE_OFF, NAT_QKV_OFF, NAT_Z_OFF, NAT_BA_OFF = 0, 3072, 4608, 5120
NAT_COLS = 5376


def _params(*sem):
    return pltpu.CompilerParams(dimension_semantics=sem, vmem_limit_bytes=VMEM_LIMIT)


def _sigmoid(x):
    return 1.0 / (1.0 + jnp.exp(-x))


def _silu(x):
    return x * _sigmoid(x)


def _dot(a, b):
    return jnp.dot(a, b, preferred_element_type=F32)


def _dot_nt(a, b):
    return lax.dot_general(a, b, (((1,), (1,)), ((), ())), preferred_element_type=F32)


def _dot_tn(a, b):
    return lax.dot_general(a, b, (((0,), (0,)), ((), ())), preferred_element_type=F32)


def _mod_kernel(c_ref, w_ref, b_ref, o_ref):
    cond = _silu(c_ref[...])
    o_ref[0] = jnp.dot(cond, w_ref[0], preferred_element_type=F32,
                       precision=lax.Precision.HIGHEST) + b_ref[0]


def _modulation(cond, w_mod, b_mod):
    depth, d, n = w_mod.shape
    tn = 1536
    return pl.pallas_call(
        _mod_kernel,
        grid=(depth, n // tn),
        in_specs=[pl.BlockSpec((8, d), lambda l, j: (0, 0)),
                  pl.BlockSpec((1, d, tn), lambda l, j: (l, 0, j)),
                  pl.BlockSpec((1, 1, tn), lambda l, j: (l, 0, j))],
        out_specs=pl.BlockSpec((1, 8, tn), lambda l, j: (l, 0, j)),
        out_shape=jax.ShapeDtypeStruct((depth, 8, n), F32),
        compiler_params=_params("parallel", "parallel"),
        name="modulation",
    )(cond, w_mod, b_mod.reshape(depth, 1, n))


def _norm_mod_kernel(x_ref, g_ref, shift_ref, scale_ref, o_ref):
    for i in range(x_ref.shape[0] // TILE):
        rows = slice(i * TILE, (i + 1) * TILE)
        x = x_ref[rows, :]
        ms = jnp.mean(x * x, axis=-1, keepdims=True)
        y = x * lax.rsqrt(ms + NORM_EPS) * g_ref[...]
        o_ref[rows, :] = (y * (1.0 + scale_ref[i]) + shift_ref[i]).astype(o_ref.dtype)


def _norm_mod(x, g, shift_tab, scale_tab):
    m, d = x.shape
    tm = _row_tile(m)
    sub = tm // TILE
    return pl.pallas_call(
        _norm_mod_kernel,
        grid=(m // tm,),
        in_specs=[pl.BlockSpec((tm, d), lambda s: (s, 0)),
                  pl.BlockSpec((1, d), lambda s: (0, 0)),
                  pl.BlockSpec((sub, 1, d), lambda s: (s, 0, 0)),
                  pl.BlockSpec((sub, 1, d), lambda s: (s, 0, 0))],
        out_specs=pl.BlockSpec((tm, d), lambda s: (s, 0)),
        out_shape=jax.ShapeDtypeStruct((m, d), BF16),
        compiler_params=_params("parallel"),
        name="norm_mod",
    )(x, g.reshape(1, d), shift_tab, scale_tab)


def _mm_nn_kernel(a_ref, b_ref, o_ref):
    o_ref[...] = _dot(a_ref[...], b_ref[...]).astype(o_ref.dtype)


def _mm_nt_kernel(wt_ref, a_ref, o_ref):
    o_ref[...] = _dot_nt(wt_ref[...], a_ref[...]).astype(o_ref.dtype)


def _row_tile(m):
    for t in (1024, 512, 256):
        if m % t == 0:
            return t
    raise ValueError(m)


def _matmul_nn(a, b, tn, out_dtype):
    m, k = a.shape
    n = b.shape[1]
    tm = _row_tile(m)
    return pl.pallas_call(
        _mm_nn_kernel,
        grid=(m // tm, n // tn),
        in_specs=[pl.BlockSpec((tm, k), lambda i, j: (i, 0)),
                  pl.BlockSpec((k, tn), lambda i, j: (0, j))],
        out_specs=pl.BlockSpec((tm, tn), lambda i, j: (i, j)),
        out_shape=jax.ShapeDtypeStruct((m, n), out_dtype),
        compiler_params=_params("parallel", "parallel"),
        name="matmul_nn",
    )(a, b)


def _matmul_nt(wt, a, tn, out_dtype):
    m, k = a.shape
    n = wt.shape[0]
    tm = _row_tile(m)
    return pl.pallas_call(
        _mm_nt_kernel,
        grid=(m // tm, n // tn),
        in_specs=[pl.BlockSpec((tn, k), lambda i, j: (j, 0)),
                  pl.BlockSpec((tm, k), lambda i, j: (i, 0))],
        out_specs=pl.BlockSpec((tn, tm), lambda i, j: (j, i)),
        out_shape=jax.ShapeDtypeStruct((n, m), out_dtype),
        compiler_params=_params("parallel", "parallel"),
        name="matmul_nt",
    )(wt, a)


def _norm_rope_t(x, gain, cos, sin, scale):
    ms = jnp.mean(x * x, axis=0, keepdims=True)
    xn = x * lax.rsqrt(ms + NORM_EPS) * gain
    sw = jnp.concatenate([xn[16:32], xn[0:16], xn[48:64], xn[32:48]], axis=0)
    out = xn * cos + sw * sin
    return out * scale if scale != 1.0 else out


def _attn_prep_kernel(y_ref, cos_ref, sin_ref, gain_ref,
                      qda_ref, kda_ref, vda_ref, qgq_ref, kgq_ref, vgq_ref):
    cos = cos_ref[...]
    sin = sin_ref[...]
    ones = jnp.ones((ONES_ROWS, TILE), F32)
    qscale = (DA_HEAD_DIM ** -0.5) * LOG2E
    hd = DA_HEAD_DIM
    for h in range(DA_HEADS):
        halves_q, halves_k = [], []
        for half in range(2):
            r = h * 2 * hd + half * hd
            halves_q.append(_norm_rope_t(y_ref[r:r + hd, :], gain_ref[0], cos, sin, qscale))
            halves_k.append(_norm_rope_t(y_ref[DA_COLS + r:DA_COLS + r + hd, :], gain_ref[1], cos, sin, 1.0))
        qda_ref[0, h] = jnp.concatenate(halves_q, axis=0).astype(BF16)
        kda_ref[0, h] = jnp.concatenate(halves_k, axis=0).T.astype(BF16)
        r = 2 * DA_COLS + h * 2 * hd
        vda_ref[0, h] = jnp.concatenate([y_ref[r:r + 2 * hd, :], ones], axis=0).astype(BF16)
    base = 3 * DA_COLS
    gscale = (GQ_HEAD_DIM ** -0.5) * LOG2E
    for h in range(GQ_HEADS):
        r = base + h * GQ_HEAD_DIM
        qgq_ref[0, h] = _norm_rope_t(y_ref[r:r + GQ_HEAD_DIM, :], gain_ref[2], cos, sin, gscale).astype(BF16)
    base += GQ_Q_COLS
    ks = [_norm_rope_t(y_ref[base + h * GQ_HEAD_DIM:base + (h + 1) * GQ_HEAD_DIM, :], gain_ref[3], cos, sin, 1.0)
          for h in range(GQ_KV_HEADS)]
    kgq_ref[0] = jnp.concatenate(ks, axis=0).T.astype(BF16)
    base += GQ_KV_COLS
    for h in range(GQ_KV_HEADS):
        r = base + h * GQ_HEAD_DIM
        vgq_ref[0, h] = jnp.concatenate([y_ref[r:r + GQ_HEAD_DIM, :], ones], axis=0).astype(BF16)


def _attn_prep(y_t, cos_t, sin_t, gains, n_t):
    m = y_t.shape[1]
    s_tiles = m // TILE
    out_shape = (
        jax.ShapeDtypeStruct((s_tiles, DA_HEADS, 2 * DA_HEAD_DIM, TILE), BF16),
        jax.ShapeDtypeStruct((s_tiles, DA_HEADS, TILE, 2 * DA_HEAD_DIM), BF16),
        jax.ShapeDtypeStruct((s_tiles, DA_HEADS, 2 * DA_HEAD_DIM + ONES_ROWS, TILE), BF16),
        jax.ShapeDtypeStruct((s_tiles, GQ_HEADS, GQ_HEAD_DIM, TILE), BF16),
        jax.ShapeDtypeStruct((s_tiles, TILE, GQ_KV_COLS), BF16),
        jax.ShapeDtypeStruct((s_tiles, GQ_KV_HEADS, GQ_HEAD_DIM + ONES_ROWS, TILE), BF16),
    )
    blk4 = lambda sds: pl.BlockSpec((1,) + sds.shape[1:], lambda s: (s,) + (0,) * (len(sds.shape) - 1))
    return pl.pallas_call(
        _attn_prep_kernel,
        grid=(s_tiles,),
        in_specs=[pl.BlockSpec((ATTN_ROWS, TILE), lambda s: (0, s)),
                  pl.BlockSpec((GQ_HEAD_DIM, TILE), lambda s: (0, s % n_t)),
                  pl.BlockSpec((GQ_HEAD_DIM, TILE), lambda s: (0, s % n_t)),
                  pl.BlockSpec((4, GQ_HEAD_DIM, TILE), lambda s: (0, 0, 0))],
        out_specs=tuple(blk4(s) for s in out_shape),
        out_shape=out_shape,
        compiler_params=_params("parallel"),
        name="attn_prep",
    )(y_t, cos_t, sin_t, gains)


def _softmax_update(s_ref, idx, m_tile, m_old):
    m_new = jnp.maximum(m_old, m_tile)
    alpha = jnp.exp2(m_old - m_new)
    p = jnp.exp2(s_ref[idx] - m_new)
    return p.astype(BF16), alpha, m_new


def _pipelined_kv_loop(n_kv, n_mat, scores, consume):
    m0 = (jnp.full((1, TILE), NEG_BIG, F32),) * n_mat
    mt0 = scores(0, 0)

    def pair(i, carry):
        mt_a, ms = carry[:n_mat], carry[n_mat:]
        mt_b = scores(2 * i + 1, 1)
        ms = consume(2 * i, 0, mt_a, ms)
        mt_a = scores(2 * i + 2, 0)
        ms = consume(2 * i + 1, 1, mt_b, ms)
        return tuple(mt_a) + tuple(ms)

    carry = lax.fori_loop(0, (n_kv - 1) // 2, pair, tuple(mt0) + m0)
    consume(n_kv - 1, 0, carry[:n_mat], carry[n_mat:])


DA_HEADS_PER_STEP = 2


def _flash_da_kernel(q_ref, k_ref, v_ref, lam_ref, g_ref, o_ref, acc_ref, sa_ref, sb_ref, *, n_t, lam_init):
    qi = pl.program_id(2)
    n_kv = jnp.where(qi == 0, 1, n_t)
    dv = 2 * DA_HEAD_DIM
    hps = DA_HEADS_PER_STEP
    qs = []
    for hh in range(hps):
        q = q_ref[0, hh]
        top = lax.broadcasted_iota(jnp.int32, q.shape, 0) < DA_HEAD_DIM
        zero = jnp.zeros_like(q)
        qs += [jnp.where(top, q, zero), jnp.where(top, zero, q)]
    acc_ref[...] = jnp.zeros_like(acc_ref)
    bufs = (sa_ref, sb_ref)

    def scores(j, buf):
        out = []
        for hh in range(hps):
            k = k_ref[j, hh]
            for i in range(2):
                s = _dot(k, qs[2 * hh + i])
                bufs[buf][2 * hh + i] = s
                out.append(jnp.max(s, axis=0, keepdims=True))
        return out

    def consume(j, buf, mts, ms):
        out = []
        for hh in range(hps):
            vt = v_ref[j, hh]
            for i in range(2):
                c = 2 * hh + i
                p, alpha, m_new = _softmax_update(bufs[buf], c, mts[c], ms[c])
                acc_ref[c] = acc_ref[c] * alpha + _dot(vt, p)
                out.append(m_new)
        return tuple(out)

    _pipelined_kv_loop(n_kv, 2 * hps, scores, consume)
    lp = lam_ref[...]
    lam = (jnp.exp(jnp.sum(lp[0:1] * lp[1:2], keepdims=True))
           - jnp.exp(jnp.sum(lp[2:3] * lp[3:4], keepdims=True)) + lam_init)
    for hh in range(hps):
        a1, a2 = acc_ref[2 * hh], acc_ref[2 * hh + 1]
        o = a1[0:dv] / a1[dv:dv + 1] - lam * (a2[0:dv] / a2[dv:dv + 1])
        ms = jnp.mean(o * o, axis=0, keepdims=True)
        o = o * lax.rsqrt(ms + NORM_EPS) * g_ref[...] * (1.0 - lam_init)
        o_ref[:, hh * dv:(hh + 1) * dv] = o.T.astype(o_ref.dtype)


def _flash_da(qda, kda, vda, lam_p, subln_g, batch, n_t, lam_init):
    dv = 2 * DA_HEAD_DIM
    hps = DA_HEADS_PER_STEP
    m = batch * n_t * TILE
    kern = functools.partial(_flash_da_kernel, n_t=n_t, lam_init=lam_init)
    return pl.pallas_call(
        kern,
        grid=(batch, DA_HEADS // hps, n_t),
        in_specs=[pl.BlockSpec((1, hps, dv, TILE), lambda b, h, i: (b * n_t + i, h, 0, 0)),
                  pl.BlockSpec((n_t, hps, TILE, dv), lambda b, h, i: (b, h, 0, 0)),
                  pl.BlockSpec((n_t, hps, dv + ONES_ROWS, TILE), lambda b, h, i: (b, h, 0, 0)),
                  pl.BlockSpec((4, DA_HEAD_DIM), lambda b, h, i: (0, 0)),
                  pl.BlockSpec((dv, TILE), lambda b, h, i: (0, 0))],
        out_specs=pl.BlockSpec((TILE, hps * dv), lambda b, h, i: (b * n_t + i, h)),
        out_shape=jax.ShapeDtypeStruct((m, DA_HEADS * dv), BF16),
        scratch_shapes=[pltpu.VMEM((2 * hps, dv + ONES_ROWS, TILE), F32),
                        pltpu.VMEM((2 * hps, TILE, TILE), F32), pltpu.VMEM((2 * hps, TILE, TILE), F32)],
        compiler_params=_params("parallel", "parallel", "arbitrary"),
        name="flash_da",
    )(qda, kda, vda, lam_p, subln_g)


def _flash_gq_kernel(q_ref, k_ref, v_ref, o_ref, acc_ref, sa_ref, sb_ref, *, n_t):
    hk = pl.program_id(1)
    qi = pl.program_id(2)
    n_kv = jnp.where(qi == 0, 1, n_t)
    group = GQ_HEADS // GQ_KV_HEADS
    dv = GQ_HEAD_DIM
    first = hk == 0
    qs = []
    for g in range(group):
        q = q_ref[0, g]
        zero = jnp.zeros_like(q)
        qs.append(jnp.concatenate([jnp.where(first, q, zero), jnp.where(first, zero, q)], axis=0))
    acc_ref[...] = jnp.zeros_like(acc_ref)
    bufs = (sa_ref, sb_ref)

    def scores(j, buf):
        k = k_ref[j]
        out = []
        for g in range(group):
            s = _dot(k, qs[g])
            bufs[buf][g] = s
            out.append(jnp.max(s, axis=0, keepdims=True))
        return out

    def consume(j, buf, mts, ms):
        vt = v_ref[j, 0]
        out = []
        for g in range(group):
            p, alpha, m_new = _softmax_update(bufs[buf], g, mts[g], ms[g])
            acc_ref[g] = acc_ref[g] * alpha + _dot(vt, p)
            out.append(m_new)
        return tuple(out)

    _pipelined_kv_loop(n_kv, group, scores, consume)
    o = jnp.concatenate([acc_ref[g, 0:dv, :] / acc_ref[g, dv:dv + 1, :] for g in range(group)], axis=0)
    o_ref[...] = o.T.astype(o_ref.dtype)


def _flash_gq(qgq, kgq, vgq, batch, n_t):
    group = GQ_HEADS // GQ_KV_HEADS
    m = batch * n_t * TILE
    kern = functools.partial(_flash_gq_kernel, n_t=n_t)
    return pl.pallas_call(
        kern,
        grid=(batch, GQ_KV_HEADS, n_t),
        in_specs=[pl.BlockSpec((1, group, GQ_HEAD_DIM, TILE), lambda b, h, i: (b * n_t + i, h, 0, 0)),
                  pl.BlockSpec((n_t, TILE, GQ_KV_COLS), lambda b, h, i: (b, 0, 0)),
                  pl.BlockSpec((n_t, 1, GQ_HEAD_DIM + ONES_ROWS, TILE), lambda b, h, i: (b, h, 0, 0))],
        out_specs=pl.BlockSpec((TILE, group * GQ_HEAD_DIM), lambda b, h, i: (b * n_t + i, h)),
        out_shape=jax.ShapeDtypeStruct((m, GQ_Q_COLS), BF16),
        scratch_shapes=[pltpu.VMEM((group, GQ_HEAD_DIM + ONES_ROWS, TILE), F32),
                        pltpu.VMEM((group, TILE, TILE), F32), pltpu.VMEM((group, TILE, TILE), F32)],
        compiler_params=_params("parallel", "parallel", "arbitrary"),
        name="flash_gq",
    )(qgq, kgq, vgq)


def _conv3(x, prev_row, next_row, w_ref):
    n = x.shape[0]
    rid = lax.broadcasted_iota(jnp.int32, x.shape, 0)
    x_m1 = jnp.where(rid == 0, prev_row, pltpu.roll(x, 1, 0))
    x_p1 = jnp.where(rid == n - 1, next_row, pltpu.roll(x, n - 1, 0))
    return x_m1 * w_ref[0:1, :] + x * w_ref[1:2, :] + x_p1 * w_ref[2:3, :]


def _halo_rows(prev_ref, next_ref, j, n_t):
    has_prev = (j >= 2).astype(F32)
    has_next = jnp.logical_and(j >= 1, j <= n_t - 2).astype(F32)
    prev_row = prev_ref[7:8, :].astype(F32) * has_prev
    next_row = next_ref[0:1, :].astype(F32) * has_next
    return prev_row, next_row


def _halo_specs(cols, col_blk, n_rows8):
    prev = pl.BlockSpec((8, cols), lambda s: (jnp.maximum(s * (TILE // 8) - 1, 0), col_blk))
    nxt = pl.BlockSpec((8, cols), lambda s: (jnp.minimum((s + 1) * (TILE // 8), n_rows8 - 1), col_blk))
    return prev, nxt


def _dn_prep_kernel(x_ref, prev_ref, next_ref, ba_ref, cw_ref, alog_ref, dtb_ref,
                    q_ref, k_ref, v_ref, gb_ref, *, n_t):
    j = pl.program_id(0) % n_t
    prev_row, next_row = _halo_rows(prev_ref, next_ref, j, n_t)
    y = _silu(_conv3(x_ref[...], prev_row, next_row, cw_ref))
    for h in range(DN_HEADS):
        c = h * DN_HEAD_DIM
        q = y[:, c:c + DN_HEAD_DIM]
        k = y[:, DN_WIDTH + c:DN_WIDTH + c + DN_HEAD_DIM]
        q_ref[:, c:c + DN_HEAD_DIM] = q * (lax.rsqrt(jnp.sum(q * q, axis=-1, keepdims=True) + NORM_EPS)
                                           * (DN_HEAD_DIM ** -0.5))
        k_ref[:, c:c + DN_HEAD_DIM] = k * lax.rsqrt(jnp.sum(k * k, axis=-1, keepdims=True) + NORM_EPS)
    v_ref[...] = y[:, 2 * DN_WIDTH:]
    ba = ba_ref[...]
    beta = _sigmoid(ba)
    z = ba + dtb_ref[...]
    softplus = jnp.maximum(z, 0.0) + jnp.log(1.0 + jnp.exp(-jnp.abs(z)))
    g = -jnp.exp(alog_ref[...]) * softplus
    col = lax.broadcasted_iota(jnp.int32, ba.shape, 1)
    gb_ref[...] = jnp.where(col < 2 * DN_HEADS, beta, g)


def _dn_prep(y_nat, conv_w, alog_tab, dtb_tab, n_t):
    m = y_nat.shape[0]
    w3 = 3 * DN_WIDTH
    prev, nxt = _halo_specs(w3, NAT_QKV_OFF // w3, m // 8)
    tok = lambda c: pl.BlockSpec((TILE, c), lambda s: (s, 0))
    return pl.pallas_call(
        functools.partial(_dn_prep_kernel, n_t=n_t),
        grid=(m // TILE,),
        in_specs=[pl.BlockSpec((TILE, w3), lambda s: (s, NAT_QKV_OFF // w3)), prev, nxt,
                  pl.BlockSpec((TILE, 128), lambda s: (s, NAT_BA_OFF // 128)),
                  pl.BlockSpec((3, w3), lambda s: (0, 0)),
                  pl.BlockSpec((1, 128), lambda s: (0, 0)),
                  pl.BlockSpec((1, 128), lambda s: (0, 0))],
        out_specs=(tok(DN_WIDTH), tok(DN_WIDTH), tok(DN_WIDTH), tok(128)),
        out_shape=(jax.ShapeDtypeStruct((m, DN_WIDTH), F32),) * 3 + (jax.ShapeDtypeStruct((m, 128), F32),),
        compiler_params=_params("parallel"),
        name="dn_prep",
    )(y_nat, y_nat, y_nat, y_nat, conv_w, alog_tab, dtb_tab)


def _dn_scan_kernel(qf_ref, kf_ref, vf_ref, gf_ref, qb_ref, kb_ref, vb_ref, gb_ref, of_ref, ob_ref,
                    s_ref, a16_ref, tinv_ref, t16_ref, w16_ref, qk16_ref, rhs16_ref, u_ref, uw16_ref,
                    kt16_ref, qd16_ref, kwb_ref, sc16_ref, vnew16_ref, inter_ref):
    t = pl.program_id(1)

    @pl.when(t == 0)
    def _():
        s_ref[...] = jnp.zeros_like(s_ref)

    n_chunk = TILE // CHUNK
    shift = CHUNK.bit_length() - 1
    dh = DN_HEAD_DIM
    ii = lax.broadcasted_iota(jnp.int32, (TILE, TILE), 0)
    jj = lax.broadcasted_iota(jnp.int32, (TILE, TILE), 1)
    same = (ii >> shift) == (jj >> shift)
    eye = (ii == jj).astype(F32)
    pair0 = (ii >> 1) == (jj >> 1)
    hi = lax.Precision.HIGHEST
    ins = ((qf_ref, kf_ref, vf_ref, gf_ref, of_ref), (qb_ref, kb_ref, vb_ref, gb_ref, ob_ref))
    chains = [(d, h) for d in range(2) for h in range(DN_HEADS)]

    gls = []
    for d in range(2):
        if d:
            incl = jnp.logical_and(same, ii <= jj)
            strict = jnp.logical_and(same, ii < jj)
        else:
            incl = jnp.logical_and(same, ii >= jj)
            strict = jnp.logical_and(same, ii > jj)
        gb = ins[d][3][...]
        gcum = jnp.dot(incl.astype(F32), gb, preferred_element_type=F32, precision=hi)
        gtot = jnp.dot(same.astype(F32), gb, preferred_element_type=F32, precision=hi)
        gcum_t = gcum.T
        for h in range(DN_HEADS):
            ch = d * DN_HEADS + h
            cg = 2 * DN_HEADS + ch
            lanes = slice(h * dh, (h + 1) * dh)
            q = ins[d][0][:, lanes]
            k = ins[d][1][:, lanes]
            v = ins[d][2][:, lanes]
            beta = gb[:, ch:ch + 1]
            gc = gcum[:, cg:cg + 1]
            gr = gcum_t[cg:cg + 1, :]
            gl = gtot[:, cg:cg + 1]
            gls.append(gl)
            decay = jnp.where(incl, jnp.exp(jnp.where(incl, gc - gr, 0.0)), 0.0)
            kb = k * beta
            k16 = k.astype(BF16)
            a = jnp.where(strict, _dot_nt(kb.astype(BF16), k16) * decay, 0.0)
            a16_ref[ch] = a.astype(BF16)
            t0 = eye - jnp.where(pair0, a, 0.0)
            tinv_ref[ch] = t0
            t16_ref[ch] = t0.astype(BF16)
            qk16_ref[ch] = jnp.where(incl, _dot_nt(q.astype(BF16), k16) * decay, 0.0).astype(BF16)
            egc = jnp.exp(gc)
            rhs16_ref[ch] = jnp.concatenate([v * beta, kb * egc], axis=1).astype(BF16)
            kt16_ref[ch] = (k * jnp.exp(gl - gc)).astype(BF16)
            qd16_ref[ch] = (q * egc).astype(BF16)

    lvl = 1
    while (2 << lvl) <= CHUNK:
        sib = jnp.logical_and((ii >> (lvl + 1)) == (jj >> (lvl + 1)), (ii >> lvl) != (jj >> lvl))
        for ch in range(len(chains)):
            w16_ref[ch] = _dot(a16_ref[ch], t16_ref[ch]).astype(BF16)
        for ch in range(len(chains)):
            tn = jnp.where(sib, -_dot(t16_ref[ch], w16_ref[ch]), tinv_ref[ch])
            tinv_ref[ch] = tn
            t16_ref[ch] = tn.astype(BF16)
        lvl += 1

    for ch in range(len(chains)):
        uw = _dot(t16_ref[ch], rhs16_ref[ch])
        u_ref[ch] = uw[:, 0:dh]
        uw16_ref[ch] = uw.astype(BF16)

    for ch in range(len(chains)):
        for c in range(n_chunk):
            rows = slice(c * CHUNK, (c + 1) * CHUNK)
            kwb_ref[ch, c] = _dot_tn(kt16_ref[ch, rows, :], uw16_ref[ch, rows, :])

    states = [s_ref[ch] for ch in range(len(chains))]
    for step in range(n_chunk):
        for ch, (d, h) in enumerate(chains):
            c = n_chunk - 1 - step if d else step
            s16 = states[ch].astype(BF16)
            sc16_ref[ch, c] = s16
            egl = jnp.exp(gls[ch][c * CHUNK:c * CHUNK + 1, :])
            states[ch] = (states[ch] * egl - _dot(kwb_ref[ch, c, :, dh:].astype(BF16), s16)
                          + kwb_ref[ch, c, :, 0:dh])
    for ch in range(len(chains)):
        s_ref[ch] = states[ch]

    for ch in range(len(chains)):
        for c in range(n_chunk):
            rows = slice(c * CHUNK, (c + 1) * CHUNK)
            s16 = sc16_ref[ch, c]
            vnew16_ref[ch, rows, :] = (u_ref[ch, rows, :] - _dot(uw16_ref[ch, rows, dh:], s16)).astype(BF16)
            inter_ref[ch, rows, :] = _dot(qd16_ref[ch, rows, :], s16)
    for ch, (d, h) in enumerate(chains):
        ins[d][4][:, h * dh:(h + 1) * dh] = inter_ref[ch] + _dot(qk16_ref[ch], vnew16_ref[ch])


def _dn_scan(q, k, v, gb, batch, n_t):
    m = q.shape[0]
    fwd = lambda b, t: (b * n_t + t, 0)
    bwd = lambda b, t: (b * n_t + jnp.where(t == 0, 0, n_t - t), 0)
    specs = lambda tile: [pl.BlockSpec((TILE, DN_WIDTH), tile)] * 3 + [pl.BlockSpec((TILE, 128), tile)]
    nc = 2 * DN_HEADS
    dh = DN_HEAD_DIM
    big = lambda dt: pltpu.VMEM((nc, TILE, TILE), dt)
    half = lambda dt: pltpu.VMEM((nc, TILE, dh), dt)
    return pl.pallas_call(
        _dn_scan_kernel,
        grid=(batch, n_t),
        in_specs=specs(fwd) + specs(bwd),
        out_specs=(pl.BlockSpec((TILE, DN_WIDTH), fwd), pl.BlockSpec((TILE, DN_WIDTH), bwd)),
        out_shape=(jax.ShapeDtypeStruct((m, DN_WIDTH), F32),) * 2,
        scratch_shapes=[pltpu.VMEM((nc, dh, dh), F32),
                        big(BF16), big(F32), big(BF16), big(BF16), big(BF16), big(BF16),
                        half(F32), big(BF16), half(BF16), half(BF16),
                        pltpu.VMEM((nc, TILE // CHUNK, dh, 2 * dh), F32),
                        pltpu.VMEM((nc, TILE // CHUNK, dh, dh), BF16),
                        half(BF16), half(F32)],
        compiler_params=_params("parallel", "arbitrary"),
        name="dn_scan",
    )(q, k, v, gb, q, k, v, gb)


def _merge_kernel(x_ref, oa_ref, ob_ref, ocf_ref, ocb_ref, z_ref, gate_ref, dng_ref, mod_ref,
                  wa_ref, wb_ref, wc_ref, wo_ref, o_ref):
    d = x_ref.shape[1]
    oc = ocf_ref[...] + ocb_ref[...]
    z = z_ref[...]
    parts = []
    for h in range(DN_HEADS):
        lanes = slice(h * DN_HEAD_DIM, (h + 1) * DN_HEAD_DIM)
        o = oc[:, lanes]
        ms = jnp.mean(o * o, axis=-1, keepdims=True)
        parts.append(o * lax.rsqrt(ms + NORM_EPS) * dng_ref[...] * _silu(z[:, lanes]))
    oc_n = jnp.concatenate(parts, axis=1).astype(BF16)
    gates = _sigmoid(gate_ref[...])
    mix = (gates[:, 0:d] * _dot(oa_ref[...], wa_ref[...])
           + gates[:, d:2 * d] * _dot(ob_ref[...], wb_ref[...])
           + gates[:, 2 * d:3 * d] * _dot(oc_n, wc_ref[...]))
    o_ref[...] = x_ref[...] + mod_ref[0] * _dot(mix.astype(BF16), wo_ref[...])


def _merge(x, oa, ob, ocf, ocb, y_nat, dn_g, mod_tab, wa, wb, wc, wo):
    m, d = x.shape
    tok = lambda c, cb=0: pl.BlockSpec((TILE, c), lambda s: (s, cb))
    full = lambda a: pl.BlockSpec(a.shape, lambda s: (0,) * a.ndim)
    return pl.pallas_call(
        _merge_kernel,
        grid=(m // TILE,),
        in_specs=[tok(d), tok(DA_COLS), tok(GQ_Q_COLS), tok(DN_WIDTH), tok(DN_WIDTH),
                  tok(DN_WIDTH, NAT_Z_OFF // DN_WIDTH), tok(3 * d, NAT_GATE_OFF),
                  full(dn_g), pl.BlockSpec((1, 1, d), lambda s: (s, 0, 0)),
                  full(wa), full(wb), full(wc), full(wo)],
        out_specs=tok(d),
        out_shape=jax.ShapeDtypeStruct((m, d), F32),
        compiler_params=_params("parallel"),
        name="merge",
    )(x, oa, ob, ocf, ocb, y_nat, y_nat, dn_g, mod_tab, wa, wb, wc, wo)


def _ffn_out_kernel(x_ref, a_ref, prev_ref, next_ref, u_ref, cw_ref, cb_ref, mod_ref, w2_ref, o_ref, *, n_t):
    j = pl.program_id(0) % n_t
    prev_row, next_row = _halo_rows(prev_ref, next_ref, j, n_t)
    a = _conv3(a_ref[...].astype(F32), prev_row, next_row, cw_ref) + cb_ref[...]
    act = (_silu(a) * u_ref[...].astype(F32)).astype(BF16)
    o_ref[...] = x_ref[...] + mod_ref[0] * _dot(act, w2_ref[...])


def _ffn_out(x, au, conv_w, conv_b, mod_tab, w2, n_t):
    m, d = x.shape
    dff = w2.shape[0]
    prev, nxt = _halo_specs(dff, 0, m // 8)
    return pl.pallas_call(
        functools.partial(_ffn_out_kernel, n_t=n_t),
        grid=(m // TILE,),
        in_specs=[pl.BlockSpec((TILE, d), lambda s: (s, 0)),
                  pl.BlockSpec((TILE, dff), lambda s: (s, 0)), prev, nxt,
                  pl.BlockSpec((TILE, dff), lambda s: (s, 1)),
                  pl.BlockSpec((3, dff), lambda s: (0, 0)),
                  pl.BlockSpec((1, dff), lambda s: (0, 0)),
                  pl.BlockSpec((1, 1, d), lambda s: (s, 0, 0)),
                  pl.BlockSpec((dff, d), lambda s: (0, 0))],
        out_specs=pl.BlockSpec((TILE, d), lambda s: (s, 0)),
        out_shape=jax.ShapeDtypeStruct((m, d), F32),
        compiler_params=_params("parallel"),
        name="ffn_out",
    )(x, au, au, au, au, conv_w, conv_b.reshape(1, dff), mod_tab, w2)


def _rope_tables(n_lat):
    t = jnp.arange(n_lat, dtype=jnp.int32)
    row = (t // GRID_W).astype(F32)
    col = (t % GRID_W).astype(F32)
    d_axis = 32
    inv_freq = ROPE_THETA ** (-jnp.arange(0, d_axis, 2, dtype=F32) / d_axis)
    ang_r = row[None, :] * inv_freq[:, None]
    ang_c = col[None, :] * inv_freq[:, None]
    cr, sr, cc, sc = jnp.cos(ang_r), jnp.sin(ang_r), jnp.cos(ang_c), jnp.sin(ang_c)
    cos_lat = jnp.concatenate([cr, cr, cc, cc], axis=0)
    sin_lat = jnp.concatenate([-sr, sr, -sc, sc], axis=0)
    cos_t = jnp.concatenate([jnp.ones((64, TILE), F32), cos_lat], axis=1)
    sin_t = jnp.concatenate([jnp.zeros((64, TILE), F32), sin_lat], axis=1)
    return cos_t, sin_t


def kernel(x, c, ctx, c_ctx, w_mod, b_mod, norm1_g, w_in, da_qn_g, da_kn_g, da_lambda, da_subln_g, gq_qn_g, gq_kn_g, dn_conv_w, dn_a_log, dn_dt_bias, dn_norm_g, w_br_a, w_br_b, w_br_c, w_o, norm2_g, ffn_w1, ffn_conv_w, ffn_conv_b, ffn_w2):
    batch, n_lat, d = x.shape
    depth = w_mod.shape[0]
    dff = ffn_w2.shape[1]
    assert ctx.shape[1] == TILE and n_lat % TILE == 0 and n_lat % GRID_W == 0 and batch + 1 <= 8
    n_t = 1 + n_lat // TILE
    assert n_t % 2 == 1, "the attention key loop handles key tiles in pairs plus one"
    s_tiles = batch * n_t
    m = s_tiles * TILE

    xs = jnp.concatenate([ctx, x], axis=1).reshape(m, d)

    cond = jnp.zeros((8, d), F32).at[:batch].set(c).at[batch].set(c_ctx)
    mod = _modulation(cond, w_mod, b_mod)
    tile_id = jnp.arange(s_tiles)
    row_of_tile = jnp.where(tile_id % n_t == 0, batch, tile_id // n_t)
    mod_tabs = mod[:, row_of_tile, :].reshape(depth, s_tiles, 6, 1, d)

    cos_t, sin_t = _rope_tables(n_lat)
    bcast = lambda g: jnp.broadcast_to(g[:, None], (g.shape[0], TILE))

    o_qkv = ATTN_ROWS
    o_b = o_qkv + 3 * DN_WIDTH
    o_z = o_b + 4 * DN_HEADS
    o_g = o_z + DN_WIDTH
    wt_attn = jnp.swapaxes(w_in[:, :, :ATTN_ROWS], 1, 2).astype(BF16)
    w_nat = jnp.concatenate(
        [w_in[:, :, o_g:], w_in[:, :, o_qkv:o_b], w_in[:, :, o_z:o_g], w_in[:, :, o_b:o_z],
         jnp.zeros((depth, d, NAT_COLS - NAT_BA_OFF - 4 * DN_HEADS), F32)], axis=2).astype(BF16)
    pad_tab = lambda a: jnp.zeros((1, 128), F32).at[0, 2 * DN_HEADS:4 * DN_HEADS].set(a.reshape(-1))

    for l in range(depth):
        lam_init = 0.8 - 0.6 * math.exp(-0.3 * l)
        tab = lambda i: mod_tabs[l, :, i]

        h1 = _norm_mod(xs, norm1_g[l], tab(0), tab(1))
        y_t = _matmul_nt(wt_attn[l], h1, ATTN_ROWS, F32)
        y_nat = _matmul_nn(h1, w_nat[l], NAT_COLS // 3, F32)

        gains = jnp.stack([bcast(da_qn_g[l]), bcast(da_kn_g[l]), bcast(gq_qn_g[l]), bcast(gq_kn_g[l])])
        qda, kda, vda, qgq, kgq, vgq = _attn_prep(y_t, cos_t, sin_t, gains, n_t)
        oa = _flash_da(qda, kda, vda, da_lambda[l], bcast(da_subln_g[l]), batch, n_t, lam_init)
        ob = _flash_gq(qgq, kgq, vgq, batch, n_t)

        dq, dk, dv, gb = _dn_prep(y_nat, dn_conv_w[l], pad_tab(dn_a_log[l]), pad_tab(dn_dt_bias[l]), n_t)
        ocf, ocb = _dn_scan(dq, dk, dv, gb, batch, n_t)

        xs = _merge(xs, oa, ob, ocf, ocb, y_nat, dn_norm_g[l].reshape(1, -1), tab(2),
                    w_br_a[l].astype(BF16), w_br_b[l].astype(BF16), w_br_c[l].astype(BF16), w_o[l].astype(BF16))

        h2 = _norm_mod(xs, norm2_g[l], tab(3), tab(4))
        au = _matmul_nn(h2, ffn_w1[l].astype(BF16), dff, BF16)
        xs = _ffn_out(xs, au, ffn_conv_w[l], ffn_conv_b[l], tab(5), ffn_w2[l].astype(BF16), n_t)

    return xs.reshape(batch, n_t * TILE, d)[:, TILE:, :]
```

```python
import functools
import math

import jax
import jax.numpy as jnp
from jax import lax
from jax.experimental import pallas as pl
from jax.experimental.pallas import tpu as pltpu

F32 = jnp.float32
BF16 = jnp.bfloat16

TILE = 256
CHUNK = 64
GRID_W = 64
ROPE_THETA = 10000.0
NORM_EPS = 1e-6
DA_HEADS, DA_HEAD_DIM = 4, 64
GQ_HEADS, GQ_KV_HEADS, GQ_HEAD_DIM = 8, 2, 64
DN_HEADS, DN_HEAD_DIM = 4, 128
LOG2E = 1.4426950408889634
NEG_BIG = -1e30
ONES_ROWS = 16
VMEM_LIMIT = 48 * 1024 * 1024

DA_COLS = DA_HEADS * 2 * DA_HEAD_DIM
GQ_Q_COLS = GQ_HEADS * GQ_HEAD_DIM
GQ_KV_COLS = GQ_KV_HEADS * GQ_HEAD_DIM
DN_WIDTH = DN_HEADS * DN_HEAD_DIM
ATTN_ROWS = 3 * DA_COLS + GQ_Q_COLS + 2 * GQ_KV_COLS
NAT_GATE_OFF, NAT_QKV_OFF, NAT_Z_OFF = 0, 3072, 4608
NAT_COLS = 5120
BA_COLS = 128


def _params(*sem):
    return pltpu.CompilerParams(dimension_semantics=sem, vmem_limit_bytes=VMEM_LIMIT)


def _sigmoid(x):
    return 1.0 / (1.0 + jnp.exp(-x))


def _silu(x):
    return x * _sigmoid(x)


def _dot(a, b):
    return jnp.dot(a, b, preferred_element_type=F32)


def _dot_nt(a, b):
    return lax.dot_general(a, b, (((1,), (1,)), ((), ())), preferred_element_type=F32)


def _dot_tn(a, b):
    return lax.dot_general(a, b, (((0,), (0,)), ((), ())), preferred_element_type=F32)


def _mod_kernel(c_ref, w_ref, b_ref, o_ref):
    cond = _silu(c_ref[...])
    o_ref[0] = jnp.dot(cond, w_ref[0], preferred_element_type=F32,
                       precision=lax.Precision.HIGHEST) + b_ref[0]


def _modulation(cond, w_mod, b_mod):
    depth, d, n = w_mod.shape
    tn = 1536
    return pl.pallas_call(
        _mod_kernel,
        grid=(depth, n // tn),
        in_specs=[pl.BlockSpec((8, d), lambda l, j: (0, 0)),
                  pl.BlockSpec((1, d, tn), lambda l, j: (l, 0, j)),
                  pl.BlockSpec((1, 1, tn), lambda l, j: (l, 0, j))],
        out_specs=pl.BlockSpec((1, 8, tn), lambda l, j: (l, 0, j)),
        out_shape=jax.ShapeDtypeStruct((depth, 8, n), F32),
        compiler_params=_params("parallel", "parallel"),
        name="modulation",
    )(cond, w_mod, b_mod.reshape(depth, 1, n))


def _norm_mod_kernel(x_ref, g_ref, shift_ref, scale_ref, o_ref):
    for i in range(x_ref.shape[0] // TILE):
        rows = slice(i * TILE, (i + 1) * TILE)
        x = x_ref[rows, :]
        ms = jnp.mean(x * x, axis=-1, keepdims=True)
        y = x * lax.rsqrt(ms + NORM_EPS) * g_ref[...]
        o_ref[rows, :] = (y * (1.0 + scale_ref[i]) + shift_ref[i]).astype(o_ref.dtype)


def _norm_mod(x, g, shift_tab, scale_tab):
    m, d = x.shape
    tm = _row_tile(m)
    sub = tm // TILE
    return pl.pallas_call(
        _norm_mod_kernel,
        grid=(m // tm,),
        in_specs=[pl.BlockSpec((tm, d), lambda s: (s, 0)),
                  pl.BlockSpec((1, d), lambda s: (0, 0)),
                  pl.BlockSpec((sub, 1, d), lambda s: (s, 0, 0)),
                  pl.BlockSpec((sub, 1, d), lambda s: (s, 0, 0))],
        out_specs=pl.BlockSpec((tm, d), lambda s: (s, 0)),
        out_shape=jax.ShapeDtypeStruct((m, d), BF16),
        compiler_params=_params("parallel"),
        name="norm_mod",
    )(x, g.reshape(1, d), shift_tab, scale_tab)


def _mm_nn_kernel(a_ref, b_ref, o_ref):
    o_ref[...] = _dot(a_ref[...], b_ref[...]).astype(o_ref.dtype)


def _mm_nt_kernel(wt_ref, a_ref, o_ref):
    o_ref[...] = _dot_nt(wt_ref[...], a_ref[...]).astype(o_ref.dtype)


def _row_tile(m):
    for t in (1024, 512, 256):
        if m % t == 0:
            return t
    raise ValueError(m)


def _matmul_nn(a, b, tn, out_dtype):
    m, k = a.shape
    n = b.shape[1]
    tm = _row_tile(m)
    return pl.pallas_call(
        _mm_nn_kernel,
        grid=(m // tm, n // tn),
        in_specs=[pl.BlockSpec((tm, k), lambda i, j: (i, 0)),
                  pl.BlockSpec((k, tn), lambda i, j: (0, j))],
        out_specs=pl.BlockSpec((tm, tn), lambda i, j: (i, j)),
        out_shape=jax.ShapeDtypeStruct((m, n), out_dtype),
        compiler_params=_params("parallel", "parallel"),
        name="matmul_nn",
    )(a, b)


def _matmul_nt(wt, a, tn, out_dtype):
    m, k = a.shape
    n = wt.shape[0]
    tm = _row_tile(m)
    return pl.pallas_call(
        _mm_nt_kernel,
        grid=(m // tm, n // tn),
        in_specs=[pl.BlockSpec((tn, k), lambda i, j: (j, 0)),
                  pl.BlockSpec((tm, k), lambda i, j: (i, 0))],
        out_specs=pl.BlockSpec((tn, tm), lambda i, j: (j, i)),
        out_shape=jax.ShapeDtypeStruct((n, m), out_dtype),
        compiler_params=_params("parallel", "parallel"),
        name="matmul_nt",
    )(wt, a)


def _norm_rope_t(x, gain, cos, sin, scale):
    x = x.astype(F32)
    ms = jnp.mean(x * x, axis=0, keepdims=True)
    xn = x * lax.rsqrt(ms + NORM_EPS) * gain
    sw = jnp.concatenate([xn[16:32], xn[0:16], xn[48:64], xn[32:48]], axis=0)
    out = xn * cos + sw * sin
    return out * scale if scale != 1.0 else out


def _attn_prep_kernel(y_ref, cos_ref, sin_ref, gain_ref,
                      qda_ref, kda_ref, vda_ref, qgq_ref, kgq_ref, vgq_ref):
    cos = cos_ref[...]
    sin = sin_ref[...]
    ones = jnp.ones((ONES_ROWS, TILE), BF16)
    qscale = (DA_HEAD_DIM ** -0.5) * LOG2E
    hd = DA_HEAD_DIM
    for h in range(DA_HEADS):
        halves_q, halves_k = [], []
        for half in range(2):
            r = h * 2 * hd + half * hd
            halves_q.append(_norm_rope_t(y_ref[r:r + hd, :], gain_ref[0], cos, sin, qscale))
            halves_k.append(_norm_rope_t(y_ref[DA_COLS + r:DA_COLS + r + hd, :], gain_ref[1], cos, sin, 1.0))
        qda_ref[0, h] = jnp.concatenate(halves_q, axis=0).astype(BF16)
        kda_ref[0, h] = jnp.concatenate(halves_k, axis=0).T.astype(BF16)
        r = 2 * DA_COLS + h * 2 * hd
        vda_ref[0, h] = jnp.concatenate([y_ref[r:r + 2 * hd, :], ones], axis=0)
    base = 3 * DA_COLS
    gscale = (GQ_HEAD_DIM ** -0.5) * LOG2E
    for h in range(GQ_HEADS):
        r = base + h * GQ_HEAD_DIM
        qgq_ref[0, h] = _norm_rope_t(y_ref[r:r + GQ_HEAD_DIM, :], gain_ref[2], cos, sin, gscale).astype(BF16)
    base += GQ_Q_COLS
    ks = [_norm_rope_t(y_ref[base + h * GQ_HEAD_DIM:base + (h + 1) * GQ_HEAD_DIM, :], gain_ref[3], cos, sin, 1.0)
          for h in range(GQ_KV_HEADS)]
    kgq_ref[0] = jnp.concatenate(ks, axis=0).T.astype(BF16)
    base += GQ_KV_COLS
    for h in range(GQ_KV_HEADS):
        r = base + h * GQ_HEAD_DIM
        vgq_ref[0, h] = jnp.concatenate([y_ref[r:r + GQ_HEAD_DIM, :], ones], axis=0)


def _attn_prep(y_t, cos_t, sin_t, gains, n_t):
    m = y_t.shape[1]
    s_tiles = m // TILE
    out_shape = (
        jax.ShapeDtypeStruct((s_tiles, DA_HEADS, 2 * DA_HEAD_DIM, TILE), BF16),
        jax.ShapeDtypeStruct((s_tiles, DA_HEADS, TILE, 2 * DA_HEAD_DIM), BF16),
        jax.ShapeDtypeStruct((s_tiles, DA_HEADS, 2 * DA_HEAD_DIM + ONES_ROWS, TILE), BF16),
        jax.ShapeDtypeStruct((s_tiles, GQ_HEADS, GQ_HEAD_DIM, TILE), BF16),
        jax.ShapeDtypeStruct((s_tiles, TILE, GQ_KV_COLS), BF16),
        jax.ShapeDtypeStruct((s_tiles, GQ_KV_HEADS, GQ_HEAD_DIM + ONES_ROWS, TILE), BF16),
    )
    blk4 = lambda sds: pl.BlockSpec((1,) + sds.shape[1:], lambda s: (s,) + (0,) * (len(sds.shape) - 1))
    return pl.pallas_call(
        _attn_prep_kernel,
        grid=(s_tiles,),
        in_specs=[pl.BlockSpec((ATTN_ROWS, TILE), lambda s: (0, s)),
                  pl.BlockSpec((GQ_HEAD_DIM, TILE), lambda s: (0, s % n_t)),
                  pl.BlockSpec((GQ_HEAD_DIM, TILE), lambda s: (0, s % n_t)),
                  pl.BlockSpec((4, GQ_HEAD_DIM, TILE), lambda s: (0, 0, 0))],
        out_specs=tuple(blk4(s) for s in out_shape),
        out_shape=out_shape,
        compiler_params=_params("parallel"),
        name="attn_prep",
    )(y_t, cos_t, sin_t, gains)


def _softmax_update(s_ref, idx, m_tile, m_old):
    m_new = jnp.maximum(m_old, m_tile)
    alpha = jnp.exp2(m_old - m_new)
    p = jnp.exp2(s_ref[idx] - m_new)
    return p.astype(BF16), alpha, m_new


def _attention_program(n_t, n_mat, scores, consume, finalize):
    neg = (jnp.full((1, TILE), NEG_BIG, F32),) * n_mat
    consume(0, 0, scores(0, 0, 0), neg)
    finalize(0)
    half = (n_t - 1) // 2
    mt0 = scores(1, 0, 0)

    def query_pair(qp, mt):
        for par in (0, 1):
            qi = 1 + 2 * qp + par
            a, b = (0, 1) if par == 0 else (1, 0)

            def key_pair(i, carry, qi=qi, a=a, b=b):
                mt_a, ms = carry[:n_mat], carry[n_mat:]
                mt_b = scores(qi, 2 * i + 1, b)
                ms = consume(2 * i, a, mt_a, ms)
                mt_a = scores(qi, 2 * i + 2, a)
                ms = consume(2 * i + 1, b, mt_b, ms)
                return tuple(mt_a) + tuple(ms)

            carry = lax.fori_loop(0, half, key_pair, tuple(mt) + neg)
            mt = scores(jnp.minimum(qi + 1, n_t - 1), 0, b)
            consume(n_t - 1, a, carry[:n_mat], carry[n_mat:])
            finalize(qi)
        return tuple(mt)

    lax.fori_loop(0, half, query_pair, tuple(mt0))


DA_HEADS_PER_STEP = 2


def _flash_da_kernel(q_ref, k_ref, v_ref, lam_ref, g_ref, o_ref, qpad_ref, acc_ref, sa_ref, sb_ref,
                     *, n_t, lam_init):
    dv = 2 * DA_HEAD_DIM
    hps = DA_HEADS_PER_STEP
    bufs = (sa_ref, sb_ref)
    acc_ref[...] = jnp.zeros_like(acc_ref)
    top = lax.broadcasted_iota(jnp.int32, (dv, TILE), 0) < DA_HEAD_DIM

    def pad_queries(qi, carry):
        for hh in range(hps):
            q = q_ref[qi, hh]
            zero = jnp.zeros_like(q)
            qpad_ref[qi, 2 * hh] = jnp.where(top, q, zero)
            qpad_ref[qi, 2 * hh + 1] = jnp.where(top, zero, q)
        return carry

    lax.fori_loop(0, n_t, pad_queries, 0)
    lp = lam_ref[...]
    lam = (jnp.exp(jnp.sum(lp[0:1] * lp[1:2], keepdims=True))
           - jnp.exp(jnp.sum(lp[2:3] * lp[3:4], keepdims=True)) + lam_init)

    def scores(qi, j, buf):
        out = []
        for hh in range(hps):
            k = k_ref[j, hh]
            for i in range(2):
                s = _dot(k, qpad_ref[qi, 2 * hh + i])
                bufs[buf][2 * hh + i] = s
                out.append(jnp.max(s, axis=0, keepdims=True))
        return out

    def consume(j, buf, mts, ms):
        out = []
        for hh in range(hps):
            vt = v_ref[j, hh]
            for i in range(2):
                c = 2 * hh + i
                p, alpha, m_new = _softmax_update(bufs[buf], c, mts[c], ms[c])
                acc_ref[c] = acc_ref[c] * alpha + _dot(vt, p)
                out.append(m_new)
        return tuple(out)

    def finalize(qi):
        rows = pl.ds(pl.multiple_of(qi * TILE, TILE), TILE)
        for hh in range(hps):
            a1, a2 = acc_ref[2 * hh], acc_ref[2 * hh + 1]
            o = a1[0:dv] / a1[dv:dv + 1] - lam * (a2[0:dv] / a2[dv:dv + 1])
            ms = jnp.mean(o * o, axis=0, keepdims=True)
            o = o * lax.rsqrt(ms + NORM_EPS) * g_ref[...] * (1.0 - lam_init)
            o_ref[rows, hh * dv:(hh + 1) * dv] = o.T.astype(o_ref.dtype)
        acc_ref[...] = jnp.zeros_like(acc_ref)

    _attention_program(n_t, 2 * hps, scores, consume, finalize)


def _flash_da(qda, kda, vda, lam_p, subln_g, batch, n_t, lam_init):
    dv = 2 * DA_HEAD_DIM
    hps = DA_HEADS_PER_STEP
    m = batch * n_t * TILE
    kern = functools.partial(_flash_da_kernel, n_t=n_t, lam_init=lam_init)
    return pl.pallas_call(
        kern,
        grid=(batch, DA_HEADS // hps),
        in_specs=[pl.BlockSpec((n_t, hps, dv, TILE), lambda b, h: (b, h, 0, 0)),
                  pl.BlockSpec((n_t, hps, TILE, dv), lambda b, h: (b, h, 0, 0)),
                  pl.BlockSpec((n_t, hps, dv + ONES_ROWS, TILE), lambda b, h: (b, h, 0, 0)),
                  pl.BlockSpec((4, DA_HEAD_DIM), lambda b, h: (0, 0)),
                  pl.BlockSpec((dv, TILE), lambda b, h: (0, 0))],
        out_specs=pl.BlockSpec((n_t * TILE, hps * dv), lambda b, h: (b, h)),
        out_shape=jax.ShapeDtypeStruct((m, DA_HEADS * dv), BF16),
        scratch_shapes=[pltpu.VMEM((n_t, 2 * hps, dv, TILE), BF16),
                        pltpu.VMEM((2 * hps, dv + ONES_ROWS, TILE), F32),
                        pltpu.VMEM((2 * hps, TILE, TILE), F32), pltpu.VMEM((2 * hps, TILE, TILE), F32)],
        compiler_params=_params("parallel", "parallel"),
        name="flash_da",
    )(qda, kda, vda, lam_p, subln_g)


def _flash_gq_kernel(q_ref, k_ref, v_ref, o_ref, qpad_ref, acc_ref, sa_ref, sb_ref, *, n_t):
    first = pl.program_id(1) == 0
    group = GQ_HEADS // GQ_KV_HEADS
    dv = GQ_HEAD_DIM
    bufs = (sa_ref, sb_ref)
    acc_ref[...] = jnp.zeros_like(acc_ref)

    def pad_queries(qi, carry):
        for g in range(group):
            q = q_ref[qi, g]
            zero = jnp.zeros_like(q)
            qpad_ref[qi, g] = jnp.concatenate([jnp.where(first, q, zero), jnp.where(first, zero, q)], axis=0)
        return carry

    lax.fori_loop(0, n_t, pad_queries, 0)

    def scores(qi, j, buf):
        k = k_ref[j]
        out = []
        for g in range(group):
            s = _dot(k, qpad_ref[qi, g])
            bufs[buf][g] = s
            out.append(jnp.max(s, axis=0, keepdims=True))
        return out

    def consume(j, buf, mts, ms):
        vt = v_ref[j, 0]
        out = []
        for g in range(group):
            p, alpha, m_new = _softmax_update(bufs[buf], g, mts[g], ms[g])
            acc_ref[g] = acc_ref[g] * alpha + _dot(vt, p)
            out.append(m_new)
        return tuple(out)

    def finalize(qi):
        rows = pl.ds(pl.multiple_of(qi * TILE, TILE), TILE)
        o = jnp.concatenate([acc_ref[g, 0:dv, :] / acc_ref[g, dv:dv + 1, :] for g in range(group)], axis=0)
        o_ref[rows, :] = o.T.astype(o_ref.dtype)
        acc_ref[...] = jnp.zeros_like(acc_ref)

    _attention_program(n_t, group, scores, consume, finalize)


def _flash_gq(qgq, kgq, vgq, batch, n_t):
    group = GQ_HEADS // GQ_KV_HEADS
    m = batch * n_t * TILE
    kern = functools.partial(_flash_gq_kernel, n_t=n_t)
    return pl.pallas_call(
        kern,
        grid=(batch, GQ_KV_HEADS),
        in_specs=[pl.BlockSpec((n_t, group, GQ_HEAD_DIM, TILE), lambda b, h: (b, h, 0, 0)),
                  pl.BlockSpec((n_t, TILE, GQ_KV_COLS), lambda b, h: (b, 0, 0)),
                  pl.BlockSpec((n_t, 1, GQ_HEAD_DIM + ONES_ROWS, TILE), lambda b, h: (b, h, 0, 0))],
        out_specs=pl.BlockSpec((n_t * TILE, group * GQ_HEAD_DIM), lambda b, h: (b, h)),
        out_shape=jax.ShapeDtypeStruct((m, GQ_Q_COLS), BF16),
        scratch_shapes=[pltpu.VMEM((n_t, group, 2 * GQ_HEAD_DIM, TILE), BF16),
                        pltpu.VMEM((group, GQ_HEAD_DIM + ONES_ROWS, TILE), F32),
                        pltpu.VMEM((group, TILE, TILE), F32), pltpu.VMEM((group, TILE, TILE), F32)],
        compiler_params=_params("parallel", "parallel"),
        name="flash_gq",
    )(qgq, kgq, vgq)


def _conv3(x, prev_row, next_row, w_ref):
    n = x.shape[0]
    rid = lax.broadcasted_iota(jnp.int32, x.shape, 0)
    x_m1 = jnp.where(rid == 0, prev_row, pltpu.roll(x, 1, 0))
    x_p1 = jnp.where(rid == n - 1, next_row, pltpu.roll(x, n - 1, 0))
    return x_m1 * w_ref[0:1, :] + x * w_ref[1:2, :] + x_p1 * w_ref[2:3, :]


def _halo_rows(prev_ref, next_ref, j, n_t):
    has_prev = (j >= 2).astype(F32)
    has_next = jnp.logical_and(j >= 1, j <= n_t - 2).astype(F32)
    prev_row = prev_ref[7:8, :].astype(F32) * has_prev
    next_row = next_ref[0:1, :].astype(F32) * has_next
    return prev_row, next_row


def _halo_specs(cols, col_blk, n_rows8):
    prev = pl.BlockSpec((8, cols), lambda s: (jnp.maximum(s * (TILE // 8) - 1, 0), col_blk))
    nxt = pl.BlockSpec((8, cols), lambda s: (jnp.minimum((s + 1) * (TILE // 8), n_rows8 - 1), col_blk))
    return prev, nxt


def _dn_prep_kernel(x_ref, prev_ref, next_ref, ba_ref, cw_ref, alog_ref, dtb_ref,
                    q_ref, k_ref, v_ref, gb_ref, *, n_t):
    j = pl.program_id(0) % n_t
    prev_row, next_row = _halo_rows(prev_ref, next_ref, j, n_t)
    y = _silu(_conv3(x_ref[...].astype(F32), prev_row, next_row, cw_ref))
    for h in range(DN_HEADS):
        c = h * DN_HEAD_DIM
        q = y[:, c:c + DN_HEAD_DIM]
        k = y[:, DN_WIDTH + c:DN_WIDTH + c + DN_HEAD_DIM]
        qn = q * (lax.rsqrt(jnp.sum(q * q, axis=-1, keepdims=True) + NORM_EPS) * (DN_HEAD_DIM ** -0.5))
        q_ref[:, c:c + DN_HEAD_DIM] = qn.astype(q_ref.dtype)
        kn = k * lax.rsqrt(jnp.sum(k * k, axis=-1, keepdims=True) + NORM_EPS)
        k_ref[:, c:c + DN_HEAD_DIM] = kn.astype(k_ref.dtype)
    v_ref[...] = y[:, 2 * DN_WIDTH:].astype(v_ref.dtype)
    ba = ba_ref[...]
    beta = _sigmoid(ba)
    z = ba + dtb_ref[...]
    softplus = jnp.maximum(z, 0.0) + jnp.log(1.0 + jnp.exp(-jnp.abs(z)))
    g = -jnp.exp(alog_ref[...]) * softplus
    col = lax.broadcasted_iota(jnp.int32, ba.shape, 1)
    gb_ref[...] = jnp.where(col < 2 * DN_HEADS, beta, g)


def _dn_prep(y_nat, y_ba, conv_w, alog_tab, dtb_tab, n_t):
    m = y_nat.shape[0]
    w3 = 3 * DN_WIDTH
    prev, nxt = _halo_specs(w3, NAT_QKV_OFF // w3, m // 8)
    tok = lambda c: pl.BlockSpec((TILE, c), lambda s: (s, 0))
    return pl.pallas_call(
        functools.partial(_dn_prep_kernel, n_t=n_t),
        grid=(m // TILE,),
        in_specs=[pl.BlockSpec((TILE, w3), lambda s: (s, NAT_QKV_OFF // w3)), prev, nxt,
                  tok(BA_COLS),
                  pl.BlockSpec((3, w3), lambda s: (0, 0)),
                  pl.BlockSpec((1, BA_COLS), lambda s: (0, 0)),
                  pl.BlockSpec((1, BA_COLS), lambda s: (0, 0))],
        out_specs=(tok(DN_WIDTH), tok(DN_WIDTH), tok(DN_WIDTH), tok(BA_COLS)),
        out_shape=(jax.ShapeDtypeStruct((m, DN_WIDTH), BF16),) * 3 + (jax.ShapeDtypeStruct((m, BA_COLS), F32),),
        compiler_params=_params("parallel"),
        name="dn_prep",
    )(y_nat, y_nat, y_nat, y_ba, conv_w, alog_tab, dtb_tab)


def _dn_scan_kernel(qf_ref, kf_ref, vf_ref, gf_ref, qb_ref, kb_ref, vb_ref, gb_ref, of_ref, ob_ref,
                    s_ref, a16_ref, tinv_ref, t16_ref, w16_ref, qk16_ref, rhs16_ref, u_ref, uw16_ref,
                    kt16_ref, qd16_ref, kwb_ref, sc16_ref, vnew16_ref, inter_ref):
    t = pl.program_id(1)

    @pl.when(t == 0)
    def _():
        s_ref[...] = jnp.zeros_like(s_ref)

    n_chunk = TILE // CHUNK
    shift = CHUNK.bit_length() - 1
    dh = DN_HEAD_DIM
    ii = lax.broadcasted_iota(jnp.int32, (TILE, TILE), 0)
    jj = lax.broadcasted_iota(jnp.int32, (TILE, TILE), 1)
    same = (ii >> shift) == (jj >> shift)
    eye = (ii == jj).astype(F32)
    pair0 = (ii >> 1) == (jj >> 1)
    ins = ((qf_ref, kf_ref, vf_ref, gf_ref, of_ref), (qb_ref, kb_ref, vb_ref, gb_ref, ob_ref))
    chains = [(d, h) for d in range(2) for h in range(DN_HEADS)]
    incls = (jnp.logical_and(same, ii >= jj), jnp.logical_and(same, ii <= jj))
    stricts = (jnp.logical_and(same, ii > jj), jnp.logical_and(same, ii < jj))
    as_bf16 = lambda mask: jnp.where(mask, 1.0, 0.0).astype(BF16)
    same16 = as_bf16(same)
    n_g = 4 * DN_HEADS
    zpad = jnp.zeros((BA_COLS - n_g, TILE), F32)

    gls = []
    for d in range(2):
        incl, strict = incls[d], stricts[d]
        gb = ins[d][3][...]
        g_t = gb.T[0:n_g]
        p0 = g_t.astype(BF16)
        r1 = g_t - p0.astype(F32)
        p1 = r1.astype(BF16)
        p2 = (r1 - p1.astype(F32)).astype(BF16)
        terms = jnp.concatenate([p0, p1, p2], axis=0)

        def masked_sum(mask16):
            r = _dot(terms, mask16)
            return r[0:n_g] + r[n_g:2 * n_g] + r[2 * n_g:3 * n_g]

        gcum_t = masked_sum(as_bf16(incls[1 - d]))
        gtot_t = masked_sum(same16)
        gcum = jnp.concatenate([gcum_t, zpad], axis=0).T
        gtot = jnp.concatenate([gtot_t, zpad], axis=0).T
        for h in range(DN_HEADS):
            ch = d * DN_HEADS + h
            cg = 2 * DN_HEADS + ch
            lanes = slice(h * dh, (h + 1) * dh)
            q16 = ins[d][0][:, lanes]
            k16 = ins[d][1][:, lanes]
            q = q16.astype(F32)
            k = k16.astype(F32)
            v = ins[d][2][:, lanes].astype(F32)
            beta = gb[:, ch:ch + 1]
            gc = gcum[:, cg:cg + 1]
            gr = gcum_t[cg:cg + 1, :]
            gl = gtot[:, cg:cg + 1]
            gls.append(gl)
            decay = jnp.where(incl, jnp.exp(jnp.where(incl, gc - gr, 0.0)), 0.0)
            kb = k * beta
            a = jnp.where(strict, _dot_nt(kb.astype(BF16), k16) * decay, 0.0)
            a16_ref[ch] = a.astype(BF16)
            t0 = eye - jnp.where(pair0, a, 0.0)
            tinv_ref[ch] = t0
            t16_ref[ch] = t0.astype(BF16)
            qk16_ref[ch] = jnp.where(incl, _dot_nt(q16, k16) * decay, 0.0).astype(BF16)
            egc = jnp.exp(gc)
            rhs16_ref[ch] = jnp.concatenate([v * beta, kb * egc], axis=1).astype(BF16)
            kt16_ref[ch] = (k * jnp.exp(gl - gc)).astype(BF16)
            qd16_ref[ch] = (q * egc).astype(BF16)

    lvl = 1
    while (2 << lvl) <= CHUNK:
        sib = jnp.logical_and((ii >> (lvl + 1)) == (jj >> (lvl + 1)), (ii >> lvl) != (jj >> lvl))
        for ch in range(len(chains)):
            w16_ref[ch] = _dot(a16_ref[ch], t16_ref[ch]).astype(BF16)
        for ch in range(len(chains)):
            tn = jnp.where(sib, -_dot(t16_ref[ch], w16_ref[ch]), tinv_ref[ch])
            tinv_ref[ch] = tn
            t16_ref[ch] = tn.astype(BF16)
        lvl += 1

    for ch in range(len(chains)):
        uw = _dot(t16_ref[ch], rhs16_ref[ch])
        u_ref[ch] = uw[:, 0:dh]
        uw16_ref[ch] = uw.astype(BF16)

    for ch in range(len(chains)):
        for c in range(n_chunk):
            rows = slice(c * CHUNK, (c + 1) * CHUNK)
            kwb_ref[ch, c] = _dot_tn(kt16_ref[ch, rows, :], uw16_ref[ch, rows, :])

    states = [s_ref[ch] for ch in range(len(chains))]
    for step in range(n_chunk):
        for ch, (d, h) in enumerate(chains):
            c = n_chunk - 1 - step if d else step
            s16 = states[ch].astype(BF16)
            sc16_ref[ch, c] = s16
            egl = jnp.exp(gls[ch][c * CHUNK:c * CHUNK + 1, :])
            states[ch] = (states[ch] * egl - _dot(kwb_ref[ch, c, :, dh:].astype(BF16), s16)
                          + kwb_ref[ch, c, :, 0:dh])
    for ch in range(len(chains)):
        s_ref[ch] = states[ch]

    for ch in range(len(chains)):
        for c in range(n_chunk):
            rows = slice(c * CHUNK, (c + 1) * CHUNK)
            s16 = sc16_ref[ch, c]
            vnew16_ref[ch, rows, :] = (u_ref[ch, rows, :] - _dot(uw16_ref[ch, rows, dh:], s16)).astype(BF16)
            inter_ref[ch, rows, :] = _dot(qd16_ref[ch, rows, :], s16)
    for ch, (d, h) in enumerate(chains):
        ins[d][4][:, h * dh:(h + 1) * dh] = inter_ref[ch] + _dot(qk16_ref[ch], vnew16_ref[ch])


def _dn_scan(q, k, v, gb, batch, n_t):
    m = q.shape[0]
    fwd = lambda b, t: (b * n_t + t, 0)
    bwd = lambda b, t: (b * n_t + jnp.where(t == 0, 0, n_t - t), 0)
    specs = lambda tile: [pl.BlockSpec((TILE, DN_WIDTH), tile)] * 3 + [pl.BlockSpec((TILE, 128), tile)]
    nc = 2 * DN_HEADS
    dh = DN_HEAD_DIM
    big = lambda dt: pltpu.VMEM((nc, TILE, TILE), dt)
    half = lambda dt: pltpu.VMEM((nc, TILE, dh), dt)
    return pl.pallas_call(
        _dn_scan_kernel,
        grid=(batch, n_t),
        in_specs=specs(fwd) + specs(bwd),
        out_specs=(pl.BlockSpec((TILE, DN_WIDTH), fwd), pl.BlockSpec((TILE, DN_WIDTH), bwd)),
        out_shape=(jax.ShapeDtypeStruct((m, DN_WIDTH), F32),) * 2,
        scratch_shapes=[pltpu.VMEM((nc, dh, dh), F32),
                        big(BF16), big(F32), big(BF16), big(BF16), big(BF16), big(BF16),
                        half(F32), big(BF16), half(BF16), half(BF16),
                        pltpu.VMEM((nc, TILE // CHUNK, dh, 2 * dh), F32),
                        pltpu.VMEM((nc, TILE // CHUNK, dh, dh), BF16),
                        half(BF16), half(F32)],
        compiler_params=_params("parallel", "arbitrary"),
        name="dn_scan",
    )(q, k, v, gb, q, k, v, gb)


def _merge_kernel(x_ref, oa_ref, ob_ref, ocf_ref, ocb_ref, z_ref, gate_ref, dng_ref, mod_ref,
                  wa_ref, wb_ref, wc_ref, wo_ref, o_ref):
    d = x_ref.shape[1]
    oc = ocf_ref[...] + ocb_ref[...]
    z = z_ref[...].astype(F32)
    parts = []
    for h in range(DN_HEADS):
        lanes = slice(h * DN_HEAD_DIM, (h + 1) * DN_HEAD_DIM)
        o = oc[:, lanes]
        ms = jnp.mean(o * o, axis=-1, keepdims=True)
        parts.append(o * lax.rsqrt(ms + NORM_EPS) * dng_ref[...] * _silu(z[:, lanes]))
    oc_n = jnp.concatenate(parts, axis=1).astype(BF16)
    gates = _sigmoid(gate_ref[...].astype(F32))
    mix = (gates[:, 0:d] * _dot(oa_ref[...], wa_ref[...])
           + gates[:, d:2 * d] * _dot(ob_ref[...], wb_ref[...])
           + gates[:, 2 * d:3 * d] * _dot(oc_n, wc_ref[...]))
    o_ref[...] = x_ref[...] + mod_ref[0] * _dot(mix.astype(BF16), wo_ref[...])


def _merge(x, oa, ob, ocf, ocb, y_nat, dn_g, mod_tab, wa, wb, wc, wo):
    m, d = x.shape
    tok = lambda c, cb=0: pl.BlockSpec((TILE, c), lambda s: (s, cb))
    full = lambda a: pl.BlockSpec(a.shape, lambda s: (0,) * a.ndim)
    return pl.pallas_call(
        _merge_kernel,
        grid=(m // TILE,),
        in_specs=[tok(d), tok(DA_COLS), tok(GQ_Q_COLS), tok(DN_WIDTH), tok(DN_WIDTH),
                  tok(DN_WIDTH, NAT_Z_OFF // DN_WIDTH), tok(3 * d, NAT_GATE_OFF),
                  full(dn_g), pl.BlockSpec((1, 1, d), lambda s: (s, 0, 0)),
                  full(wa), full(wb), full(wc), full(wo)],
        out_specs=tok(d),
        out_shape=jax.ShapeDtypeStruct((m, d), F32),
        compiler_params=_params("parallel"),
        name="merge",
    )(x, oa, ob, ocf, ocb, y_nat, y_nat, dn_g, mod_tab, wa, wb, wc, wo)


def _ffn_out_kernel(x_ref, a_ref, prev_ref, next_ref, u_ref, cw_ref, cb_ref, mod_ref, w2_ref, o_ref, *, n_t):
    j = pl.program_id(0) % n_t
    prev_row, next_row = _halo_rows(prev_ref, next_ref, j, n_t)
    a = _conv3(a_ref[...].astype(F32), prev_row, next_row, cw_ref) + cb_ref[...]
    act = (_silu(a) * u_ref[...].astype(F32)).astype(BF16)
    o_ref[...] = x_ref[...] + mod_ref[0] * _dot(act, w2_ref[...])


def _ffn_out(x, au, conv_w, conv_b, mod_tab, w2, n_t):
    m, d = x.shape
    dff = w2.shape[0]
    prev, nxt = _halo_specs(dff, 0, m // 8)
    return pl.pallas_call(
        functools.partial(_ffn_out_kernel, n_t=n_t),
        grid=(m // TILE,),
        in_specs=[pl.BlockSpec((TILE, d), lambda s: (s, 0)),
                  pl.BlockSpec((TILE, dff), lambda s: (s, 0)), prev, nxt,
                  pl.BlockSpec((TILE, dff), lambda s: (s, 1)),
                  pl.BlockSpec((3, dff), lambda s: (0, 0)),
                  pl.BlockSpec((1, dff), lambda s: (0, 0)),
                  pl.BlockSpec((1, 1, d), lambda s: (s, 0, 0)),
                  pl.BlockSpec((dff, d), lambda s: (0, 0))],
        out_specs=pl.BlockSpec((TILE, d), lambda s: (s, 0)),
        out_shape=jax.ShapeDtypeStruct((m, d), F32),
        compiler_params=_params("parallel"),
        name="ffn_out",
    )(x, au, au, au, au, conv_w, conv_b.reshape(1, dff), mod_tab, w2)


def _rope_tables(n_lat):
    t = jnp.arange(n_lat, dtype=jnp.int32)
    row = (t // GRID_W).astype(F32)
    col = (t % GRID_W).astype(F32)
    d_axis = 32
    inv_freq = ROPE_THETA ** (-jnp.arange(0, d_axis, 2, dtype=F32) / d_axis)
    ang_r = row[None, :] * inv_freq[:, None]
    ang_c = col[None, :] * inv_freq[:, None]
    cr, sr, cc, sc = jnp.cos(ang_r), jnp.sin(ang_r), jnp.cos(ang_c), jnp.sin(ang_c)
    cos_lat = jnp.concatenate([cr, cr, cc, cc], axis=0)
    sin_lat = jnp.concatenate([-sr, sr, -sc, sc], axis=0)
    cos_t = jnp.concatenate([jnp.ones((64, TILE), F32), cos_lat], axis=1)
    sin_t = jnp.concatenate([jnp.zeros((64, TILE), F32), sin_lat], axis=1)
    return cos_t, sin_t


def kernel(x, c, ctx, c_ctx, w_mod, b_mod, norm1_g, w_in, da_qn_g, da_kn_g, da_lambda, da_subln_g, gq_qn_g, gq_kn_g, dn_conv_w, dn_a_log, dn_dt_bias, dn_norm_g, w_br_a, w_br_b, w_br_c, w_o, norm2_g, ffn_w1, ffn_conv_w, ffn_conv_b, ffn_w2):
    batch, n_lat, d = x.shape
    depth = w_mod.shape[0]
    dff = ffn_w2.shape[1]
    assert ctx.shape[1] == TILE and n_lat % TILE == 0 and n_lat % GRID_W == 0 and batch + 1 <= 8
    n_t = 1 + n_lat // TILE
    assert n_t % 2 == 1, "the attention key loop handles key tiles in pairs plus one"
    s_tiles = batch * n_t
    m = s_tiles * TILE

    xs = jnp.concatenate([ctx, x], axis=1).reshape(m, d)

    cond = jnp.zeros((8, d), F32).at[:batch].set(c).at[batch].set(c_ctx)
    mod = _modulation(cond, w_mod, b_mod)
    tile_id = jnp.arange(s_tiles)
    row_of_tile = jnp.where(tile_id % n_t == 0, batch, tile_id // n_t)
    mod_tabs = mod[:, row_of_tile, :].reshape(depth, s_tiles, 6, 1, d)

    cos_t, sin_t = _rope_tables(n_lat)
    bcast = lambda g: jnp.broadcast_to(g[:, None], (g.shape[0], TILE))

    o_qkv = ATTN_ROWS
    o_b = o_qkv + 3 * DN_WIDTH
    o_z = o_b + 4 * DN_HEADS
    o_g = o_z + DN_WIDTH
    wt_attn = jnp.swapaxes(w_in[:, :, :ATTN_ROWS], 1, 2).astype(BF16)
    w_nat = jnp.concatenate([w_in[:, :, o_g:], w_in[:, :, o_qkv:o_b], w_in[:, :, o_z:o_g]], axis=2).astype(BF16)
    w_ba = jnp.concatenate([w_in[:, :, o_b:o_z], jnp.zeros((depth, d, BA_COLS - 4 * DN_HEADS), F32)],
                           axis=2).astype(BF16)
    pad_tab = lambda a: jnp.zeros((1, BA_COLS), F32).at[0, 2 * DN_HEADS:4 * DN_HEADS].set(a.reshape(-1))

    for l in range(depth):
        lam_init = 0.8 - 0.6 * math.exp(-0.3 * l)
        tab = lambda i: mod_tabs[l, :, i]

        h1 = _norm_mod(xs, norm1_g[l], tab(0), tab(1))
        y_t = _matmul_nt(wt_attn[l], h1, ATTN_ROWS, BF16)
        y_nat = _matmul_nn(h1, w_nat[l], NAT_COLS // 2, BF16)
        y_ba = _matmul_nn(h1, w_ba[l], BA_COLS, F32)

        gains = jnp.stack([bcast(da_qn_g[l]), bcast(da_kn_g[l]), bcast(gq_qn_g[l]), bcast(gq_kn_g[l])])
        qda, kda, vda, qgq, kgq, vgq = _attn_prep(y_t, cos_t, sin_t, gains, n_t)
        oa = _flash_da(qda, kda, vda, da_lambda[l], bcast(da_subln_g[l]), batch, n_t, lam_init)
        ob = _flash_gq(qgq, kgq, vgq, batch, n_t)

        dq, dk, dv, gb = _dn_prep(y_nat, y_ba, dn_conv_w[l], pad_tab(dn_a_log[l]), pad_tab(dn_dt_bias[l]), n_t)
        ocf, ocb = _dn_scan(dq, dk, dv, gb, batch, n_t)

        xs = _merge(xs, oa, ob, ocf, ocb, y_nat, dn_norm_g[l].reshape(1, -1), tab(2),
                    w_br_a[l].astype(BF16), w_br_b[l].astype(BF16), w_br_c[l].astype(BF16), w_o[l].astype(BF16))

        h2 = _norm_mod(xs, norm2_g[l], tab(3), tab(4))
        au = _matmul_nn(h2, ffn_w1[l].astype(BF16), dff, BF16)
        xs = _ffn_out(xs, au, ffn_conv_w[l], ffn_conv_b[l], tab(5), ffn_w2[l].astype(BF16), n_t)

    return xs.reshape(batch, n_t * TILE, d)[:, TILE:, :]
```

```python
import functools
import math

import jax
import jax.numpy as jnp
from jax import lax
from jax.experimental import pallas as pl
from jax.experimental.pallas import tpu as pltpu

F32 = jnp.float32
BF16 = jnp.bfloat16

TILE = 256
CHUNK = 64
GRID_W = 64
ROPE_THETA = 10000.0
NORM_EPS = 1e-6
DA_HEADS, DA_HEAD_DIM = 4, 64
GQ_HEADS, GQ_KV_HEADS, GQ_HEAD_DIM = 8, 2, 64
DN_HEADS, DN_HEAD_DIM = 4, 128
LOG2E = 1.4426950408889634
NEG_BIG = -1e30
ONES_ROWS = 16
VMEM_LIMIT = 48 * 1024 * 1024

DA_COLS = DA_HEADS * 2 * DA_HEAD_DIM
GQ_Q_COLS = GQ_HEADS * GQ_HEAD_DIM
GQ_KV_COLS = GQ_KV_HEADS * GQ_HEAD_DIM
DN_WIDTH = DN_HEADS * DN_HEAD_DIM
ATTN_ROWS = 3 * DA_COLS + GQ_Q_COLS + 2 * GQ_KV_COLS
NAT_GATE_OFF, NAT_QKV_OFF, NAT_Z_OFF = 0, 3072, 4608
NAT_COLS = 5120
BA_COLS = 128


def _params(*sem):
    return pltpu.CompilerParams(dimension_semantics=sem, vmem_limit_bytes=VMEM_LIMIT)


def _sigmoid(x):
    return 1.0 / (1.0 + jnp.exp(-x))


def _silu(x):
    return x * _sigmoid(x)


def _dot(a, b):
    return jnp.dot(a, b, preferred_element_type=F32)


def _dot_nt(a, b):
    return lax.dot_general(a, b, (((1,), (1,)), ((), ())), preferred_element_type=F32)


def _dot_tn(a, b):
    return lax.dot_general(a, b, (((0,), (0,)), ((), ())), preferred_element_type=F32)


def _mod_kernel(c_ref, w_ref, b_ref, o_ref):
    cond = _silu(c_ref[...])
    o_ref[0] = jnp.dot(cond, w_ref[0], preferred_element_type=F32,
                       precision=lax.Precision.HIGHEST) + b_ref[0]


def _modulation(cond, w_mod, b_mod):
    depth, d, n = w_mod.shape
    tn = 1536
    return pl.pallas_call(
        _mod_kernel,
        grid=(depth, n // tn),
        in_specs=[pl.BlockSpec((8, d), lambda l, j: (0, 0)),
                  pl.BlockSpec((1, d, tn), lambda l, j: (l, 0, j)),
                  pl.BlockSpec((1, 1, tn), lambda l, j: (l, 0, j))],
        out_specs=pl.BlockSpec((1, 8, tn), lambda l, j: (l, 0, j)),
        out_shape=jax.ShapeDtypeStruct((depth, 8, n), F32),
        compiler_params=_params("parallel", "parallel"),
        name="modulation",
    )(cond, w_mod, b_mod.reshape(depth, 1, n))


def _norm_mod_kernel(x_ref, g_ref, shift_ref, scale_ref, o_ref):
    for i in range(x_ref.shape[0] // TILE):
        rows = slice(i * TILE, (i + 1) * TILE)
        x = x_ref[rows, :]
        ms = jnp.mean(x * x, axis=-1, keepdims=True)
        y = x * lax.rsqrt(ms + NORM_EPS) * g_ref[...]
        o_ref[rows, :] = (y * (1.0 + scale_ref[i]) + shift_ref[i]).astype(o_ref.dtype)


def _norm_mod(x, g, shift_tab, scale_tab):
    m, d = x.shape
    tm = _row_tile(m)
    sub = tm // TILE
    return pl.pallas_call(
        _norm_mod_kernel,
        grid=(m // tm,),
        in_specs=[pl.BlockSpec((tm, d), lambda s: (s, 0)),
                  pl.BlockSpec((1, d), lambda s: (0, 0)),
                  pl.BlockSpec((sub, 1, d), lambda s: (s, 0, 0)),
                  pl.BlockSpec((sub, 1, d), lambda s: (s, 0, 0))],
        out_specs=pl.BlockSpec((tm, d), lambda s: (s, 0)),
        out_shape=jax.ShapeDtypeStruct((m, d), BF16),
        compiler_params=_params("parallel"),
        name="norm_mod",
    )(x, g.reshape(1, d), shift_tab, scale_tab)


def _mm_nn_kernel(a_ref, b_ref, o_ref):
    o_ref[...] = _dot(a_ref[...], b_ref[...]).astype(o_ref.dtype)


def _row_tile(m):
    for t in (1024, 512, 256):
        if m % t == 0:
            return t
    raise ValueError(m)


def _matmul_nn(a, b, tn, out_dtype):
    m, k = a.shape
    n = b.shape[1]
    tm = _row_tile(m)
    return pl.pallas_call(
        _mm_nn_kernel,
        grid=(m // tm, n // tn),
        in_specs=[pl.BlockSpec((tm, k), lambda i, j: (i, 0)),
                  pl.BlockSpec((k, tn), lambda i, j: (0, j))],
        out_specs=pl.BlockSpec((tm, tn), lambda i, j: (i, j)),
        out_shape=jax.ShapeDtypeStruct((m, n), out_dtype),
        compiler_params=_params("parallel", "parallel"),
        name="matmul_nn",
    )(a, b)


def _norm_rope_t(x, gain, cos, sin, scale):
    x = x.astype(F32)
    ms = jnp.mean(x * x, axis=0, keepdims=True)
    xn = x * lax.rsqrt(ms + NORM_EPS) * gain
    sw = jnp.concatenate([xn[16:32], xn[0:16], xn[48:64], xn[32:48]], axis=0)
    out = xn * cos + sw * sin
    return out * scale if scale != 1.0 else out


def _attn_prep_tile(y_ref, lanes, i, cos, sin, gain_ref, qda_ref, kda_ref, vda_ref, qgq_ref, kgq_ref, vgq_ref):
    ones = jnp.ones((ONES_ROWS, TILE), F32)
    qscale = (DA_HEAD_DIM ** -0.5) * LOG2E
    hd = DA_HEAD_DIM
    for h in range(DA_HEADS):
        halves_q, halves_k = [], []
        for half in range(2):
            r = h * 2 * hd + half * hd
            halves_q.append(_norm_rope_t(y_ref[r:r + hd, lanes], gain_ref[0], cos, sin, qscale))
            halves_k.append(_norm_rope_t(y_ref[DA_COLS + r:DA_COLS + r + hd, lanes], gain_ref[1], cos, sin, 1.0))
        qda_ref[i, h] = jnp.concatenate(halves_q, axis=0).astype(BF16)
        kda_ref[i, h] = jnp.concatenate(halves_k, axis=0).T.astype(BF16)
        r = 2 * DA_COLS + h * 2 * hd
        vda_ref[i, h] = jnp.concatenate([y_ref[r:r + 2 * hd, lanes], ones], axis=0).astype(BF16)
    base = 3 * DA_COLS
    gscale = (GQ_HEAD_DIM ** -0.5) * LOG2E
    for h in range(GQ_HEADS):
        r = base + h * GQ_HEAD_DIM
        qgq_ref[i, h] = _norm_rope_t(y_ref[r:r + GQ_HEAD_DIM, lanes], gain_ref[2], cos, sin, gscale).astype(BF16)
    base += GQ_Q_COLS
    ks = [_norm_rope_t(y_ref[base + h * GQ_HEAD_DIM:base + (h + 1) * GQ_HEAD_DIM, lanes], gain_ref[3], cos, sin, 1.0)
          for h in range(GQ_KV_HEADS)]
    kgq_ref[i] = jnp.concatenate(ks, axis=0).T.astype(BF16)
    base += GQ_KV_COLS
    for h in range(GQ_KV_HEADS):
        r = base + h * GQ_HEAD_DIM
        vgq_ref[i, h] = jnp.concatenate([y_ref[r:r + GQ_HEAD_DIM, lanes], ones], axis=0).astype(BF16)


def _inproj_attn_kernel(wt_ref, h_ref, cos_ref, sin_ref, gain_ref,
                        qda_ref, kda_ref, vda_ref, qgq_ref, kgq_ref, vgq_ref, y_ref):
    n_sub = h_ref.shape[0] // TILE
    lanes = [slice(i * TILE, (i + 1) * TILE) for i in range(n_sub)]
    for i in range(n_sub + 1):
        if i < n_sub:
            y_ref[i] = _dot_nt(wt_ref[...], h_ref[lanes[i], :])
        if i > 0:
            j = i - 1
            _attn_prep_tile(y_ref.at[j], slice(None), j, cos_ref[:, lanes[j]], sin_ref[:, lanes[j]], gain_ref,
                            qda_ref, kda_ref, vda_ref, qgq_ref, kgq_ref, vgq_ref)


def _inproj_attn(wt, h, cos_m, sin_m, gains):
    m, k = h.shape
    tm = _row_tile(m)
    sub = tm // TILE
    s_tiles = m // TILE
    out_shape = (
        jax.ShapeDtypeStruct((s_tiles, DA_HEADS, 2 * DA_HEAD_DIM, TILE), BF16),
        jax.ShapeDtypeStruct((s_tiles, DA_HEADS, TILE, 2 * DA_HEAD_DIM), BF16),
        jax.ShapeDtypeStruct((s_tiles, DA_HEADS, 2 * DA_HEAD_DIM + ONES_ROWS, TILE), BF16),
        jax.ShapeDtypeStruct((s_tiles, GQ_HEADS, GQ_HEAD_DIM, TILE), BF16),
        jax.ShapeDtypeStruct((s_tiles, TILE, GQ_KV_COLS), BF16),
        jax.ShapeDtypeStruct((s_tiles, GQ_KV_HEADS, GQ_HEAD_DIM + ONES_ROWS, TILE), BF16),
    )
    blk = lambda sds: pl.BlockSpec((sub,) + sds.shape[1:], lambda s: (s,) + (0,) * (len(sds.shape) - 1))
    return pl.pallas_call(
        _inproj_attn_kernel,
        grid=(m // tm,),
        in_specs=[pl.BlockSpec((ATTN_ROWS, k), lambda s: (0, 0)),
                  pl.BlockSpec((tm, k), lambda s: (s, 0)),
                  pl.BlockSpec((GQ_HEAD_DIM, tm), lambda s: (0, s)),
                  pl.BlockSpec((GQ_HEAD_DIM, tm), lambda s: (0, s)),
                  pl.BlockSpec((4, GQ_HEAD_DIM, TILE), lambda s: (0, 0, 0))],
        out_specs=tuple(blk(s) for s in out_shape),
        out_shape=out_shape,
        scratch_shapes=[pltpu.VMEM((sub, ATTN_ROWS, TILE), F32)],
        compiler_params=_params("parallel"),
        name="inproj_attn",
    )(wt, h, cos_m, sin_m, gains)


def _softmax_update(s_ref, idx, m_tile, m_old):
    m_new = jnp.maximum(m_old, m_tile)
    alpha = jnp.exp2(m_old - m_new)
    p = jnp.exp2(s_ref[idx] - m_new)
    return p.astype(BF16), alpha, m_new


def _attention_program(n_t, n_mat, scores, consume, finalize):
    neg = (jnp.full((1, TILE), NEG_BIG, F32),) * n_mat
    consume(0, 0, scores(0, 0, 0), neg)
    finalize(0)
    half = (n_t - 1) // 2
    pairs_per_iter = next(u for u in (4, 2, 1) if half % u == 0)
    mt0 = scores(1, 0, 0)

    def query_pair(qp, mt):
        for par in (0, 1):
            qi = 1 + 2 * qp + par
            a, b = (0, 1) if par == 0 else (1, 0)

            def key_pairs(i, carry, qi=qi, a=a, b=b):
                mt_a, ms = carry[:n_mat], carry[n_mat:]
                for u in range(pairs_per_iter):
                    j = 2 * (pairs_per_iter * i + u)
                    mt_b = scores(qi, j + 1, b)
                    ms = consume(j, a, mt_a, ms)
                    mt_a = scores(qi, j + 2, a)
                    ms = consume(j + 1, b, mt_b, ms)
                return tuple(mt_a) + tuple(ms)

            carry = lax.fori_loop(0, half // pairs_per_iter, key_pairs, tuple(mt) + neg)
            mt = scores(jnp.minimum(qi + 1, n_t - 1), 0, b)
            consume(n_t - 1, a, carry[:n_mat], carry[n_mat:])
            finalize(qi)
        return tuple(mt)

    lax.fori_loop(0, half, query_pair, tuple(mt0))


DA_HEADS_PER_STEP = 2


def _flash_da_kernel(q_ref, k_ref, v_ref, lam_ref, g_ref, o_ref, qpad_ref, acc_ref, sa_ref, sb_ref,
                     *, n_t, lam_init):
    dv = 2 * DA_HEAD_DIM
    hps = DA_HEADS_PER_STEP
    bufs = (sa_ref, sb_ref)
    acc_ref[...] = jnp.zeros_like(acc_ref)
    top = lax.broadcasted_iota(jnp.int32, (dv, TILE), 0) < DA_HEAD_DIM

    def pad_queries(qi, carry):
        for hh in range(hps):
            q = q_ref[qi, hh]
            zero = jnp.zeros_like(q)
            qpad_ref[qi, 2 * hh] = jnp.where(top, q, zero)
            qpad_ref[qi, 2 * hh + 1] = jnp.where(top, zero, q)
        return carry

    lax.fori_loop(0, n_t, pad_queries, 0)
    lp = lam_ref[...]
    lam = (jnp.exp(jnp.sum(lp[0:1] * lp[1:2], keepdims=True))
           - jnp.exp(jnp.sum(lp[2:3] * lp[3:4], keepdims=True)) + lam_init)

    def scores(qi, j, buf):
        out = []
        for hh in range(hps):
            k = k_ref[j, hh]
            for i in range(2):
                s = _dot(k, qpad_ref[qi, 2 * hh + i])
                bufs[buf][2 * hh + i] = s
                out.append(jnp.max(s, axis=0, keepdims=True))
        return out

    def consume(j, buf, mts, ms):
        out = []
        for hh in range(hps):
            vt = v_ref[j, hh]
            for i in range(2):
                c = 2 * hh + i
                p, alpha, m_new = _softmax_update(bufs[buf], c, mts[c], ms[c])
                acc_ref[c] = acc_ref[c] * alpha + _dot(vt, p)
                out.append(m_new)
        return tuple(out)

    def finalize(qi):
        rows = pl.ds(pl.multiple_of(qi * TILE, TILE), TILE)
        for hh in range(hps):
            a1, a2 = acc_ref[2 * hh], acc_ref[2 * hh + 1]
            o = a1[0:dv] / a1[dv:dv + 1] - lam * (a2[0:dv] / a2[dv:dv + 1])
            ms = jnp.mean(o * o, axis=0, keepdims=True)
            o = o * lax.rsqrt(ms + NORM_EPS) * g_ref[...] * (1.0 - lam_init)
            o_ref[rows, hh * dv:(hh + 1) * dv] = o.T.astype(o_ref.dtype)
        acc_ref[...] = jnp.zeros_like(acc_ref)

    _attention_program(n_t, 2 * hps, scores, consume, finalize)


def _flash_da(qda, kda, vda, lam_p, subln_g, batch, n_t, lam_init):
    dv = 2 * DA_HEAD_DIM
    hps = DA_HEADS_PER_STEP
    m = batch * n_t * TILE
    kern = functools.partial(_flash_da_kernel, n_t=n_t, lam_init=lam_init)
    return pl.pallas_call(
        kern,
        grid=(batch, DA_HEADS // hps),
        in_specs=[pl.BlockSpec((n_t, hps, dv, TILE), lambda b, h: (b, h, 0, 0)),
                  pl.BlockSpec((n_t, hps, TILE, dv), lambda b, h: (b, h, 0, 0)),
                  pl.BlockSpec((n_t, hps, dv + ONES_ROWS, TILE), lambda b, h: (b, h, 0, 0)),
                  pl.BlockSpec((4, DA_HEAD_DIM), lambda b, h: (0, 0)),
                  pl.BlockSpec((dv, TILE), lambda b, h: (0, 0))],
        out_specs=pl.BlockSpec((n_t * TILE, hps * dv), lambda b, h: (b, h)),
        out_shape=jax.ShapeDtypeStruct((m, DA_HEADS * dv), BF16),
        scratch_shapes=[pltpu.VMEM((n_t, 2 * hps, dv, TILE), BF16),
                        pltpu.VMEM((2 * hps, dv + ONES_ROWS, TILE), F32),
                        pltpu.VMEM((2 * hps, TILE, TILE), F32), pltpu.VMEM((2 * hps, TILE, TILE), F32)],
        compiler_params=_params("parallel", "parallel"),
        name="flash_da",
    )(qda, kda, vda, lam_p, subln_g)


def _flash_gq_kernel(q_ref, k_ref, v_ref, o_ref, qpad_ref, acc_ref, sa_ref, sb_ref, *, n_t):
    first = pl.program_id(1) == 0
    group = GQ_HEADS // GQ_KV_HEADS
    dv = GQ_HEAD_DIM
    bufs = (sa_ref, sb_ref)
    acc_ref[...] = jnp.zeros_like(acc_ref)

    def pad_queries(qi, carry):
        for g in range(group):
            q = q_ref[qi, g]
            zero = jnp.zeros_like(q)
            qpad_ref[qi, g] = jnp.concatenate([jnp.where(first, q, zero), jnp.where(first, zero, q)], axis=0)
        return carry

    lax.fori_loop(0, n_t, pad_queries, 0)

    def scores(qi, j, buf):
        k = k_ref[j]
        out = []
        for g in range(group):
            s = _dot(k, qpad_ref[qi, g])
            bufs[buf][g] = s
            out.append(jnp.max(s, axis=0, keepdims=True))
        return out

    def consume(j, buf, mts, ms):
        vt = v_ref[j, 0]
        out = []
        for g in range(group):
            p, alpha, m_new = _softmax_update(bufs[buf], g, mts[g], ms[g])
            acc_ref[g] = acc_ref[g] * alpha + _dot(vt, p)
            out.append(m_new)
        return tuple(out)

    def finalize(qi):
        rows = pl.ds(pl.multiple_of(qi * TILE, TILE), TILE)
        o = jnp.concatenate([acc_ref[g, 0:dv, :] / acc_ref[g, dv:dv + 1, :] for g in range(group)], axis=0)
        o_ref[rows, :] = o.T.astype(o_ref.dtype)
        acc_ref[...] = jnp.zeros_like(acc_ref)

    _attention_program(n_t, group, scores, consume, finalize)


def _flash_gq(qgq, kgq, vgq, batch, n_t):
    group = GQ_HEADS // GQ_KV_HEADS
    m = batch * n_t * TILE
    kern = functools.partial(_flash_gq_kernel, n_t=n_t)
    return pl.pallas_call(
        kern,
        grid=(batch, GQ_KV_HEADS),
        in_specs=[pl.BlockSpec((n_t, group, GQ_HEAD_DIM, TILE), lambda b, h: (b, h, 0, 0)),
                  pl.BlockSpec((n_t, TILE, GQ_KV_COLS), lambda b, h: (b, 0, 0)),
                  pl.BlockSpec((n_t, 1, GQ_HEAD_DIM + ONES_ROWS, TILE), lambda b, h: (b, h, 0, 0))],
        out_specs=pl.BlockSpec((n_t * TILE, group * GQ_HEAD_DIM), lambda b, h: (b, h)),
        out_shape=jax.ShapeDtypeStruct((m, GQ_Q_COLS), BF16),
        scratch_shapes=[pltpu.VMEM((n_t, group, 2 * GQ_HEAD_DIM, TILE), BF16),
                        pltpu.VMEM((group, GQ_HEAD_DIM + ONES_ROWS, TILE), F32),
                        pltpu.VMEM((group, TILE, TILE), F32), pltpu.VMEM((group, TILE, TILE), F32)],
        compiler_params=_params("parallel", "parallel"),
        name="flash_gq",
    )(qgq, kgq, vgq)


def _conv3(x, prev_row, next_row, w_ref):
    n = x.shape[0]
    rid = lax.broadcasted_iota(jnp.int32, x.shape, 0)
    x_m1 = jnp.where(rid == 0, prev_row, pltpu.roll(x, 1, 0))
    x_p1 = jnp.where(rid == n - 1, next_row, pltpu.roll(x, n - 1, 0))
    return x_m1 * w_ref[0:1, :] + x * w_ref[1:2, :] + x_p1 * w_ref[2:3, :]


def _halo_rows(prev_ref, next_ref, j, n_t):
    has_prev = (j >= 2).astype(F32)
    has_next = jnp.logical_and(j >= 1, j <= n_t - 2).astype(F32)
    prev_row = prev_ref[7:8, :].astype(F32) * has_prev
    next_row = next_ref[0:1, :].astype(F32) * has_next
    return prev_row, next_row


def _halo_specs(cols, col_blk, n_rows8):
    prev = pl.BlockSpec((8, cols), lambda s: (jnp.maximum(s * (TILE // 8) - 1, 0), col_blk))
    nxt = pl.BlockSpec((8, cols), lambda s: (jnp.minimum((s + 1) * (TILE // 8), n_rows8 - 1), col_blk))
    return prev, nxt


def _dn_prep_kernel(x_ref, prev_ref, next_ref, ba_ref, cw_ref, alog_ref, dtb_ref,
                    q_ref, k_ref, v_ref, gb_ref, *, n_t):
    j = pl.program_id(0) % n_t
    prev_row, next_row = _halo_rows(prev_ref, next_ref, j, n_t)
    y = _silu(_conv3(x_ref[...].astype(F32), prev_row, next_row, cw_ref))
    for h in range(DN_HEADS):
        c = h * DN_HEAD_DIM
        q = y[:, c:c + DN_HEAD_DIM]
        k = y[:, DN_WIDTH + c:DN_WIDTH + c + DN_HEAD_DIM]
        qn = q * (lax.rsqrt(jnp.sum(q * q, axis=-1, keepdims=True) + NORM_EPS) * (DN_HEAD_DIM ** -0.5))
        q_ref[:, c:c + DN_HEAD_DIM] = qn.astype(q_ref.dtype)
        kn = k * lax.rsqrt(jnp.sum(k * k, axis=-1, keepdims=True) + NORM_EPS)
        k_ref[:, c:c + DN_HEAD_DIM] = kn.astype(k_ref.dtype)
    v_ref[...] = y[:, 2 * DN_WIDTH:].astype(v_ref.dtype)
    ba = ba_ref[...]
    beta = _sigmoid(ba)
    z = ba + dtb_ref[...]
    softplus = jnp.maximum(z, 0.0) + jnp.log(1.0 + jnp.exp(-jnp.abs(z)))
    g = -jnp.exp(alog_ref[...]) * softplus
    col = lax.broadcasted_iota(jnp.int32, ba.shape, 1)
    gb_ref[...] = jnp.where(col < 2 * DN_HEADS, beta, g)


def _dn_prep(y_nat, y_ba, conv_w, alog_tab, dtb_tab, n_t):
    m = y_nat.shape[0]
    w3 = 3 * DN_WIDTH
    prev, nxt = _halo_specs(w3, NAT_QKV_OFF // w3, m // 8)
    tok = lambda c: pl.BlockSpec((TILE, c), lambda s: (s, 0))
    return pl.pallas_call(
        functools.partial(_dn_prep_kernel, n_t=n_t),
        grid=(m // TILE,),
        in_specs=[pl.BlockSpec((TILE, w3), lambda s: (s, NAT_QKV_OFF // w3)), prev, nxt,
                  tok(BA_COLS),
                  pl.BlockSpec((3, w3), lambda s: (0, 0)),
                  pl.BlockSpec((1, BA_COLS), lambda s: (0, 0)),
                  pl.BlockSpec((1, BA_COLS), lambda s: (0, 0))],
        out_specs=(tok(DN_WIDTH), tok(DN_WIDTH), tok(DN_WIDTH), tok(BA_COLS)),
        out_shape=(jax.ShapeDtypeStruct((m, DN_WIDTH), BF16),) * 3 + (jax.ShapeDtypeStruct((m, BA_COLS), F32),),
        compiler_params=_params("parallel"),
        name="dn_prep",
    )(y_nat, y_nat, y_nat, y_ba, conv_w, alog_tab, dtb_tab)


def _dn_scan_kernel(qf_ref, kf_ref, vf_ref, gf_ref, qb_ref, kb_ref, vb_ref, gb_ref, of_ref, ob_ref,
                    s_ref, a16_ref, tinv_ref, t16_ref, w16_ref, qk16_ref, rhs16_ref, u_ref, uw16_ref,
                    kt16_ref, qd16_ref, kwb_ref, sc16_ref, vnew16_ref, inter_ref):
    t = pl.program_id(1)

    @pl.when(t == 0)
    def _():
        s_ref[...] = jnp.zeros_like(s_ref)

    n_chunk = TILE // CHUNK
    shift = CHUNK.bit_length() - 1
    dh = DN_HEAD_DIM
    ii = lax.broadcasted_iota(jnp.int32, (TILE, TILE), 0)
    jj = lax.broadcasted_iota(jnp.int32, (TILE, TILE), 1)
    same = (ii >> shift) == (jj >> shift)
    eye = (ii == jj).astype(F32)
    pair0 = (ii >> 1) == (jj >> 1)
    ins = ((qf_ref, kf_ref, vf_ref, gf_ref, of_ref), (qb_ref, kb_ref, vb_ref, gb_ref, ob_ref))
    chains = [(d, h) for d in range(2) for h in range(DN_HEADS)]
    incls = (jnp.logical_and(same, ii >= jj), jnp.logical_and(same, ii <= jj))
    stricts = (jnp.logical_and(same, ii > jj), jnp.logical_and(same, ii < jj))
    as_bf16 = lambda mask: jnp.where(mask, 1.0, 0.0).astype(BF16)
    same16 = as_bf16(same)
    n_g = 4 * DN_HEADS
    zpad = jnp.zeros((BA_COLS - n_g, TILE), F32)

    gls = []
    for d in range(2):
        incl, strict = incls[d], stricts[d]
        gb = ins[d][3][...]
        g_t = gb.T[0:n_g]
        p0 = g_t.astype(BF16)
        r1 = g_t - p0.astype(F32)
        p1 = r1.astype(BF16)
        p2 = (r1 - p1.astype(F32)).astype(BF16)
        terms = jnp.concatenate([p0, p1, p2], axis=0)

        def masked_sum(mask16):
            r = _dot(terms, mask16)
            return r[0:n_g] + r[n_g:2 * n_g] + r[2 * n_g:3 * n_g]

        gcum_t = masked_sum(as_bf16(incls[1 - d]))
        gtot_t = masked_sum(same16)
        gcum = jnp.concatenate([gcum_t, zpad], axis=0).T
        gtot = jnp.concatenate([gtot_t, zpad], axis=0).T
        for h in range(DN_HEADS):
            ch = d * DN_HEADS + h
            cg = 2 * DN_HEADS + ch
            lanes = slice(h * dh, (h + 1) * dh)
            q16 = ins[d][0][:, lanes]
            k16 = ins[d][1][:, lanes]
            q = q16.astype(F32)
            k = k16.astype(F32)
            v = ins[d][2][:, lanes].astype(F32)
            beta = gb[:, ch:ch + 1]
            gc = gcum[:, cg:cg + 1]
            gr = gcum_t[cg:cg + 1, :]
            gl = gtot[:, cg:cg + 1]
            gls.append(gl)
            decay = jnp.where(incl, jnp.exp(jnp.where(incl, gc - gr, 0.0)), 0.0)
            kb = k * beta
            a = jnp.where(strict, _dot_nt(kb.astype(BF16), k16) * decay, 0.0)
            a16_ref[ch] = a.astype(BF16)
            t0 = eye - jnp.where(pair0, a, 0.0)
            tinv_ref[ch] = t0
            t16_ref[ch] = t0.astype(BF16)
            qk16_ref[ch] = jnp.where(incl, _dot_nt(q16, k16) * decay, 0.0).astype(BF16)
            egc = jnp.exp(gc)
            rhs16_ref[ch] = jnp.concatenate([v * beta, kb * egc], axis=1).astype(BF16)
            kt16_ref[ch] = (k * jnp.exp(gl - gc)).astype(BF16)
            qd16_ref[ch] = (q * egc).astype(BF16)

    lvl = 1
    while (2 << lvl) <= CHUNK:
        sib = jnp.logical_and((ii >> (lvl + 1)) == (jj >> (lvl + 1)), (ii >> lvl) != (jj >> lvl))
        for ch in range(len(chains)):
            w16_ref[ch] = _dot(a16_ref[ch], t16_ref[ch]).astype(BF16)
        for ch in range(len(chains)):
            tn = jnp.where(sib, -_dot(t16_ref[ch], w16_ref[ch]), tinv_ref[ch])
            tinv_ref[ch] = tn
            t16_ref[ch] = tn.astype(BF16)
        lvl += 1

    for ch in range(len(chains)):
        uw = _dot(t16_ref[ch], rhs16_ref[ch])
        u_ref[ch] = uw[:, 0:dh]
        uw16_ref[ch] = uw.astype(BF16)

    for ch in range(len(chains)):
        for c in range(n_chunk):
            rows = slice(c * CHUNK, (c + 1) * CHUNK)
            kwb_ref[ch, c] = _dot_tn(kt16_ref[ch, rows, :], uw16_ref[ch, rows, :])

    states = [s_ref[ch] for ch in range(len(chains))]
    for step in range(n_chunk):
        for ch, (d, h) in enumerate(chains):
            c = n_chunk - 1 - step if d else step
            s16 = states[ch].astype(BF16)
            sc16_ref[ch, c] = s16
            egl = jnp.exp(gls[ch][c * CHUNK:c * CHUNK + 1, :])
            states[ch] = (states[ch] * egl - _dot(kwb_ref[ch, c, :, dh:].astype(BF16), s16)
                          + kwb_ref[ch, c, :, 0:dh])
    for ch in range(len(chains)):
        s_ref[ch] = states[ch]

    for ch in range(len(chains)):
        for c in range(n_chunk):
            rows = slice(c * CHUNK, (c + 1) * CHUNK)
            s16 = sc16_ref[ch, c]
            vnew16_ref[ch, rows, :] = (u_ref[ch, rows, :] - _dot(uw16_ref[ch, rows, dh:], s16)).astype(BF16)
            inter_ref[ch, rows, :] = _dot(qd16_ref[ch, rows, :], s16)
    for ch, (d, h) in enumerate(chains):
        ins[d][4][:, h * dh:(h + 1) * dh] = inter_ref[ch] + _dot(qk16_ref[ch], vnew16_ref[ch])


def _dn_scan(q, k, v, gb, batch, n_t):
    m = q.shape[0]
    fwd = lambda b, t: (b * n_t + t, 0)
    bwd = lambda b, t: (b * n_t + jnp.where(t == 0, 0, n_t - t), 0)
    specs = lambda tile: [pl.BlockSpec((TILE, DN_WIDTH), tile)] * 3 + [pl.BlockSpec((TILE, 128), tile)]
    nc = 2 * DN_HEADS
    dh = DN_HEAD_DIM
    big = lambda dt: pltpu.VMEM((nc, TILE, TILE), dt)
    half = lambda dt: pltpu.VMEM((nc, TILE, dh), dt)
    return pl.pallas_call(
        _dn_scan_kernel,
        grid=(batch, n_t),
        in_specs=specs(fwd) + specs(bwd),
        out_specs=(pl.BlockSpec((TILE, DN_WIDTH), fwd), pl.BlockSpec((TILE, DN_WIDTH), bwd)),
        out_shape=(jax.ShapeDtypeStruct((m, DN_WIDTH), F32),) * 2,
        scratch_shapes=[pltpu.VMEM((nc, dh, dh), F32),
                        big(BF16), big(F32), big(BF16), big(BF16), big(BF16), big(BF16),
                        half(F32), big(BF16), half(BF16), half(BF16),
                        pltpu.VMEM((nc, TILE // CHUNK, dh, 2 * dh), F32),
                        pltpu.VMEM((nc, TILE // CHUNK, dh, dh), BF16),
                        half(BF16), half(F32)],
        compiler_params=_params("parallel", "arbitrary"),
        name="dn_scan",
    )(q, k, v, gb, q, k, v, gb)


def _merge_kernel(x_ref, oa_ref, ob_ref, ocf_ref, ocb_ref, z_ref, gate_ref, dng_ref, mod_ref,
                  wa_ref, wb_ref, wc_ref, wo_ref, o_ref):
    d = x_ref.shape[1]
    oc = ocf_ref[...] + ocb_ref[...]
    z = z_ref[...].astype(F32)
    parts = []
    for h in range(DN_HEADS):
        lanes = slice(h * DN_HEAD_DIM, (h + 1) * DN_HEAD_DIM)
        o = oc[:, lanes]
        ms = jnp.mean(o * o, axis=-1, keepdims=True)
        parts.append(o * lax.rsqrt(ms + NORM_EPS) * dng_ref[...] * _silu(z[:, lanes]))
    oc_n = jnp.concatenate(parts, axis=1).astype(BF16)
    gates = _sigmoid(gate_ref[...].astype(F32))
    mix = (gates[:, 0:d] * _dot(oa_ref[...], wa_ref[...])
           + gates[:, d:2 * d] * _dot(ob_ref[...], wb_ref[...])
           + gates[:, 2 * d:3 * d] * _dot(oc_n, wc_ref[...]))
    o_ref[...] = x_ref[...] + mod_ref[0] * _dot(mix.astype(BF16), wo_ref[...])


def _merge(x, oa, ob, ocf, ocb, y_nat, dn_g, mod_tab, wa, wb, wc, wo):
    m, d = x.shape
    tok = lambda c, cb=0: pl.BlockSpec((TILE, c), lambda s: (s, cb))
    full = lambda a: pl.BlockSpec(a.shape, lambda s: (0,) * a.ndim)
    return pl.pallas_call(
        _merge_kernel,
        grid=(m // TILE,),
        in_specs=[tok(d), tok(DA_COLS), tok(GQ_Q_COLS), tok(DN_WIDTH), tok(DN_WIDTH),
                  tok(DN_WIDTH, NAT_Z_OFF // DN_WIDTH), tok(3 * d, NAT_GATE_OFF),
                  full(dn_g), pl.BlockSpec((1, 1, d), lambda s: (s, 0, 0)),
                  full(wa), full(wb), full(wc), full(wo)],
        out_specs=tok(d),
        out_shape=jax.ShapeDtypeStruct((m, d), F32),
        compiler_params=_params("parallel"),
        name="merge",
    )(x, oa, ob, ocf, ocb, y_nat, y_nat, dn_g, mod_tab, wa, wb, wc, wo)


def _ffn_out_kernel(x_ref, a_ref, prev_ref, next_ref, u_ref, cw_ref, cb_ref, mod_ref, w2_ref, o_ref, *, n_t):
    j = pl.program_id(0) % n_t
    prev_row, next_row = _halo_rows(prev_ref, next_ref, j, n_t)
    a = _conv3(a_ref[...].astype(F32), prev_row, next_row, cw_ref) + cb_ref[...]
    act = (_silu(a) * u_ref[...].astype(F32)).astype(BF16)
    o_ref[...] = x_ref[...] + mod_ref[0] * _dot(act, w2_ref[...])


def _ffn_out(x, au, conv_w, conv_b, mod_tab, w2, n_t):
    m, d = x.shape
    dff = w2.shape[0]
    prev, nxt = _halo_specs(dff, 0, m // 8)
    return pl.pallas_call(
        functools.partial(_ffn_out_kernel, n_t=n_t),
        grid=(m // TILE,),
        in_specs=[pl.BlockSpec((TILE, d), lambda s: (s, 0)),
                  pl.BlockSpec((TILE, dff), lambda s: (s, 0)), prev, nxt,
                  pl.BlockSpec((TILE, dff), lambda s: (s, 1)),
                  pl.BlockSpec((3, dff), lambda s: (0, 0)),
                  pl.BlockSpec((1, dff), lambda s: (0, 0)),
                  pl.BlockSpec((1, 1, d), lambda s: (s, 0, 0)),
                  pl.BlockSpec((dff, d), lambda s: (0, 0))],
        out_specs=pl.BlockSpec((TILE, d), lambda s: (s, 0)),
        out_shape=jax.ShapeDtypeStruct((m, d), F32),
        compiler_params=_params("parallel"),
        name="ffn_out",
    )(x, au, au, au, au, conv_w, conv_b.reshape(1, dff), mod_tab, w2)


def _rope_tables(n_lat):
    t = jnp.arange(n_lat, dtype=jnp.int32)
    row = (t // GRID_W).astype(F32)
    col = (t % GRID_W).astype(F32)
    d_axis = 32
    inv_freq = ROPE_THETA ** (-jnp.arange(0, d_axis, 2, dtype=F32) / d_axis)
    ang_r = row[None, :] * inv_freq[:, None]
    ang_c = col[None, :] * inv_freq[:, None]
    cr, sr, cc, sc = jnp.cos(ang_r), jnp.sin(ang_r), jnp.cos(ang_c), jnp.sin(ang_c)
    cos_lat = jnp.concatenate([cr, cr, cc, cc], axis=0)
    sin_lat = jnp.concatenate([-sr, sr, -sc, sc], axis=0)
    cos_t = jnp.concatenate([jnp.ones((64, TILE), F32), cos_lat], axis=1)
    sin_t = jnp.concatenate([jnp.zeros((64, TILE), F32), sin_lat], axis=1)
    return cos_t, sin_t


def kernel(x, c, ctx, c_ctx, w_mod, b_mod, norm1_g, w_in, da_qn_g, da_kn_g, da_lambda, da_subln_g, gq_qn_g, gq_kn_g, dn_conv_w, dn_a_log, dn_dt_bias, dn_norm_g, w_br_a, w_br_b, w_br_c, w_o, norm2_g, ffn_w1, ffn_conv_w, ffn_conv_b, ffn_w2):
    batch, n_lat, d = x.shape
    depth = w_mod.shape[0]
    dff = ffn_w2.shape[1]
    assert ctx.shape[1] == TILE and n_lat % TILE == 0 and n_lat % GRID_W == 0 and batch + 1 <= 8
    n_t = 1 + n_lat // TILE
    assert n_t % 2 == 1, "the attention key loop handles key tiles in pairs plus one"
    s_tiles = batch * n_t
    m = s_tiles * TILE

    xs = jnp.concatenate([ctx, x], axis=1).reshape(m, d)

    cond = jnp.zeros((8, d), F32).at[:batch].set(c).at[batch].set(c_ctx)
    mod = _modulation(cond, w_mod, b_mod)
    tile_id = jnp.arange(s_tiles)
    row_of_tile = jnp.where(tile_id % n_t == 0, batch, tile_id // n_t)
    mod_tabs = mod[:, row_of_tile, :].reshape(depth, s_tiles, 6, 1, d)

    cos_t, sin_t = _rope_tables(n_lat)
    cos_m, sin_m = jnp.tile(cos_t, (1, batch)), jnp.tile(sin_t, (1, batch))
    bcast = lambda g: jnp.broadcast_to(g[:, None], (g.shape[0], TILE))

    o_qkv = ATTN_ROWS
    o_b = o_qkv + 3 * DN_WIDTH
    o_z = o_b + 4 * DN_HEADS
    o_g = o_z + DN_WIDTH
    wt_attn = jnp.swapaxes(w_in[:, :, :ATTN_ROWS], 1, 2).astype(BF16)
    w_nat = jnp.concatenate([w_in[:, :, o_g:], w_in[:, :, o_qkv:o_b], w_in[:, :, o_z:o_g]], axis=2).astype(BF16)
    w_ba = jnp.concatenate([w_in[:, :, o_b:o_z], jnp.zeros((depth, d, BA_COLS - 4 * DN_HEADS), F32)],
                           axis=2).astype(BF16)
    pad_tab = lambda a: jnp.zeros((1, BA_COLS), F32).at[0, 2 * DN_HEADS:4 * DN_HEADS].set(a.reshape(-1))

    for l in range(depth):
        lam_init = 0.8 - 0.6 * math.exp(-0.3 * l)
        tab = lambda i: mod_tabs[l, :, i]

        h1 = _norm_mod(xs, norm1_g[l], tab(0), tab(1))
        y_nat = _matmul_nn(h1, w_nat[l], NAT_COLS // 2, BF16)
        y_ba = _matmul_nn(h1, w_ba[l], BA_COLS, F32)

        gains = jnp.stack([bcast(da_qn_g[l]), bcast(da_kn_g[l]), bcast(gq_qn_g[l]), bcast(gq_kn_g[l])])
        qda, kda, vda, qgq, kgq, vgq = _inproj_attn(wt_attn[l], h1, cos_m, sin_m, gains)
        oa = _flash_da(qda, kda, vda, da_lambda[l], bcast(da_subln_g[l]), batch, n_t, lam_init)
        ob = _flash_gq(qgq, kgq, vgq, batch, n_t)

        dq, dk, dv, gb = _dn_prep(y_nat, y_ba, dn_conv_w[l], pad_tab(dn_a_log[l]), pad_tab(dn_dt_bias[l]), n_t)
        ocf, ocb = _dn_scan(dq, dk, dv, gb, batch, n_t)

        xs = _merge(xs, oa, ob, ocf, ocb, y_nat, dn_norm_g[l].reshape(1, -1), tab(2),
                    w_br_a[l].astype(BF16), w_br_b[l].astype(BF16), w_br_c[l].astype(BF16), w_o[l].astype(BF16))

        h2 = _norm_mod(xs, norm2_g[l], tab(3), tab(4))
        au = _matmul_nn(h2, ffn_w1[l].astype(BF16), dff, BF16)
        xs = _ffn_out(xs, au, ffn_conv_w[l], ffn_conv_b[l], tab(5), ffn_w2[l].astype(BF16), n_t)

    return xs.reshape(batch, n_t * TILE, d)[:, TILE:, :]
```

```python
import functools
import math

import jax
import jax.numpy as jnp
from jax import lax
from jax.experimental import pallas as pl
from jax.experimental.pallas import tpu as pltpu

F32 = jnp.float32
BF16 = jnp.bfloat16

TILE = 256
CHUNK = 64
GRID_W = 64
ROPE_THETA = 10000.0
NORM_EPS = 1e-6
DA_HEADS, DA_HEAD_DIM = 4, 64
GQ_HEADS, GQ_KV_HEADS, GQ_HEAD_DIM = 8, 2, 64
DN_HEADS, DN_HEAD_DIM = 4, 128
LOG2E = 1.4426950408889634
NEG_BIG = -1e30
ONES_ROWS = 16
VMEM_LIMIT = 48 * 1024 * 1024

DA_COLS = DA_HEADS * 2 * DA_HEAD_DIM
GQ_Q_COLS = GQ_HEADS * GQ_HEAD_DIM
GQ_KV_COLS = GQ_KV_HEADS * GQ_HEAD_DIM
DN_WIDTH = DN_HEADS * DN_HEAD_DIM
ATTN_ROWS = 3 * DA_COLS + GQ_Q_COLS + 2 * GQ_KV_COLS
NAT_GATE_OFF, NAT_QKV_OFF, NAT_Z_OFF = 0, 3072, 4608
NAT_COLS = 5120
BA_COLS = 128


def _params(*sem):
    return pltpu.CompilerParams(dimension_semantics=sem, vmem_limit_bytes=VMEM_LIMIT)


def _sigmoid(x):
    return 1.0 / (1.0 + jnp.exp(-x))


def _silu(x):
    return x * _sigmoid(x)


def _dot(a, b):
    return jnp.dot(a, b, preferred_element_type=F32)


def _dot_nt(a, b):
    return lax.dot_general(a, b, (((1,), (1,)), ((), ())), preferred_element_type=F32)


def _dot_tn(a, b):
    return lax.dot_general(a, b, (((0,), (0,)), ((), ())), preferred_element_type=F32)


def _mod_kernel(c_ref, w_ref, b_ref, o_ref):
    cond = _silu(c_ref[...])
    o_ref[0] = jnp.dot(cond, w_ref[0], preferred_element_type=F32,
                       precision=lax.Precision.HIGHEST) + b_ref[0]


def _modulation(cond, w_mod, b_mod):
    depth, d, n = w_mod.shape
    tn = 1536
    return pl.pallas_call(
        _mod_kernel,
        grid=(depth, n // tn),
        in_specs=[pl.BlockSpec((8, d), lambda l, j: (0, 0)),
                  pl.BlockSpec((1, d, tn), lambda l, j: (l, 0, j)),
                  pl.BlockSpec((1, 1, tn), lambda l, j: (l, 0, j))],
        out_specs=pl.BlockSpec((1, 8, tn), lambda l, j: (l, 0, j)),
        out_shape=jax.ShapeDtypeStruct((depth, 8, n), F32),
        compiler_params=_params("parallel", "parallel"),
        name="modulation",
    )(cond, w_mod, b_mod.reshape(depth, 1, n))


def _norm_mod_kernel(x_ref, g_ref, shift_ref, scale_ref, o_ref):
    for i in range(x_ref.shape[0] // TILE):
        rows = slice(i * TILE, (i + 1) * TILE)
        x = x_ref[rows, :]
        ms = jnp.mean(x * x, axis=-1, keepdims=True)
        y = x * lax.rsqrt(ms + NORM_EPS) * g_ref[...]
        o_ref[rows, :] = (y * (1.0 + scale_ref[i]) + shift_ref[i]).astype(o_ref.dtype)


def _norm_mod(x, g, shift_tab, scale_tab):
    m, d = x.shape
    tm = _row_tile(m)
    sub = tm // TILE
    return pl.pallas_call(
        _norm_mod_kernel,
        grid=(m // tm,),
        in_specs=[pl.BlockSpec((tm, d), lambda s: (s, 0)),
                  pl.BlockSpec((1, d), lambda s: (0, 0)),
                  pl.BlockSpec((sub, 1, d), lambda s: (s, 0, 0)),
                  pl.BlockSpec((sub, 1, d), lambda s: (s, 0, 0))],
        out_specs=pl.BlockSpec((tm, d), lambda s: (s, 0)),
        out_shape=jax.ShapeDtypeStruct((m, d), BF16),
        compiler_params=_params("parallel"),
        name="norm_mod",
    )(x, g.reshape(1, d), shift_tab, scale_tab)


def _mm_nn_kernel(a_ref, b_ref, o_ref):
    o_ref[...] = _dot(a_ref[...], b_ref[...]).astype(o_ref.dtype)


def _row_tile(m):
    for t in (1024, 512, 256):
        if m % t == 0:
            return t
    raise ValueError(m)


def _matmul_nn(a, b, tn, out_dtype):
    m, k = a.shape
    n = b.shape[1]
    tm = _row_tile(m)
    return pl.pallas_call(
        _mm_nn_kernel,
        grid=(m // tm, n // tn),
        in_specs=[pl.BlockSpec((tm, k), lambda i, j: (i, 0)),
                  pl.BlockSpec((k, tn), lambda i, j: (0, j))],
        out_specs=pl.BlockSpec((tm, tn), lambda i, j: (i, j)),
        out_shape=jax.ShapeDtypeStruct((m, n), out_dtype),
        compiler_params=_params("parallel", "parallel"),
        name="matmul_nn",
    )(a, b)


def _norm_rope_t(x, gain, cos, sin, scale):
    x = x.astype(F32)
    ms = jnp.mean(x * x, axis=0, keepdims=True)
    xn = x * lax.rsqrt(ms + NORM_EPS) * gain
    sw = jnp.concatenate([xn[16:32], xn[0:16], xn[48:64], xn[32:48]], axis=0)
    out = xn * cos + sw * sin
    return out * scale if scale != 1.0 else out


def _attn_prep_tile(y_ref, lanes, i, cos, sin, gain_ref, qda_ref, kda_ref, vda_ref, qgq_ref, kgq_ref, vgq_ref):
    ones = jnp.ones((ONES_ROWS, TILE), F32)
    qscale = (DA_HEAD_DIM ** -0.5) * LOG2E
    hd = DA_HEAD_DIM
    for h in range(DA_HEADS):
        halves_q, halves_k = [], []
        for half in range(2):
            r = h * 2 * hd + half * hd
            halves_q.append(_norm_rope_t(y_ref[r:r + hd, lanes], gain_ref[0], cos, sin, qscale))
            halves_k.append(_norm_rope_t(y_ref[DA_COLS + r:DA_COLS + r + hd, lanes], gain_ref[1], cos, sin, 1.0))
        qda_ref[i, h] = jnp.concatenate(halves_q, axis=0).astype(BF16)
        kda_ref[i, h] = jnp.concatenate(halves_k, axis=0).T.astype(BF16)
        r = 2 * DA_COLS + h * 2 * hd
        vda_ref[i, h] = jnp.concatenate([y_ref[r:r + 2 * hd, lanes], ones], axis=0).astype(BF16)
    base = 3 * DA_COLS
    gscale = (GQ_HEAD_DIM ** -0.5) * LOG2E
    for h in range(GQ_HEADS):
        r = base + h * GQ_HEAD_DIM
        qgq_ref[i, h] = _norm_rope_t(y_ref[r:r + GQ_HEAD_DIM, lanes], gain_ref[2], cos, sin, gscale).astype(BF16)
    base += GQ_Q_COLS
    ks = [_norm_rope_t(y_ref[base + h * GQ_HEAD_DIM:base + (h + 1) * GQ_HEAD_DIM, lanes], gain_ref[3], cos, sin, 1.0)
          for h in range(GQ_KV_HEADS)]
    kgq_ref[i] = jnp.concatenate(ks, axis=0).T.astype(BF16)
    base += GQ_KV_COLS
    for h in range(GQ_KV_HEADS):
        r = base + h * GQ_HEAD_DIM
        vgq_ref[i, h] = jnp.concatenate([y_ref[r:r + GQ_HEAD_DIM, lanes], ones], axis=0).astype(BF16)


def _inproj_attn_kernel(wt_ref, h_ref, cos_ref, sin_ref, gain_ref,
                        qda_ref, kda_ref, vda_ref, qgq_ref, kgq_ref, vgq_ref, y_ref):
    n_sub = h_ref.shape[0] // TILE
    lanes = [slice(i * TILE, (i + 1) * TILE) for i in range(n_sub)]
    for i in range(n_sub + 1):
        if i < n_sub:
            y_ref[i] = _dot_nt(wt_ref[...], h_ref[lanes[i], :])
        if i > 0:
            j = i - 1
            _attn_prep_tile(y_ref.at[j], slice(None), j, cos_ref[:, lanes[j]], sin_ref[:, lanes[j]], gain_ref,
                            qda_ref, kda_ref, vda_ref, qgq_ref, kgq_ref, vgq_ref)


def _inproj_attn(wt, h, cos_m, sin_m, gains):
    m, k = h.shape
    tm = _row_tile(m)
    sub = tm // TILE
    s_tiles = m // TILE
    out_shape = (
        jax.ShapeDtypeStruct((s_tiles, DA_HEADS, 2 * DA_HEAD_DIM, TILE), BF16),
        jax.ShapeDtypeStruct((s_tiles, DA_HEADS, TILE, 2 * DA_HEAD_DIM), BF16),
        jax.ShapeDtypeStruct((s_tiles, DA_HEADS, 2 * DA_HEAD_DIM + ONES_ROWS, TILE), BF16),
        jax.ShapeDtypeStruct((s_tiles, GQ_HEADS, GQ_HEAD_DIM, TILE), BF16),
        jax.ShapeDtypeStruct((s_tiles, TILE, GQ_KV_COLS), BF16),
        jax.ShapeDtypeStruct((s_tiles, GQ_KV_HEADS, GQ_HEAD_DIM + ONES_ROWS, TILE), BF16),
    )
    blk = lambda sds: pl.BlockSpec((sub,) + sds.shape[1:], lambda s: (s,) + (0,) * (len(sds.shape) - 1))
    return pl.pallas_call(
        _inproj_attn_kernel,
        grid=(m // tm,),
        in_specs=[pl.BlockSpec((ATTN_ROWS, k), lambda s: (0, 0)),
                  pl.BlockSpec((tm, k), lambda s: (s, 0)),
                  pl.BlockSpec((GQ_HEAD_DIM, tm), lambda s: (0, s)),
                  pl.BlockSpec((GQ_HEAD_DIM, tm), lambda s: (0, s)),
                  pl.BlockSpec((4, GQ_HEAD_DIM, TILE), lambda s: (0, 0, 0))],
        out_specs=tuple(blk(s) for s in out_shape),
        out_shape=out_shape,
        scratch_shapes=[pltpu.VMEM((sub, ATTN_ROWS, TILE), F32)],
        compiler_params=_params("parallel"),
        name="inproj_attn",
    )(wt, h, cos_m, sin_m, gains)


def _softmax_update(s_ref, idx, m_tile, m_old):
    m_new = jnp.maximum(m_old, m_tile)
    alpha = jnp.exp2(m_old - m_new)
    p = jnp.exp2(s_ref[idx] - m_new)
    return p.astype(BF16), alpha, m_new


def _attention_program(n_t, n_mat, scores, consume, finalize):
    neg = (jnp.full((1, TILE), NEG_BIG, F32),) * n_mat
    consume(0, 0, scores(0, 0, 0), neg)
    finalize(0)
    half = (n_t - 1) // 2
    pairs_per_iter = next(u for u in (4, 2, 1) if half % u == 0)
    mt0 = scores(1, 0, 0)

    def query_pair(qp, mt):
        for par in (0, 1):
            qi = 1 + 2 * qp + par
            a, b = (0, 1) if par == 0 else (1, 0)

            def key_pairs(i, carry, qi=qi, a=a, b=b):
                mt_a, ms = carry[:n_mat], carry[n_mat:]
                for u in range(pairs_per_iter):
                    j = 2 * (pairs_per_iter * i + u)
                    mt_b = scores(qi, j + 1, b)
                    ms = consume(j, a, mt_a, ms)
                    mt_a = scores(qi, j + 2, a)
                    ms = consume(j + 1, b, mt_b, ms)
                return tuple(mt_a) + tuple(ms)

            carry = lax.fori_loop(0, half // pairs_per_iter, key_pairs, tuple(mt) + neg)
            mt = scores(jnp.minimum(qi + 1, n_t - 1), 0, b)
            consume(n_t - 1, a, carry[:n_mat], carry[n_mat:])
            finalize(qi)
        return tuple(mt)

    lax.fori_loop(0, half, query_pair, tuple(mt0))


DA_HEADS_PER_STEP = 2


def _flash_da_kernel(q_ref, k_ref, v_ref, lam_ref, g_ref, o_ref, qpad_ref, acc_ref, sa_ref, sb_ref,
                     *, n_t, lam_init):
    dv = 2 * DA_HEAD_DIM
    hps = DA_HEADS_PER_STEP
    bufs = (sa_ref, sb_ref)
    acc_ref[...] = jnp.zeros_like(acc_ref)
    top = lax.broadcasted_iota(jnp.int32, (dv, TILE), 0) < DA_HEAD_DIM

    def pad_queries(qi, carry):
        for hh in range(hps):
            q = q_ref[qi, hh]
            zero = jnp.zeros_like(q)
            qpad_ref[qi, 2 * hh] = jnp.where(top, q, zero)
            qpad_ref[qi, 2 * hh + 1] = jnp.where(top, zero, q)
        return carry

    lax.fori_loop(0, n_t, pad_queries, 0)
    lp = lam_ref[...]
    lam = (jnp.exp(jnp.sum(lp[0:1] * lp[1:2], keepdims=True))
           - jnp.exp(jnp.sum(lp[2:3] * lp[3:4], keepdims=True)) + lam_init)

    def scores(qi, j, buf):
        out = []
        for hh in range(hps):
            k = k_ref[j, hh]
            for i in range(2):
                s = _dot(k, qpad_ref[qi, 2 * hh + i])
                bufs[buf][2 * hh + i] = s
                out.append(jnp.max(s, axis=0, keepdims=True))
        return out

    def consume(j, buf, mts, ms):
        out = []
        for hh in range(hps):
            vt = v_ref[j, hh]
            for i in range(2):
                c = 2 * hh + i
                p, alpha, m_new = _softmax_update(bufs[buf], c, mts[c], ms[c])
                acc_ref[c] = acc_ref[c] * alpha + _dot(vt, p)
                out.append(m_new)
        return tuple(out)

    def finalize(qi):
        rows = pl.ds(pl.multiple_of(qi * TILE, TILE), TILE)
        for hh in range(hps):
            a1, a2 = acc_ref[2 * hh], acc_ref[2 * hh + 1]
            o = a1[0:dv] / a1[dv:dv + 1] - lam * (a2[0:dv] / a2[dv:dv + 1])
            ms = jnp.mean(o * o, axis=0, keepdims=True)
            o = o * lax.rsqrt(ms + NORM_EPS) * g_ref[...] * (1.0 - lam_init)
            o_ref[rows, hh * dv:(hh + 1) * dv] = o.T.astype(o_ref.dtype)
        acc_ref[...] = jnp.zeros_like(acc_ref)

    _attention_program(n_t, 2 * hps, scores, consume, finalize)


def _flash_da(qda, kda, vda, lam_p, subln_g, batch, n_t, lam_init):
    dv = 2 * DA_HEAD_DIM
    hps = DA_HEADS_PER_STEP
    m = batch * n_t * TILE
    kern = functools.partial(_flash_da_kernel, n_t=n_t, lam_init=lam_init)
    return pl.pallas_call(
        kern,
        grid=(batch, DA_HEADS // hps),
        in_specs=[pl.BlockSpec((n_t, hps, dv, TILE), lambda b, h: (b, h, 0, 0)),
                  pl.BlockSpec((n_t, hps, TILE, dv), lambda b, h: (b, h, 0, 0)),
                  pl.BlockSpec((n_t, hps, dv + ONES_ROWS, TILE), lambda b, h: (b, h, 0, 0)),
                  pl.BlockSpec((4, DA_HEAD_DIM), lambda b, h: (0, 0)),
                  pl.BlockSpec((dv, TILE), lambda b, h: (0, 0))],
        out_specs=pl.BlockSpec((n_t * TILE, hps * dv), lambda b, h: (b, h)),
        out_shape=jax.ShapeDtypeStruct((m, DA_HEADS * dv), BF16),
        scratch_shapes=[pltpu.VMEM((n_t, 2 * hps, dv, TILE), BF16),
                        pltpu.VMEM((2 * hps, dv + ONES_ROWS, TILE), F32),
                        pltpu.VMEM((2 * hps, TILE, TILE), F32), pltpu.VMEM((2 * hps, TILE, TILE), F32)],
        compiler_params=_params("parallel", "parallel"),
        name="flash_da",
    )(qda, kda, vda, lam_p, subln_g)


def _flash_gq_kernel(q_ref, k_ref, v_ref, o_ref, qpad_ref, acc_ref, sa_ref, sb_ref, *, n_t):
    first = pl.program_id(1) == 0
    group = GQ_HEADS // GQ_KV_HEADS
    dv = GQ_HEAD_DIM
    bufs = (sa_ref, sb_ref)
    acc_ref[...] = jnp.zeros_like(acc_ref)

    def pad_queries(qi, carry):
        for g in range(group):
            q = q_ref[qi, g]
            zero = jnp.zeros_like(q)
            qpad_ref[qi, g] = jnp.concatenate([jnp.where(first, q, zero), jnp.where(first, zero, q)], axis=0)
        return carry

    lax.fori_loop(0, n_t, pad_queries, 0)

    def scores(qi, j, buf):
        k = k_ref[j]
        out = []
        for g in range(group):
            s = _dot(k, qpad_ref[qi, g])
            bufs[buf][g] = s
            out.append(jnp.max(s, axis=0, keepdims=True))
        return out

    def consume(j, buf, mts, ms):
        vt = v_ref[j, 0]
        out = []
        for g in range(group):
            p, alpha, m_new = _softmax_update(bufs[buf], g, mts[g], ms[g])
            acc_ref[g] = acc_ref[g] * alpha + _dot(vt, p)
            out.append(m_new)
        return tuple(out)

    def finalize(qi):
        rows = pl.ds(pl.multiple_of(qi * TILE, TILE), TILE)
        o = jnp.concatenate([acc_ref[g, 0:dv, :] / acc_ref[g, dv:dv + 1, :] for g in range(group)], axis=0)
        o_ref[rows, :] = o.T.astype(o_ref.dtype)
        acc_ref[...] = jnp.zeros_like(acc_ref)

    _attention_program(n_t, group, scores, consume, finalize)


def _flash_gq(qgq, kgq, vgq, batch, n_t):
    group = GQ_HEADS // GQ_KV_HEADS
    m = batch * n_t * TILE
    kern = functools.partial(_flash_gq_kernel, n_t=n_t)
    return pl.pallas_call(
        kern,
        grid=(batch, GQ_KV_HEADS),
        in_specs=[pl.BlockSpec((n_t, group, GQ_HEAD_DIM, TILE), lambda b, h: (b, h, 0, 0)),
                  pl.BlockSpec((n_t, TILE, GQ_KV_COLS), lambda b, h: (b, 0, 0)),
                  pl.BlockSpec((n_t, 1, GQ_HEAD_DIM + ONES_ROWS, TILE), lambda b, h: (b, h, 0, 0))],
        out_specs=pl.BlockSpec((n_t * TILE, group * GQ_HEAD_DIM), lambda b, h: (b, h)),
        out_shape=jax.ShapeDtypeStruct((m, GQ_Q_COLS), BF16),
        scratch_shapes=[pltpu.VMEM((n_t, group, 2 * GQ_HEAD_DIM, TILE), BF16),
                        pltpu.VMEM((group, GQ_HEAD_DIM + ONES_ROWS, TILE), F32),
                        pltpu.VMEM((group, TILE, TILE), F32), pltpu.VMEM((group, TILE, TILE), F32)],
        compiler_params=_params("parallel", "parallel"),
        name="flash_gq",
    )(qgq, kgq, vgq)


def _conv3(x, prev_row, next_row, w_ref):
    n = x.shape[0]
    rid = lax.broadcasted_iota(jnp.int32, x.shape, 0)
    x_m1 = jnp.where(rid == 0, prev_row, pltpu.roll(x, 1, 0))
    x_p1 = jnp.where(rid == n - 1, next_row, pltpu.roll(x, n - 1, 0))
    return x_m1 * w_ref[0:1, :] + x * w_ref[1:2, :] + x_p1 * w_ref[2:3, :]


def _halo_rows(prev_ref, next_ref, j, n_t):
    has_prev = (j >= 2).astype(F32)
    has_next = jnp.logical_and(j >= 1, j <= n_t - 2).astype(F32)
    prev_row = prev_ref[7:8, :].astype(F32) * has_prev
    next_row = next_ref[0:1, :].astype(F32) * has_next
    return prev_row, next_row


def _halo_specs(cols, col_blk, n_rows8):
    prev = pl.BlockSpec((8, cols), lambda s: (jnp.maximum(s * (TILE // 8) - 1, 0), col_blk))
    nxt = pl.BlockSpec((8, cols), lambda s: (jnp.minimum((s + 1) * (TILE // 8), n_rows8 - 1), col_blk))
    return prev, nxt


def _dn_prep_kernel(x_ref, prev_ref, next_ref, ba_ref, cw_ref, alog_ref, dtb_ref,
                    q_ref, k_ref, v_ref, gb_ref, *, n_t):
    j = pl.program_id(0) % n_t
    prev_row, next_row = _halo_rows(prev_ref, next_ref, j, n_t)
    y = _silu(_conv3(x_ref[...].astype(F32), prev_row, next_row, cw_ref))
    for h in range(DN_HEADS):
        c = h * DN_HEAD_DIM
        q = y[:, c:c + DN_HEAD_DIM]
        k = y[:, DN_WIDTH + c:DN_WIDTH + c + DN_HEAD_DIM]
        qn = q * (lax.rsqrt(jnp.sum(q * q, axis=-1, keepdims=True) + NORM_EPS) * (DN_HEAD_DIM ** -0.5))
        q_ref[:, c:c + DN_HEAD_DIM] = qn.astype(q_ref.dtype)
        kn = k * lax.rsqrt(jnp.sum(k * k, axis=-1, keepdims=True) + NORM_EPS)
        k_ref[:, c:c + DN_HEAD_DIM] = kn.astype(k_ref.dtype)
    v_ref[...] = y[:, 2 * DN_WIDTH:].astype(v_ref.dtype)
    ba = ba_ref[...]
    beta = _sigmoid(ba)
    z = ba + dtb_ref[...]
    softplus = jnp.maximum(z, 0.0) + jnp.log(1.0 + jnp.exp(-jnp.abs(z)))
    g = -jnp.exp(alog_ref[...]) * softplus
    col = lax.broadcasted_iota(jnp.int32, ba.shape, 1)
    gb_ref[...] = jnp.where(col < 2 * DN_HEADS, beta, g)


def _dn_prep(y_nat, y_ba, conv_w, alog_tab, dtb_tab, n_t):
    m = y_nat.shape[0]
    w3 = 3 * DN_WIDTH
    prev, nxt = _halo_specs(w3, NAT_QKV_OFF // w3, m // 8)
    tok = lambda c: pl.BlockSpec((TILE, c), lambda s: (s, 0))
    return pl.pallas_call(
        functools.partial(_dn_prep_kernel, n_t=n_t),
        grid=(m // TILE,),
        in_specs=[pl.BlockSpec((TILE, w3), lambda s: (s, NAT_QKV_OFF // w3)), prev, nxt,
                  tok(BA_COLS),
                  pl.BlockSpec((3, w3), lambda s: (0, 0)),
                  pl.BlockSpec((1, BA_COLS), lambda s: (0, 0)),
                  pl.BlockSpec((1, BA_COLS), lambda s: (0, 0))],
        out_specs=(tok(DN_WIDTH), tok(DN_WIDTH), tok(DN_WIDTH), tok(BA_COLS)),
        out_shape=(jax.ShapeDtypeStruct((m, DN_WIDTH), BF16),) * 3 + (jax.ShapeDtypeStruct((m, BA_COLS), F32),),
        compiler_params=_params("parallel"),
        name="dn_prep",
    )(y_nat, y_nat, y_nat, y_ba, conv_w, alog_tab, dtb_tab)


def _dn_scan_kernel(qf_ref, kf_ref, vf_ref, gf_ref, qb_ref, kb_ref, vb_ref, gb_ref, of_ref, ob_ref,
                    s_ref, a16_ref, tinv_ref, t16_ref, w16_ref, qk16_ref, rhs16_ref, u_ref, uw16_ref,
                    kt16_ref, qd16_ref, kwb_ref, sc16_ref, vnew16_ref, inter_ref):
    t = pl.program_id(1)

    @pl.when(t == 0)
    def _():
        s_ref[...] = jnp.zeros_like(s_ref)

    n_chunk = TILE // CHUNK
    shift = CHUNK.bit_length() - 1
    dh = DN_HEAD_DIM
    ii = lax.broadcasted_iota(jnp.int32, (TILE, TILE), 0)
    jj = lax.broadcasted_iota(jnp.int32, (TILE, TILE), 1)
    same = (ii >> shift) == (jj >> shift)
    eye = (ii == jj).astype(F32)
    pair0 = (ii >> 1) == (jj >> 1)
    ins = ((qf_ref, kf_ref, vf_ref, gf_ref, of_ref), (qb_ref, kb_ref, vb_ref, gb_ref, ob_ref))
    chains = [(d, h) for d in range(2) for h in range(DN_HEADS)]
    incls = (jnp.logical_and(same, ii >= jj), jnp.logical_and(same, ii <= jj))
    stricts = (jnp.logical_and(same, ii > jj), jnp.logical_and(same, ii < jj))
    as_bf16 = lambda mask: jnp.where(mask, 1.0, 0.0).astype(BF16)
    same16 = as_bf16(same)
    n_g = 4 * DN_HEADS
    zpad = jnp.zeros((BA_COLS - n_g, TILE), F32)

    gls = []
    for d in range(2):
        incl, strict = incls[d], stricts[d]
        gb = ins[d][3][...]
        g_t = gb.T[0:n_g]
        p0 = g_t.astype(BF16)
        r1 = g_t - p0.astype(F32)
        p1 = r1.astype(BF16)
        p2 = (r1 - p1.astype(F32)).astype(BF16)
        terms = jnp.concatenate([p0, p1, p2], axis=0)

        def masked_sum(mask16):
            r = _dot(terms, mask16)
            return r[0:n_g] + r[n_g:2 * n_g] + r[2 * n_g:3 * n_g]

        gcum_t = masked_sum(as_bf16(incls[1 - d]))
        gtot_t = masked_sum(same16)
        gcum = jnp.concatenate([gcum_t, zpad], axis=0).T
        gtot = jnp.concatenate([gtot_t, zpad], axis=0).T
        for h in range(DN_HEADS):
            ch = d * DN_HEADS + h
            cg = 2 * DN_HEADS + ch
            lanes = slice(h * dh, (h + 1) * dh)
            q16 = ins[d][0][:, lanes]
            k16 = ins[d][1][:, lanes]
            q = q16.astype(F32)
            k = k16.astype(F32)
            v = ins[d][2][:, lanes].astype(F32)
            beta = gb[:, ch:ch + 1]
            gc = gcum[:, cg:cg + 1]
            gr = gcum_t[cg:cg + 1, :]
            gl = gtot[:, cg:cg + 1]
            gls.append(gl)
            decay = jnp.where(incl, jnp.exp(jnp.where(incl, gc - gr, 0.0)), 0.0)
            kb = k * beta
            a = jnp.where(strict, _dot_nt(kb.astype(BF16), k16) * decay, 0.0)
            a16_ref[ch] = a.astype(BF16)
            t0 = eye - jnp.where(pair0, a, 0.0)
            tinv_ref[ch] = t0
            t16_ref[ch] = t0.astype(BF16)
            qk16_ref[ch] = jnp.where(incl, _dot_nt(q16, k16) * decay, 0.0).astype(BF16)
            egc = jnp.exp(gc)
            rhs16_ref[ch] = jnp.concatenate([v * beta, kb * egc], axis=1).astype(BF16)
            kt16_ref[ch] = (k * jnp.exp(gl - gc)).astype(BF16)
            qd16_ref[ch] = (q * egc).astype(BF16)

    lvl = 1
    while (2 << lvl) <= CHUNK:
        sib = jnp.logical_and((ii >> (lvl + 1)) == (jj >> (lvl + 1)), (ii >> lvl) != (jj >> lvl))
        for ch in range(len(chains)):
            w16_ref[ch] = _dot(a16_ref[ch], t16_ref[ch]).astype(BF16)
        for ch in range(len(chains)):
            tn = jnp.where(sib, -_dot(t16_ref[ch], w16_ref[ch]), tinv_ref[ch])
            tinv_ref[ch] = tn
            t16_ref[ch] = tn.astype(BF16)
        lvl += 1

    for ch in range(len(chains)):
        uw = _dot(t16_ref[ch], rhs16_ref[ch])
        u_ref[ch] = uw[:, 0:dh]
        uw16_ref[ch] = uw.astype(BF16)

    for ch in range(len(chains)):
        for c in range(n_chunk):
            rows = slice(c * CHUNK, (c + 1) * CHUNK)
            kwb_ref[ch, c] = _dot_tn(kt16_ref[ch, rows, :], uw16_ref[ch, rows, :])

    states = [s_ref[ch] for ch in range(len(chains))]
    for step in range(n_chunk):
        for ch, (d, h) in enumerate(chains):
            c = n_chunk - 1 - step if d else step
            s16 = states[ch].astype(BF16)
            sc16_ref[ch, c] = s16
            egl = jnp.exp(gls[ch][c * CHUNK:c * CHUNK + 1, :])
            states[ch] = (states[ch] * egl - _dot(kwb_ref[ch, c, :, dh:].astype(BF16), s16)
                          + kwb_ref[ch, c, :, 0:dh])
    for ch in range(len(chains)):
        s_ref[ch] = states[ch]

    for ch in range(len(chains)):
        for c in range(n_chunk):
            rows = slice(c * CHUNK, (c + 1) * CHUNK)
            s16 = sc16_ref[ch, c]
            vnew16_ref[ch, rows, :] = (u_ref[ch, rows, :] - _dot(uw16_ref[ch, rows, dh:], s16)).astype(BF16)
            inter_ref[ch, rows, :] = _dot(qd16_ref[ch, rows, :], s16)
    for ch, (d, h) in enumerate(chains):
        ins[d][4][:, h * dh:(h + 1) * dh] = inter_ref[ch] + _dot(qk16_ref[ch], vnew16_ref[ch])


def _dn_scan(q, k, v, gb, batch, n_t):
    m = q.shape[0]
    fwd = lambda b, t: (b * n_t + t, 0)
    bwd = lambda b, t: (b * n_t + jnp.where(t == 0, 0, n_t - t), 0)
    specs = lambda tile: [pl.BlockSpec((TILE, DN_WIDTH), tile)] * 3 + [pl.BlockSpec((TILE, 128), tile)]
    nc = 2 * DN_HEADS
    dh = DN_HEAD_DIM
    big = lambda dt: pltpu.VMEM((nc, TILE, TILE), dt)
    half = lambda dt: pltpu.VMEM((nc, TILE, dh), dt)
    return pl.pallas_call(
        _dn_scan_kernel,
        grid=(batch, n_t),
        in_specs=specs(fwd) + specs(bwd),
        out_specs=(pl.BlockSpec((TILE, DN_WIDTH), fwd), pl.BlockSpec((TILE, DN_WIDTH), bwd)),
        out_shape=(jax.ShapeDtypeStruct((m, DN_WIDTH), F32),) * 2,
        scratch_shapes=[pltpu.VMEM((nc, dh, dh), F32),
                        big(BF16), big(F32), big(BF16), big(BF16), big(BF16), big(BF16),
                        half(F32), big(BF16), half(BF16), half(BF16),
                        pltpu.VMEM((nc, TILE // CHUNK, dh, 2 * dh), F32),
                        pltpu.VMEM((nc, TILE // CHUNK, dh, dh), BF16),
                        half(BF16), half(F32)],
        compiler_params=_params("parallel", "arbitrary"),
        name="dn_scan",
    )(q, k, v, gb, q, k, v, gb)


MERGE_CHUNK = 256


def _merge_kernel(x_ref, oa_ref, ob_ref, ocf_ref, ocb_ref, z_ref, gate_ref, dng_ref, mod_ref,
                  wa_ref, wb_ref, wc_ref, wo_ref, o_ref):
    d = x_ref.shape[1]
    oa = oa_ref[...]
    ob = ob_ref[...]
    ch = MERGE_CHUNK
    n_chunks = d // ch
    first_ab = _dot(oa, wa_ref[:, 0:ch]), _dot(ob, wb_ref[:, 0:ch])

    oc = ocf_ref[...] + ocb_ref[...]
    z = z_ref[...].astype(F32)
    parts = []
    for h in range(DN_HEADS):
        lanes = slice(h * DN_HEAD_DIM, (h + 1) * DN_HEAD_DIM)
        o = oc[:, lanes]
        ms = jnp.mean(o * o, axis=-1, keepdims=True)
        parts.append(o * lax.rsqrt(ms + NORM_EPS) * dng_ref[...] * _silu(z[:, lanes]))
    oc_n = jnp.concatenate(parts, axis=1).astype(BF16)

    def branches(c):
        cols = slice(c * ch, (c + 1) * ch)
        return _dot(oa, wa_ref[:, cols]), _dot(ob, wb_ref[:, cols]), _dot(oc_n, wc_ref[:, cols])

    nxt = first_ab + (_dot(oc_n, wc_ref[:, 0:ch]),)
    acc = None
    for c in range(n_chunks):
        pa, pb, pc = nxt
        if c + 1 < n_chunks:
            nxt = branches(c + 1)
        gate = lambda i: _sigmoid(gate_ref[:, i * d + c * ch:i * d + (c + 1) * ch].astype(F32))
        mix = gate(0) * pa + gate(1) * pb + gate(2) * pc
        part = _dot(mix.astype(BF16), wo_ref[c * ch:(c + 1) * ch, :])
        acc = part if acc is None else acc + part
    o_ref[...] = x_ref[...] + mod_ref[0] * acc


def _merge(x, oa, ob, ocf, ocb, y_nat, dn_g, mod_tab, wa, wb, wc, wo):
    m, d = x.shape
    tok = lambda c, cb=0: pl.BlockSpec((TILE, c), lambda s: (s, cb))
    full = lambda a: pl.BlockSpec(a.shape, lambda s: (0,) * a.ndim)
    return pl.pallas_call(
        _merge_kernel,
        grid=(m // TILE,),
        in_specs=[tok(d), tok(DA_COLS), tok(GQ_Q_COLS), tok(DN_WIDTH), tok(DN_WIDTH),
                  tok(DN_WIDTH, NAT_Z_OFF // DN_WIDTH), tok(3 * d, NAT_GATE_OFF),
                  full(dn_g), pl.BlockSpec((1, 1, d), lambda s: (s, 0, 0)),
                  full(wa), full(wb), full(wc), full(wo)],
        out_specs=tok(d),
        out_shape=jax.ShapeDtypeStruct((m, d), F32),
        compiler_params=_params("parallel"),
        name="merge",
    )(x, oa, ob, ocf, ocb, y_nat, y_nat, dn_g, mod_tab, wa, wb, wc, wo)


FFN_CHUNK = 256


def _ffn_kernel(x_ref, prev_ref, next_ref, g_ref, shift_ref, scale_ref, w1_ref, cw_ref, cb_ref, mod_ref, w2_ref,
                o_ref, acc_ref, *, n_t):
    j = pl.program_id(0) % n_t
    has_prev = (j >= 2).astype(F32)
    has_next = jnp.logical_and(j >= 1, j <= n_t - 2).astype(F32)

    def norm(x):
        ms = jnp.mean(x * x, axis=-1, keepdims=True)
        return (x * lax.rsqrt(ms + NORM_EPS) * g_ref[...]) * (1.0 + scale_ref[0]) + shift_ref[0]

    x = x_ref[...]
    h_ext = jnp.concatenate([norm(prev_ref[...]) * has_prev, norm(x), norm(next_ref[...]) * has_next],
                            axis=0).astype(BF16)
    n_ext = TILE + 16
    ch = FFN_CHUNK
    n_chunks = cw_ref.shape[1] // ch
    project = lambda c: _dot(h_ext, w1_ref[:, 2 * c * ch:2 * (c + 1) * ch])
    au_next = project(0)
    for c in range(n_chunks):
        au = au_next
        if c + 1 < n_chunks:
            au_next = project(c + 1)
        a_ext = au[:, 0:ch]
        cols = slice(c * ch, (c + 1) * ch)
        a = (pltpu.roll(a_ext, 1, 0)[8:8 + TILE] * cw_ref[0:1, cols] + a_ext[8:8 + TILE] * cw_ref[1:2, cols]
             + pltpu.roll(a_ext, n_ext - 1, 0)[8:8 + TILE] * cw_ref[2:3, cols] + cb_ref[:, cols])
        act = (_silu(a) * au[8:8 + TILE, ch:2 * ch]).astype(BF16)
        part = _dot(act, w2_ref[cols, :])
        if c == 0:
            acc_ref[...] = part
        else:
            acc_ref[...] += part
    o_ref[...] = x + mod_ref[0] * acc_ref[...]


def _ffn(x, g, shift_tab, scale_tab, w1r, conv_w, conv_b, mod_tab, w2, n_t):
    m, d = x.shape
    dff = w2.shape[0]
    n8 = m // 8
    row = lambda s: (s, 0, 0)
    const = lambda s: (0, 0)
    return pl.pallas_call(
        functools.partial(_ffn_kernel, n_t=n_t),
        grid=(m // TILE,),
        in_specs=[pl.BlockSpec((TILE, d), lambda s: (s, 0)),
                  pl.BlockSpec((8, d), lambda s: (jnp.maximum(s * (TILE // 8) - 1, 0), 0)),
                  pl.BlockSpec((8, d), lambda s: (jnp.minimum((s + 1) * (TILE // 8), n8 - 1), 0)),
                  pl.BlockSpec((1, d), const),
                  pl.BlockSpec((1, 1, d), row), pl.BlockSpec((1, 1, d), row),
                  pl.BlockSpec((d, 2 * dff), const),
                  pl.BlockSpec((3, dff), const), pl.BlockSpec((1, dff), const),
                  pl.BlockSpec((1, 1, d), row),
                  pl.BlockSpec((dff, d), const)],
        out_specs=pl.BlockSpec((TILE, d), lambda s: (s, 0)),
        out_shape=jax.ShapeDtypeStruct((m, d), F32),
        scratch_shapes=[pltpu.VMEM((TILE, d), F32)],
        compiler_params=_params("parallel"),
        name="ffn",
    )(x, x, x, g.reshape(1, d), shift_tab, scale_tab, w1r, conv_w, conv_b.reshape(1, dff), mod_tab, w2)


def _rope_tables(n_lat):
    t = jnp.arange(n_lat, dtype=jnp.int32)
    row = (t // GRID_W).astype(F32)
    col = (t % GRID_W).astype(F32)
    d_axis = 32
    inv_freq = ROPE_THETA ** (-jnp.arange(0, d_axis, 2, dtype=F32) / d_axis)
    ang_r = row[None, :] * inv_freq[:, None]
    ang_c = col[None, :] * inv_freq[:, None]
    cr, sr, cc, sc = jnp.cos(ang_r), jnp.sin(ang_r), jnp.cos(ang_c), jnp.sin(ang_c)
    cos_lat = jnp.concatenate([cr, cr, cc, cc], axis=0)
    sin_lat = jnp.concatenate([-sr, sr, -sc, sc], axis=0)
    cos_t = jnp.concatenate([jnp.ones((64, TILE), F32), cos_lat], axis=1)
    sin_t = jnp.concatenate([jnp.zeros((64, TILE), F32), sin_lat], axis=1)
    return cos_t, sin_t


def kernel(x, c, ctx, c_ctx, w_mod, b_mod, norm1_g, w_in, da_qn_g, da_kn_g, da_lambda, da_subln_g, gq_qn_g, gq_kn_g, dn_conv_w, dn_a_log, dn_dt_bias, dn_norm_g, w_br_a, w_br_b, w_br_c, w_o, norm2_g, ffn_w1, ffn_conv_w, ffn_conv_b, ffn_w2):
    batch, n_lat, d = x.shape
    depth = w_mod.shape[0]
    dff = ffn_w2.shape[1]
    assert ctx.shape[1] == TILE and n_lat % TILE == 0 and n_lat % GRID_W == 0 and batch + 1 <= 8
    n_t = 1 + n_lat // TILE
    assert n_t % 2 == 1, "the attention key loop handles key tiles in pairs plus one"
    s_tiles = batch * n_t
    m = s_tiles * TILE

    xs = jnp.concatenate([ctx, x], axis=1).reshape(m, d)

    cond = jnp.zeros((8, d), F32).at[:batch].set(c).at[batch].set(c_ctx)
    mod = _modulation(cond, w_mod, b_mod)
    tile_id = jnp.arange(s_tiles)
    row_of_tile = jnp.where(tile_id % n_t == 0, batch, tile_id // n_t)
    mod_tabs = mod[:, row_of_tile, :].reshape(depth, s_tiles, 6, 1, d)

    cos_t, sin_t = _rope_tables(n_lat)
    cos_m, sin_m = jnp.tile(cos_t, (1, batch)), jnp.tile(sin_t, (1, batch))
    bcast = lambda g: jnp.broadcast_to(g[:, None], (g.shape[0], TILE))

    o_qkv = ATTN_ROWS
    o_b = o_qkv + 3 * DN_WIDTH
    o_z = o_b + 4 * DN_HEADS
    o_g = o_z + DN_WIDTH
    wt_attn = jnp.swapaxes(w_in[:, :, :ATTN_ROWS], 1, 2).astype(BF16)
    w_nat = jnp.concatenate([w_in[:, :, o_g:], w_in[:, :, o_qkv:o_b], w_in[:, :, o_z:o_g]], axis=2).astype(BF16)
    w_ba = jnp.concatenate([w_in[:, :, o_b:o_z], jnp.zeros((depth, d, BA_COLS - 4 * DN_HEADS), F32)],
                           axis=2).astype(BF16)
    n_chunks = dff // FFN_CHUNK
    w1r = jnp.stack([ffn_w1[:, :, :dff].reshape(depth, d, n_chunks, FFN_CHUNK),
                     ffn_w1[:, :, dff:].reshape(depth, d, n_chunks, FFN_CHUNK)],
                    axis=3).reshape(depth, d, 2 * dff).astype(BF16)
    pad_tab = lambda a: jnp.zeros((1, BA_COLS), F32).at[0, 2 * DN_HEADS:4 * DN_HEADS].set(a.reshape(-1))

    for l in range(depth):
        lam_init = 0.8 - 0.6 * math.exp(-0.3 * l)
        tab = lambda i: mod_tabs[l, :, i]

        h1 = _norm_mod(xs, norm1_g[l], tab(0), tab(1))
        y_nat = _matmul_nn(h1, w_nat[l], NAT_COLS // 2, BF16)
        y_ba = _matmul_nn(h1, w_ba[l], BA_COLS, F32)

        gains = jnp.stack([bcast(da_qn_g[l]), bcast(da_kn_g[l]), bcast(gq_qn_g[l]), bcast(gq_kn_g[l])])
        qda, kda, vda, qgq, kgq, vgq = _inproj_attn(wt_attn[l], h1, cos_m, sin_m, gains)
        oa = _flash_da(qda, kda, vda, da_lambda[l], bcast(da_subln_g[l]), batch, n_t, lam_init)
        ob = _flash_gq(qgq, kgq, vgq, batch, n_t)

        dq, dk, dv, gb = _dn_prep(y_nat, y_ba, dn_conv_w[l], pad_tab(dn_a_log[l]), pad_tab(dn_dt_bias[l]), n_t)
        ocf, ocb = _dn_scan(dq, dk, dv, gb, batch, n_t)

        xs = _merge(xs, oa, ob, ocf, ocb, y_nat, dn_norm_g[l].reshape(1, -1), tab(2),
                    w_br_a[l].astype(BF16), w_br_b[l].astype(BF16), w_br_c[l].astype(BF16), w_o[l].astype(BF16))

        xs = _ffn(xs, norm2_g[l], tab(3), tab(4), w1r[l], ffn_conv_w[l], ffn_conv_b[l], tab(5),
                  ffn_w2[l].astype(BF16), n_t)

    return xs.reshape(batch, n_t * TILE, d)[:, TILE:, :]
```

```python
import functools
import math

import jax
import jax.numpy as jnp
from jax import lax
from jax.experimental import pallas as pl
from jax.experimental.pallas import tpu as pltpu

F32 = jnp.float32
BF16 = jnp.bfloat16

TILE = 256
CHUNK = 64
GRID_W = 64
ROPE_THETA = 10000.0
NORM_EPS = 1e-6
DA_HEADS, DA_HEAD_DIM = 4, 64
GQ_HEADS, GQ_KV_HEADS, GQ_HEAD_DIM = 8, 2, 64
DN_HEADS, DN_HEAD_DIM = 4, 128
LOG2E = 1.4426950408889634
NEG_BIG = -1e30
ONES_ROWS = 16
VMEM_LIMIT = 48 * 1024 * 1024

DA_COLS = DA_HEADS * 2 * DA_HEAD_DIM
GQ_Q_COLS = GQ_HEADS * GQ_HEAD_DIM
GQ_KV_COLS = GQ_KV_HEADS * GQ_HEAD_DIM
DN_WIDTH = DN_HEADS * DN_HEAD_DIM
ATTN_ROWS = 3 * DA_COLS + GQ_Q_COLS + 2 * GQ_KV_COLS
NAT_GATE_OFF, NAT_QKV_OFF, NAT_Z_OFF = 0, 3072, 4608
NAT_COLS = 5120
BA_COLS = 128


def _params(*sem):
    return pltpu.CompilerParams(dimension_semantics=sem, vmem_limit_bytes=VMEM_LIMIT)


def _sigmoid(x):
    return 1.0 / (1.0 + jnp.exp(-x))


def _silu(x):
    return x * _sigmoid(x)


def _dot(a, b):
    return jnp.dot(a, b, preferred_element_type=F32)


def _dot_nt(a, b):
    return lax.dot_general(a, b, (((1,), (1,)), ((), ())), preferred_element_type=F32)


def _dot_tn(a, b):
    return lax.dot_general(a, b, (((0,), (0,)), ((), ())), preferred_element_type=F32)


def _mod_kernel(c_ref, w_ref, b_ref, o_ref):
    cond = _silu(c_ref[...])
    o_ref[0] = jnp.dot(cond, w_ref[0], preferred_element_type=F32,
                       precision=lax.Precision.HIGHEST) + b_ref[0]


def _modulation(cond, w_mod, b_mod):
    depth, d, n = w_mod.shape
    tn = 1536
    return pl.pallas_call(
        _mod_kernel,
        grid=(depth, n // tn),
        in_specs=[pl.BlockSpec((8, d), lambda l, j: (0, 0)),
                  pl.BlockSpec((1, d, tn), lambda l, j: (l, 0, j)),
                  pl.BlockSpec((1, 1, tn), lambda l, j: (l, 0, j))],
        out_specs=pl.BlockSpec((1, 8, tn), lambda l, j: (l, 0, j)),
        out_shape=jax.ShapeDtypeStruct((depth, 8, n), F32),
        compiler_params=_params("parallel", "parallel"),
        name="modulation",
    )(cond, w_mod, b_mod.reshape(depth, 1, n))


def _norm_mod_kernel(x_ref, g_ref, shift_ref, scale_ref, o_ref):
    for i in range(x_ref.shape[0] // TILE):
        rows = slice(i * TILE, (i + 1) * TILE)
        x = x_ref[rows, :]
        ms = jnp.mean(x * x, axis=-1, keepdims=True)
        y = x * lax.rsqrt(ms + NORM_EPS) * g_ref[...]
        o_ref[rows, :] = (y * (1.0 + scale_ref[i]) + shift_ref[i]).astype(o_ref.dtype)


def _norm_mod(x, g, shift_tab, scale_tab):
    m, d = x.shape
    tm = _row_tile(m)
    sub = tm // TILE
    return pl.pallas_call(
        _norm_mod_kernel,
        grid=(m // tm,),
        in_specs=[pl.BlockSpec((tm, d), lambda s: (s, 0)),
                  pl.BlockSpec((1, d), lambda s: (0, 0)),
                  pl.BlockSpec((sub, 1, d), lambda s: (s, 0, 0)),
                  pl.BlockSpec((sub, 1, d), lambda s: (s, 0, 0))],
        out_specs=pl.BlockSpec((tm, d), lambda s: (s, 0)),
        out_shape=jax.ShapeDtypeStruct((m, d), BF16),
        compiler_params=_params("parallel"),
        name="norm_mod",
    )(x, g.reshape(1, d), shift_tab, scale_tab)


def _mm_nn_kernel(a_ref, b_ref, o_ref):
    o_ref[...] = _dot(a_ref[...], b_ref[...]).astype(o_ref.dtype)


def _row_tile(m):
    for t in (1024, 512, 256):
        if m % t == 0:
            return t
    raise ValueError(m)


def _matmul_nn(a, b, tn, out_dtype):
    m, k = a.shape
    n = b.shape[1]
    tm = _row_tile(m)
    return pl.pallas_call(
        _mm_nn_kernel,
        grid=(m // tm, n // tn),
        in_specs=[pl.BlockSpec((tm, k), lambda i, j: (i, 0)),
                  pl.BlockSpec((k, tn), lambda i, j: (0, j))],
        out_specs=pl.BlockSpec((tm, tn), lambda i, j: (i, j)),
        out_shape=jax.ShapeDtypeStruct((m, n), out_dtype),
        compiler_params=_params("parallel", "parallel"),
        name="matmul_nn",
    )(a, b)


def _norm_rope_t(x, gain, cos, sin, scale):
    x = x.astype(F32)
    ms = jnp.mean(x * x, axis=0, keepdims=True)
    xn = x * lax.rsqrt(ms + NORM_EPS) * gain
    sw = jnp.concatenate([xn[16:32], xn[0:16], xn[48:64], xn[32:48]], axis=0)
    out = xn * cos + sw * sin
    return out * scale if scale != 1.0 else out


_ATTN_SECTIONS = ((0, DA_COLS), (DA_COLS, DA_COLS), (2 * DA_COLS, DA_COLS), (3 * DA_COLS, GQ_Q_COLS),
                  (3 * DA_COLS + GQ_Q_COLS, 2 * GQ_KV_COLS))


def _attn_prep_section(sec, y, i, cos, sin, gain_ref, qda_ref, kda_ref, vda_ref, qgq_ref, kgq_ref, vgq_ref):
    ones = jnp.ones((ONES_ROWS, TILE), F32)
    hd = DA_HEAD_DIM
    gd = GQ_HEAD_DIM
    if sec in (0, 1):
        scale = (hd ** -0.5) * LOG2E if sec == 0 else 1.0
        for h in range(DA_HEADS):
            halves = [_norm_rope_t(y[(2 * h + half) * hd:(2 * h + half + 1) * hd], gain_ref[sec], cos, sin, scale)
                      for half in range(2)]
            both = jnp.concatenate(halves, axis=0)
            if sec == 0:
                qda_ref[i, h] = both.astype(BF16)
            else:
                kda_ref[i, h] = both.T.astype(BF16)
    elif sec == 2:
        for h in range(DA_HEADS):
            vda_ref[i, h] = jnp.concatenate([y[2 * h * hd:2 * (h + 1) * hd], ones], axis=0).astype(BF16)
    elif sec == 3:
        for h in range(GQ_HEADS):
            qgq_ref[i, h] = _norm_rope_t(y[h * gd:(h + 1) * gd], gain_ref[2], cos, sin,
                                         (gd ** -0.5) * LOG2E).astype(BF16)
    else:
        ks = [_norm_rope_t(y[h * gd:(h + 1) * gd], gain_ref[3], cos, sin, 1.0) for h in range(GQ_KV_HEADS)]
        kgq_ref[i] = jnp.concatenate(ks, axis=0).T.astype(BF16)
        for h in range(GQ_KV_HEADS):
            r = GQ_KV_COLS + h * gd
            vgq_ref[i, h] = jnp.concatenate([y[r:r + gd], ones], axis=0).astype(BF16)


def _inproj_attn_kernel(wt_ref, h_ref, cos_ref, sin_ref, gain_ref,
                        qda_ref, kda_ref, vda_ref, qgq_ref, kgq_ref, vgq_ref):
    n_sub = h_ref.shape[0] // TILE
    pieces = [(i, sec) for i in range(n_sub) for sec in range(len(_ATTN_SECTIONS))]

    def project(piece):
        i, sec = piece
        r0, rows = _ATTN_SECTIONS[sec]
        return _dot_nt(wt_ref[r0:r0 + rows, :], h_ref[i * TILE:(i + 1) * TILE, :])

    y_next = project(pieces[0])
    for n, (i, sec) in enumerate(pieces):
        y = y_next
        if n + 1 < len(pieces):
            y_next = project(pieces[n + 1])
        lanes = slice(i * TILE, (i + 1) * TILE)
        _attn_prep_section(sec, y, i, cos_ref[:, lanes], sin_ref[:, lanes], gain_ref,
                           qda_ref, kda_ref, vda_ref, qgq_ref, kgq_ref, vgq_ref)


def _inproj_attn(wt, h, cos_m, sin_m, gains):
    m, k = h.shape
    tm = _row_tile(m)
    sub = tm // TILE
    s_tiles = m // TILE
    out_shape = (
        jax.ShapeDtypeStruct((s_tiles, DA_HEADS, 2 * DA_HEAD_DIM, TILE), BF16),
        jax.ShapeDtypeStruct((s_tiles, DA_HEADS, TILE, 2 * DA_HEAD_DIM), BF16),
        jax.ShapeDtypeStruct((s_tiles, DA_HEADS, 2 * DA_HEAD_DIM + ONES_ROWS, TILE), BF16),
        jax.ShapeDtypeStruct((s_tiles, GQ_HEADS, GQ_HEAD_DIM, TILE), BF16),
        jax.ShapeDtypeStruct((s_tiles, TILE, GQ_KV_COLS), BF16),
        jax.ShapeDtypeStruct((s_tiles, GQ_KV_HEADS, GQ_HEAD_DIM + ONES_ROWS, TILE), BF16),
    )
    blk = lambda sds: pl.BlockSpec((sub,) + sds.shape[1:], lambda s: (s,) + (0,) * (len(sds.shape) - 1))
    return pl.pallas_call(
        _inproj_attn_kernel,
        grid=(m // tm,),
        in_specs=[pl.BlockSpec((ATTN_ROWS, k), lambda s: (0, 0)),
                  pl.BlockSpec((tm, k), lambda s: (s, 0)),
                  pl.BlockSpec((GQ_HEAD_DIM, tm), lambda s: (0, s)),
                  pl.BlockSpec((GQ_HEAD_DIM, tm), lambda s: (0, s)),
                  pl.BlockSpec((4, GQ_HEAD_DIM, TILE), lambda s: (0, 0, 0))],
        out_specs=tuple(blk(s) for s in out_shape),
        out_shape=out_shape,
        compiler_params=_params("parallel"),
        name="inproj_attn",
    )(wt, h, cos_m, sin_m, gains)


def _softmax_update(s_ref, idx, m_tile, m_old):
    m_new = jnp.maximum(m_old, m_tile)
    alpha = jnp.exp2(m_old - m_new)
    p = jnp.exp2(s_ref[idx] - m_new)
    return p.astype(BF16), alpha, m_new


def _attention_program(n_t, n_mat, scores, consume, finalize):
    neg = (jnp.full((1, TILE), NEG_BIG, F32),) * n_mat
    consume(0, 0, scores(0, 0, 0), neg)
    finalize(0)
    half = (n_t - 1) // 2
    pairs_per_iter = next(u for u in (4, 2, 1) if half % u == 0)
    mt0 = scores(1, 0, 0)

    def query_pair(qp, mt):
        for par in (0, 1):
            qi = 1 + 2 * qp + par
            a, b = (0, 1) if par == 0 else (1, 0)

            def key_pairs(i, carry, qi=qi, a=a, b=b):
                mt_a, ms = carry[:n_mat], carry[n_mat:]
                for u in range(pairs_per_iter):
                    j = 2 * (pairs_per_iter * i + u)
                    mt_b = scores(qi, j + 1, b)
                    ms = consume(j, a, mt_a, ms)
                    mt_a = scores(qi, j + 2, a)
                    ms = consume(j + 1, b, mt_b, ms)
                return tuple(mt_a) + tuple(ms)

            carry = lax.fori_loop(0, half // pairs_per_iter, key_pairs, tuple(mt) + neg)
            mt = scores(jnp.minimum(qi + 1, n_t - 1), 0, b)
            consume(n_t - 1, a, carry[:n_mat], carry[n_mat:])
            finalize(qi)
        return tuple(mt)

    lax.fori_loop(0, half, query_pair, tuple(mt0))


DA_HEADS_PER_STEP = 2


def _flash_da_kernel(q_ref, k_ref, v_ref, lam_ref, g_ref, o_ref, qpad_ref, acc_ref, sa_ref, sb_ref,
                     *, n_t, lam_init):
    dv = 2 * DA_HEAD_DIM
    hps = DA_HEADS_PER_STEP
    bufs = (sa_ref, sb_ref)
    acc_ref[...] = jnp.zeros_like(acc_ref)
    top = lax.broadcasted_iota(jnp.int32, (dv, TILE), 0) < DA_HEAD_DIM

    def pad_queries(qi, carry):
        for hh in range(hps):
            q = q_ref[qi, hh]
            zero = jnp.zeros_like(q)
            qpad_ref[qi, 2 * hh] = jnp.where(top, q, zero)
            qpad_ref[qi, 2 * hh + 1] = jnp.where(top, zero, q)
        return carry

    lax.fori_loop(0, n_t, pad_queries, 0)
    lp = lam_ref[...]
    lam = (jnp.exp(jnp.sum(lp[0:1] * lp[1:2], keepdims=True))
           - jnp.exp(jnp.sum(lp[2:3] * lp[3:4], keepdims=True)) + lam_init)

    def scores(qi, j, buf):
        out = []
        for hh in range(hps):
            k = k_ref[j, hh]
            for i in range(2):
                s = _dot(k, qpad_ref[qi, 2 * hh + i])
                bufs[buf][2 * hh + i] = s
                out.append(jnp.max(s, axis=0, keepdims=True))
        return out

    def consume(j, buf, mts, ms):
        out = []
        for hh in range(hps):
            vt = v_ref[j, hh]
            for i in range(2):
                c = 2 * hh + i
                p, alpha, m_new = _softmax_update(bufs[buf], c, mts[c], ms[c])
                acc_ref[c] = acc_ref[c] * alpha + _dot(vt, p)
                out.append(m_new)
        return tuple(out)

    def finalize(qi):
        rows = pl.ds(pl.multiple_of(qi * TILE, TILE), TILE)
        for hh in range(hps):
            a1, a2 = acc_ref[2 * hh], acc_ref[2 * hh + 1]
            o = a1[0:dv] / a1[dv:dv + 1] - lam * (a2[0:dv] / a2[dv:dv + 1])
            ms = jnp.mean(o * o, axis=0, keepdims=True)
            o = o * lax.rsqrt(ms + NORM_EPS) * g_ref[...] * (1.0 - lam_init)
            o_ref[rows, hh * dv:(hh + 1) * dv] = o.T.astype(o_ref.dtype)
        acc_ref[...] = jnp.zeros_like(acc_ref)

    _attention_program(n_t, 2 * hps, scores, consume, finalize)


def _flash_da(qda, kda, vda, lam_p, subln_g, batch, n_t, lam_init):
    dv = 2 * DA_HEAD_DIM
    hps = DA_HEADS_PER_STEP
    m = batch * n_t * TILE
    kern = functools.partial(_flash_da_kernel, n_t=n_t, lam_init=lam_init)
    return pl.pallas_call(
        kern,
        grid=(batch, DA_HEADS // hps),
        in_specs=[pl.BlockSpec((n_t, hps, dv, TILE), lambda b, h: (b, h, 0, 0)),
                  pl.BlockSpec((n_t, hps, TILE, dv), lambda b, h: (b, h, 0, 0)),
                  pl.BlockSpec((n_t, hps, dv + ONES_ROWS, TILE), lambda b, h: (b, h, 0, 0)),
                  pl.BlockSpec((4, DA_HEAD_DIM), lambda b, h: (0, 0)),
                  pl.BlockSpec((dv, TILE), lambda b, h: (0, 0))],
        out_specs=pl.BlockSpec((n_t * TILE, hps * dv), lambda b, h: (b, h)),
        out_shape=jax.ShapeDtypeStruct((m, DA_HEADS * dv), BF16),
        scratch_shapes=[pltpu.VMEM((n_t, 2 * hps, dv, TILE), BF16),
                        pltpu.VMEM((2 * hps, dv + ONES_ROWS, TILE), F32),
                        pltpu.VMEM((2 * hps, TILE, TILE), F32), pltpu.VMEM((2 * hps, TILE, TILE), F32)],
        compiler_params=_params("parallel", "parallel"),
        name="flash_da",
    )(qda, kda, vda, lam_p, subln_g)


def _flash_gq_kernel(q_ref, k_ref, v_ref, o_ref, qpad_ref, acc_ref, sa_ref, sb_ref, *, n_t):
    first = pl.program_id(1) == 0
    group = GQ_HEADS // GQ_KV_HEADS
    dv = GQ_HEAD_DIM
    bufs = (sa_ref, sb_ref)
    acc_ref[...] = jnp.zeros_like(acc_ref)

    def pad_queries(qi, carry):
        for g in range(group):
            q = q_ref[qi, g]
            zero = jnp.zeros_like(q)
            qpad_ref[qi, g] = jnp.concatenate([jnp.where(first, q, zero), jnp.where(first, zero, q)], axis=0)
        return carry

    lax.fori_loop(0, n_t, pad_queries, 0)

    def scores(qi, j, buf):
        k = k_ref[j]
        out = []
        for g in range(group):
            s = _dot(k, qpad_ref[qi, g])
            bufs[buf][g] = s
            out.append(jnp.max(s, axis=0, keepdims=True))
        return out

    def consume(j, buf, mts, ms):
        vt = v_ref[j, 0]
        out = []
        for g in range(group):
            p, alpha, m_new = _softmax_update(bufs[buf], g, mts[g], ms[g])
            acc_ref[g] = acc_ref[g] * alpha + _dot(vt, p)
            out.append(m_new)
        return tuple(out)

    def finalize(qi):
        rows = pl.ds(pl.multiple_of(qi * TILE, TILE), TILE)
        o = jnp.concatenate([acc_ref[g, 0:dv, :] / acc_ref[g, dv:dv + 1, :] for g in range(group)], axis=0)
        o_ref[rows, :] = o.T.astype(o_ref.dtype)
        acc_ref[...] = jnp.zeros_like(acc_ref)

    _attention_program(n_t, group, scores, consume, finalize)


def _flash_gq(qgq, kgq, vgq, batch, n_t):
    group = GQ_HEADS // GQ_KV_HEADS
    m = batch * n_t * TILE
    kern = functools.partial(_flash_gq_kernel, n_t=n_t)
    return pl.pallas_call(
        kern,
        grid=(batch, GQ_KV_HEADS),
        in_specs=[pl.BlockSpec((n_t, group, GQ_HEAD_DIM, TILE), lambda b, h: (b, h, 0, 0)),
                  pl.BlockSpec((n_t, TILE, GQ_KV_COLS), lambda b, h: (b, 0, 0)),
                  pl.BlockSpec((n_t, 1, GQ_HEAD_DIM + ONES_ROWS, TILE), lambda b, h: (b, h, 0, 0))],
        out_specs=pl.BlockSpec((n_t * TILE, group * GQ_HEAD_DIM), lambda b, h: (b, h)),
        out_shape=jax.ShapeDtypeStruct((m, GQ_Q_COLS), BF16),
        scratch_shapes=[pltpu.VMEM((n_t, group, 2 * GQ_HEAD_DIM, TILE), BF16),
                        pltpu.VMEM((group, GQ_HEAD_DIM + ONES_ROWS, TILE), F32),
                        pltpu.VMEM((group, TILE, TILE), F32), pltpu.VMEM((group, TILE, TILE), F32)],
        compiler_params=_params("parallel", "parallel"),
        name="flash_gq",
    )(qgq, kgq, vgq)


def _conv3(x, prev_row, next_row, w_ref):
    n = x.shape[0]
    rid = lax.broadcasted_iota(jnp.int32, x.shape, 0)
    x_m1 = jnp.where(rid == 0, prev_row, pltpu.roll(x, 1, 0))
    x_p1 = jnp.where(rid == n - 1, next_row, pltpu.roll(x, n - 1, 0))
    return x_m1 * w_ref[0:1, :] + x * w_ref[1:2, :] + x_p1 * w_ref[2:3, :]


def _halo_rows(prev_ref, next_ref, j, n_t):
    has_prev = (j >= 2).astype(F32)
    has_next = jnp.logical_and(j >= 1, j <= n_t - 2).astype(F32)
    prev_row = prev_ref[7:8, :].astype(F32) * has_prev
    next_row = next_ref[0:1, :].astype(F32) * has_next
    return prev_row, next_row


def _halo_specs(cols, col_blk, n_rows8):
    prev = pl.BlockSpec((8, cols), lambda s: (jnp.maximum(s * (TILE // 8) - 1, 0), col_blk))
    nxt = pl.BlockSpec((8, cols), lambda s: (jnp.minimum((s + 1) * (TILE // 8), n_rows8 - 1), col_blk))
    return prev, nxt


def _dn_prep_kernel(x_ref, prev_ref, next_ref, ba_ref, cw_ref, alog_ref, dtb_ref,
                    q_ref, k_ref, v_ref, gb_ref, *, n_t):
    j = pl.program_id(0) % n_t
    prev_row, next_row = _halo_rows(prev_ref, next_ref, j, n_t)
    y = _silu(_conv3(x_ref[...].astype(F32), prev_row, next_row, cw_ref))
    for h in range(DN_HEADS):
        c = h * DN_HEAD_DIM
        q = y[:, c:c + DN_HEAD_DIM]
        k = y[:, DN_WIDTH + c:DN_WIDTH + c + DN_HEAD_DIM]
        qn = q * (lax.rsqrt(jnp.sum(q * q, axis=-1, keepdims=True) + NORM_EPS) * (DN_HEAD_DIM ** -0.5))
        q_ref[:, c:c + DN_HEAD_DIM] = qn.astype(q_ref.dtype)
        kn = k * lax.rsqrt(jnp.sum(k * k, axis=-1, keepdims=True) + NORM_EPS)
        k_ref[:, c:c + DN_HEAD_DIM] = kn.astype(k_ref.dtype)
    v_ref[...] = y[:, 2 * DN_WIDTH:].astype(v_ref.dtype)
    ba = ba_ref[...]
    beta = _sigmoid(ba)
    z = ba + dtb_ref[...]
    softplus = jnp.maximum(z, 0.0) + jnp.log(1.0 + jnp.exp(-jnp.abs(z)))
    g = -jnp.exp(alog_ref[...]) * softplus
    col = lax.broadcasted_iota(jnp.int32, ba.shape, 1)
    gb_ref[...] = jnp.where(col < 2 * DN_HEADS, beta, g)


def _dn_prep(y_nat, y_ba, conv_w, alog_tab, dtb_tab, n_t):
    m = y_nat.shape[0]
    w3 = 3 * DN_WIDTH
    prev, nxt = _halo_specs(w3, NAT_QKV_OFF // w3, m // 8)
    tok = lambda c: pl.BlockSpec((TILE, c), lambda s: (s, 0))
    return pl.pallas_call(
        functools.partial(_dn_prep_kernel, n_t=n_t),
        grid=(m // TILE,),
        in_specs=[pl.BlockSpec((TILE, w3), lambda s: (s, NAT_QKV_OFF // w3)), prev, nxt,
                  tok(BA_COLS),
                  pl.BlockSpec((3, w3), lambda s: (0, 0)),
                  pl.BlockSpec((1, BA_COLS), lambda s: (0, 0)),
                  pl.BlockSpec((1, BA_COLS), lambda s: (0, 0))],
        out_specs=(tok(DN_WIDTH), tok(DN_WIDTH), tok(DN_WIDTH), tok(BA_COLS)),
        out_shape=(jax.ShapeDtypeStruct((m, DN_WIDTH), BF16),) * 3 + (jax.ShapeDtypeStruct((m, BA_COLS), F32),),
        compiler_params=_params("parallel"),
        name="dn_prep",
    )(y_nat, y_nat, y_nat, y_ba, conv_w, alog_tab, dtb_tab)


def _dn_scan_kernel(qf_ref, kf_ref, vf_ref, gf_ref, qb_ref, kb_ref, vb_ref, gb_ref, of_ref, ob_ref,
                    s_ref, a16_ref, tinv_ref, t16_ref, w16_ref, qk16_ref, rhs16_ref, u_ref, uw16_ref,
                    kt16_ref, qd16_ref, kwb_ref, sc16_ref, vnew16_ref, inter_ref):
    t = pl.program_id(1)

    @pl.when(t == 0)
    def _():
        s_ref[...] = jnp.zeros_like(s_ref)

    n_chunk = TILE // CHUNK
    shift = CHUNK.bit_length() - 1
    dh = DN_HEAD_DIM
    ii = lax.broadcasted_iota(jnp.int32, (TILE, TILE), 0)
    jj = lax.broadcasted_iota(jnp.int32, (TILE, TILE), 1)
    same = (ii >> shift) == (jj >> shift)
    eye = (ii == jj).astype(F32)
    pair0 = (ii >> 1) == (jj >> 1)
    ins = ((qf_ref, kf_ref, vf_ref, gf_ref, of_ref), (qb_ref, kb_ref, vb_ref, gb_ref, ob_ref))
    chains = [(d, h) for d in range(2) for h in range(DN_HEADS)]
    incls = (jnp.logical_and(same, ii >= jj), jnp.logical_and(same, ii <= jj))
    stricts = (jnp.logical_and(same, ii > jj), jnp.logical_and(same, ii < jj))
    as_bf16 = lambda mask: jnp.where(mask, 1.0, 0.0).astype(BF16)
    same16 = as_bf16(same)
    n_g = 4 * DN_HEADS
    zpad = jnp.zeros((BA_COLS - n_g, TILE), F32)

    gls = []
    for d in range(2):
        incl, strict = incls[d], stricts[d]
        gb = ins[d][3][...]
        g_t = gb.T[0:n_g]
        p0 = g_t.astype(BF16)
        r1 = g_t - p0.astype(F32)
        p1 = r1.astype(BF16)
        p2 = (r1 - p1.astype(F32)).astype(BF16)
        terms = jnp.concatenate([p0, p1, p2], axis=0)

        def masked_sum(mask16):
            r = _dot(terms, mask16)
            return r[0:n_g] + r[n_g:2 * n_g] + r[2 * n_g:3 * n_g]

        gcum_t = masked_sum(as_bf16(incls[1 - d]))
        gtot_t = masked_sum(same16)
        gcum = jnp.concatenate([gcum_t, zpad], axis=0).T
        gtot = jnp.concatenate([gtot_t, zpad], axis=0).T
        for h in range(DN_HEADS):
            ch = d * DN_HEADS + h
            cg = 2 * DN_HEADS + ch
            lanes = slice(h * dh, (h + 1) * dh)
            q16 = ins[d][0][:, lanes]
            k16 = ins[d][1][:, lanes]
            q = q16.astype(F32)
            k = k16.astype(F32)
            v = ins[d][2][:, lanes].astype(F32)
            beta = gb[:, ch:ch + 1]
            gc = gcum[:, cg:cg + 1]
            gr = gcum_t[cg:cg + 1, :]
            gl = gtot[:, cg:cg + 1]
            gls.append(gl)
            decay = jnp.where(incl, jnp.exp(jnp.where(incl, gc - gr, 0.0)), 0.0)
            kb = k * beta
            a = jnp.where(strict, _dot_nt(kb.astype(BF16), k16) * decay, 0.0)
            a16_ref[ch] = a.astype(BF16)
            t0 = eye - jnp.where(pair0, a, 0.0)
            tinv_ref[ch] = t0
            t16_ref[ch] = t0.astype(BF16)
            qk16_ref[ch] = jnp.where(incl, _dot_nt(q16, k16) * decay, 0.0).astype(BF16)
            egc = jnp.exp(gc)
            rhs16_ref[ch] = jnp.concatenate([v * beta, kb * egc], axis=1).astype(BF16)
            kt16_ref[ch] = (k * jnp.exp(gl - gc)).astype(BF16)
            qd16_ref[ch] = (q * egc).astype(BF16)

    lvl = 1
    while (2 << lvl) <= CHUNK:
        sib = jnp.logical_and((ii >> (lvl + 1)) == (jj >> (lvl + 1)), (ii >> lvl) != (jj >> lvl))
        for ch in range(len(chains)):
            w16_ref[ch] = _dot(a16_ref[ch], t16_ref[ch]).astype(BF16)
        for ch in range(len(chains)):
            tn = jnp.where(sib, -_dot(t16_ref[ch], w16_ref[ch]), tinv_ref[ch])
            tinv_ref[ch] = tn
            t16_ref[ch] = tn.astype(BF16)
        lvl += 1

    for ch in range(len(chains)):
        uw = _dot(t16_ref[ch], rhs16_ref[ch])
        u_ref[ch] = uw[:, 0:dh]
        uw16_ref[ch] = uw.astype(BF16)

    for ch in range(len(chains)):
        for c in range(n_chunk):
            rows = slice(c * CHUNK, (c + 1) * CHUNK)
            kwb_ref[ch, c] = _dot_tn(kt16_ref[ch, rows, :], uw16_ref[ch, rows, :])

    states = [s_ref[ch] for ch in range(len(chains))]
    for step in range(n_chunk):
        for ch, (d, h) in enumerate(chains):
            c = n_chunk - 1 - step if d else step
            s16 = states[ch].astype(BF16)
            sc16_ref[ch, c] = s16
            egl = jnp.exp(gls[ch][c * CHUNK:c * CHUNK + 1, :])
            states[ch] = (states[ch] * egl - _dot(kwb_ref[ch, c, :, dh:].astype(BF16), s16)
                          + kwb_ref[ch, c, :, 0:dh])
    for ch in range(len(chains)):
        s_ref[ch] = states[ch]

    for ch in range(len(chains)):
        for c in range(n_chunk):
            rows = slice(c * CHUNK, (c + 1) * CHUNK)
            s16 = sc16_ref[ch, c]
            vnew16_ref[ch, rows, :] = (u_ref[ch, rows, :] - _dot(uw16_ref[ch, rows, dh:], s16)).astype(BF16)
            inter_ref[ch, rows, :] = _dot(qd16_ref[ch, rows, :], s16)
    for ch, (d, h) in enumerate(chains):
        ins[d][4][:, h * dh:(h + 1) * dh] = inter_ref[ch] + _dot(qk16_ref[ch], vnew16_ref[ch])


def _dn_scan(q, k, v, gb, batch, n_t):
    m = q.shape[0]
    fwd = lambda b, t: (b * n_t + t, 0)
    bwd = lambda b, t: (b * n_t + jnp.where(t == 0, 0, n_t - t), 0)
    specs = lambda tile: [pl.BlockSpec((TILE, DN_WIDTH), tile)] * 3 + [pl.BlockSpec((TILE, 128), tile)]
    nc = 2 * DN_HEADS
    dh = DN_HEAD_DIM
    big = lambda dt: pltpu.VMEM((nc, TILE, TILE), dt)
    half = lambda dt: pltpu.VMEM((nc, TILE, dh), dt)
    return pl.pallas_call(
        _dn_scan_kernel,
        grid=(batch, n_t),
        in_specs=specs(fwd) + specs(bwd),
        out_specs=(pl.BlockSpec((TILE, DN_WIDTH), fwd), pl.BlockSpec((TILE, DN_WIDTH), bwd)),
        out_shape=(jax.ShapeDtypeStruct((m, DN_WIDTH), F32),) * 2,
        scratch_shapes=[pltpu.VMEM((nc, dh, dh), F32),
                        big(BF16), big(F32), big(BF16), big(BF16), big(BF16), big(BF16),
                        half(F32), big(BF16), half(BF16), half(BF16),
                        pltpu.VMEM((nc, TILE // CHUNK, dh, 2 * dh), F32),
                        pltpu.VMEM((nc, TILE // CHUNK, dh, dh), BF16),
                        half(BF16), half(F32)],
        compiler_params=_params("parallel", "arbitrary"),
        name="dn_scan",
    )(q, k, v, gb, q, k, v, gb)


MERGE_CHUNK = 256


def _merge_kernel(x_ref, oa_ref, ob_ref, ocf_ref, ocb_ref, z_ref, gate_ref, dng_ref, mod_ref,
                  wa_ref, wb_ref, wc_ref, wo_ref, o_ref):
    d = x_ref.shape[1]
    oa = oa_ref[...]
    ob = ob_ref[...]
    ch = MERGE_CHUNK
    n_chunks = d // ch
    first_ab = _dot(oa, wa_ref[:, 0:ch]), _dot(ob, wb_ref[:, 0:ch])

    oc = ocf_ref[...] + ocb_ref[...]
    z = z_ref[...].astype(F32)
    parts = []
    for h in range(DN_HEADS):
        lanes = slice(h * DN_HEAD_DIM, (h + 1) * DN_HEAD_DIM)
        o = oc[:, lanes]
        ms = jnp.mean(o * o, axis=-1, keepdims=True)
        parts.append(o * lax.rsqrt(ms + NORM_EPS) * dng_ref[...] * _silu(z[:, lanes]))
    oc_n = jnp.concatenate(parts, axis=1).astype(BF16)

    def branches(c):
        cols = slice(c * ch, (c + 1) * ch)
        return _dot(oa, wa_ref[:, cols]), _dot(ob, wb_ref[:, cols]), _dot(oc_n, wc_ref[:, cols])

    nxt = first_ab + (_dot(oc_n, wc_ref[:, 0:ch]),)
    acc = None
    for c in range(n_chunks):
        pa, pb, pc = nxt
        if c + 1 < n_chunks:
            nxt = branches(c + 1)
        gate = lambda i: _sigmoid(gate_ref[:, i * d + c * ch:i * d + (c + 1) * ch].astype(F32))
        mix = gate(0) * pa + gate(1) * pb + gate(2) * pc
        part = _dot(mix.astype(BF16), wo_ref[c * ch:(c + 1) * ch, :])
        acc = part if acc is None else acc + part
    o_ref[...] = x_ref[...] + mod_ref[0] * acc


def _merge(x, oa, ob, ocf, ocb, y_nat, dn_g, mod_tab, wa, wb, wc, wo):
    m, d = x.shape
    tok = lambda c, cb=0: pl.BlockSpec((TILE, c), lambda s: (s, cb))
    full = lambda a: pl.BlockSpec(a.shape, lambda s: (0,) * a.ndim)
    return pl.pallas_call(
        _merge_kernel,
        grid=(m // TILE,),
        in_specs=[tok(d), tok(DA_COLS), tok(GQ_Q_COLS), tok(DN_WIDTH), tok(DN_WIDTH),
                  tok(DN_WIDTH, NAT_Z_OFF // DN_WIDTH), tok(3 * d, NAT_GATE_OFF),
                  full(dn_g), pl.BlockSpec((1, 1, d), lambda s: (s, 0, 0)),
                  full(wa), full(wb), full(wc), full(wo)],
        out_specs=tok(d),
        out_shape=jax.ShapeDtypeStruct((m, d), F32),
        compiler_params=_params("parallel"),
        name="merge",
    )(x, oa, ob, ocf, ocb, y_nat, y_nat, dn_g, mod_tab, wa, wb, wc, wo)


FFN_CHUNK = 256


def _ffn_kernel(x_ref, prev_ref, next_ref, g_ref, shift_ref, scale_ref, w1_ref, cw_ref, cb_ref, mod_ref, w2_ref,
                o_ref, acc_ref, *, n_t):
    j = pl.program_id(0) % n_t
    has_prev = (j >= 2).astype(F32)
    has_next = jnp.logical_and(j >= 1, j <= n_t - 2).astype(F32)

    def norm(x):
        ms = jnp.mean(x * x, axis=-1, keepdims=True)
        return (x * lax.rsqrt(ms + NORM_EPS) * g_ref[...]) * (1.0 + scale_ref[0]) + shift_ref[0]

    x = x_ref[...]
    h_ext = jnp.concatenate([norm(prev_ref[...]) * has_prev, norm(x), norm(next_ref[...]) * has_next],
                            axis=0).astype(BF16)
    n_ext = TILE + 16
    ch = FFN_CHUNK
    dff = cw_ref.shape[1]
    n_chunks = dff // ch

    def project(c):
        return (_dot(h_ext, w1_ref[:, c * ch:(c + 1) * ch]),
                _dot(h_ext, w1_ref[:, dff + c * ch:dff + (c + 1) * ch]))

    au_next = project(0)
    for c in range(n_chunks):
        a_ext, u_ext = au_next
        if c + 1 < n_chunks:
            au_next = project(c + 1)
        cols = slice(c * ch, (c + 1) * ch)
        a = (pltpu.roll(a_ext, 1, 0)[8:8 + TILE] * cw_ref[0:1, cols] + a_ext[8:8 + TILE] * cw_ref[1:2, cols]
             + pltpu.roll(a_ext, n_ext - 1, 0)[8:8 + TILE] * cw_ref[2:3, cols] + cb_ref[:, cols])
        act = (_silu(a) * u_ext[8:8 + TILE]).astype(BF16)
        part = _dot(act, w2_ref[cols, :])
        if c == 0:
            acc_ref[...] = part
        else:
            acc_ref[...] += part
    o_ref[...] = x + mod_ref[0] * acc_ref[...]


def _ffn(x, g, shift_tab, scale_tab, w1, conv_w, conv_b, mod_tab, w2, n_t):
    m, d = x.shape
    dff = w2.shape[0]
    n8 = m // 8
    row = lambda s: (s, 0, 0)
    const = lambda s: (0, 0)
    return pl.pallas_call(
        functools.partial(_ffn_kernel, n_t=n_t),
        grid=(m // TILE,),
        in_specs=[pl.BlockSpec((TILE, d), lambda s: (s, 0)),
                  pl.BlockSpec((8, d), lambda s: (jnp.maximum(s * (TILE // 8) - 1, 0), 0)),
                  pl.BlockSpec((8, d), lambda s: (jnp.minimum((s + 1) * (TILE // 8), n8 - 1), 0)),
                  pl.BlockSpec((1, d), const),
                  pl.BlockSpec((1, 1, d), row), pl.BlockSpec((1, 1, d), row),
                  pl.BlockSpec((d, 2 * dff), const),
                  pl.BlockSpec((3, dff), const), pl.BlockSpec((1, dff), const),
                  pl.BlockSpec((1, 1, d), row),
                  pl.BlockSpec((dff, d), const)],
        out_specs=pl.BlockSpec((TILE, d), lambda s: (s, 0)),
        out_shape=jax.ShapeDtypeStruct((m, d), F32),
        scratch_shapes=[pltpu.VMEM((TILE, d), F32)],
        compiler_params=_params("parallel"),
        name="ffn",
    )(x, x, x, g.reshape(1, d), shift_tab, scale_tab, w1, conv_w, conv_b.reshape(1, dff), mod_tab, w2)


def _rope_tables(n_lat):
    t = jnp.arange(n_lat, dtype=jnp.int32)
    row = (t // GRID_W).astype(F32)
    col = (t % GRID_W).astype(F32)
    d_axis = 32
    inv_freq = ROPE_THETA ** (-jnp.arange(0, d_axis, 2, dtype=F32) / d_axis)
    ang_r = row[None, :] * inv_freq[:, None]
    ang_c = col[None, :] * inv_freq[:, None]
    cr, sr, cc, sc = jnp.cos(ang_r), jnp.sin(ang_r), jnp.cos(ang_c), jnp.sin(ang_c)
    cos_lat = jnp.concatenate([cr, cr, cc, cc], axis=0)
    sin_lat = jnp.concatenate([-sr, sr, -sc, sc], axis=0)
    cos_t = jnp.concatenate([jnp.ones((64, TILE), F32), cos_lat], axis=1)
    sin_t = jnp.concatenate([jnp.zeros((64, TILE), F32), sin_lat], axis=1)
    return cos_t, sin_t


def kernel(x, c, ctx, c_ctx, w_mod, b_mod, norm1_g, w_in, da_qn_g, da_kn_g, da_lambda, da_subln_g, gq_qn_g, gq_kn_g, dn_conv_w, dn_a_log, dn_dt_bias, dn_norm_g, w_br_a, w_br_b, w_br_c, w_o, norm2_g, ffn_w1, ffn_conv_w, ffn_conv_b, ffn_w2):
    batch, n_lat, d = x.shape
    depth = w_mod.shape[0]
    dff = ffn_w2.shape[1]
    assert ctx.shape[1] == TILE and n_lat % TILE == 0 and n_lat % GRID_W == 0 and batch + 1 <= 8
    n_t = 1 + n_lat // TILE
    assert n_t % 2 == 1, "the attention key loop handles key tiles in pairs plus one"
    s_tiles = batch * n_t
    m = s_tiles * TILE

    xs = jnp.concatenate([ctx, x], axis=1).reshape(m, d)

    cond = jnp.zeros((8, d), F32).at[:batch].set(c).at[batch].set(c_ctx)
    mod = _modulation(cond, w_mod, b_mod)
    tile_id = jnp.arange(s_tiles)
    row_of_tile = jnp.where(tile_id % n_t == 0, batch, tile_id // n_t)
    mod_tabs = mod[:, row_of_tile, :].reshape(depth, s_tiles, 6, 1, d)

    cos_t, sin_t = _rope_tables(n_lat)
    cos_m, sin_m = jnp.tile(cos_t, (1, batch)), jnp.tile(sin_t, (1, batch))
    bcast = lambda g: jnp.broadcast_to(g[:, None], (g.shape[0], TILE))

    o_qkv = ATTN_ROWS
    o_b = o_qkv + 3 * DN_WIDTH
    o_z = o_b + 4 * DN_HEADS
    o_g = o_z + DN_WIDTH
    wt_attn = jnp.swapaxes(w_in[:, :, :ATTN_ROWS], 1, 2).astype(BF16)
    w_nat = jnp.concatenate([w_in[:, :, o_g:], w_in[:, :, o_qkv:o_b], w_in[:, :, o_z:o_g]], axis=2).astype(BF16)
    w_ba = jnp.concatenate([w_in[:, :, o_b:o_z], jnp.zeros((depth, d, BA_COLS - 4 * DN_HEADS), F32)],
                           axis=2).astype(BF16)
    pad_tab = lambda a: jnp.zeros((1, BA_COLS), F32).at[0, 2 * DN_HEADS:4 * DN_HEADS].set(a.reshape(-1))

    for l in range(depth):
        lam_init = 0.8 - 0.6 * math.exp(-0.3 * l)
        tab = lambda i: mod_tabs[l, :, i]

        h1 = _norm_mod(xs, norm1_g[l], tab(0), tab(1))
        y_nat = _matmul_nn(h1, w_nat[l], NAT_COLS // 2, BF16)
        y_ba = _matmul_nn(h1, w_ba[l], BA_COLS, F32)

        gains = jnp.stack([bcast(da_qn_g[l]), bcast(da_kn_g[l]), bcast(gq_qn_g[l]), bcast(gq_kn_g[l])])
        qda, kda, vda, qgq, kgq, vgq = _inproj_attn(wt_attn[l], h1, cos_m, sin_m, gains)
        oa = _flash_da(qda, kda, vda, da_lambda[l], bcast(da_subln_g[l]), batch, n_t, lam_init)
        ob = _flash_gq(qgq, kgq, vgq, batch, n_t)

        dq, dk, dv, gb = _dn_prep(y_nat, y_ba, dn_conv_w[l], pad_tab(dn_a_log[l]), pad_tab(dn_dt_bias[l]), n_t)
        ocf, ocb = _dn_scan(dq, dk, dv, gb, batch, n_t)

        xs = _merge(xs, oa, ob, ocf, ocb, y_nat, dn_norm_g[l].reshape(1, -1), tab(2),
                    w_br_a[l].astype(BF16), w_br_b[l].astype(BF16), w_br_c[l].astype(BF16), w_o[l].astype(BF16))

        xs = _ffn(xs, norm2_g[l], tab(3), tab(4), ffn_w1[l].astype(BF16), ffn_conv_w[l], ffn_conv_b[l], tab(5),
                  ffn_w2[l].astype(BF16), n_t)

    return xs.reshape(batch, n_t * TILE, d)[:, TILE:, :]
```

```python
import functools
import math

import jax
import jax.numpy as jnp
from jax import lax
from jax.experimental import pallas as pl
from jax.experimental.pallas import tpu as pltpu

F32 = jnp.float32
BF16 = jnp.bfloat16

TILE = 256
CHUNK = 64
GRID_W = 64
ROPE_THETA = 10000.0
NORM_EPS = 1e-6
DA_HEADS, DA_HEAD_DIM = 4, 64
GQ_HEADS, GQ_KV_HEADS, GQ_HEAD_DIM = 8, 2, 64
DN_HEADS, DN_HEAD_DIM = 4, 128
LOG2E = 1.4426950408889634
NEG_BIG = -1e30
ONES_ROWS = 16
VMEM_LIMIT = 48 * 1024 * 1024

DA_COLS = DA_HEADS * 2 * DA_HEAD_DIM
GQ_Q_COLS = GQ_HEADS * GQ_HEAD_DIM
GQ_KV_COLS = GQ_KV_HEADS * GQ_HEAD_DIM
DN_WIDTH = DN_HEADS * DN_HEAD_DIM
ATTN_ROWS = 3 * DA_COLS + GQ_Q_COLS + 2 * GQ_KV_COLS
NAT_GATE_OFF, NAT_Z_OFF = 0, 3072
NAT_COLS = 3584
BA_COLS = 128


def _params(*sem):
    return pltpu.CompilerParams(dimension_semantics=sem, vmem_limit_bytes=VMEM_LIMIT)


def _sigmoid(x):
    return 1.0 / (1.0 + jnp.exp(-x))


def _silu(x):
    return x * _sigmoid(x)


def _dot(a, b):
    return jnp.dot(a, b, preferred_element_type=F32)


def _dot_nt(a, b):
    return lax.dot_general(a, b, (((1,), (1,)), ((), ())), preferred_element_type=F32)


def _dot_tn(a, b):
    return lax.dot_general(a, b, (((0,), (0,)), ((), ())), preferred_element_type=F32)


def _mod_kernel(c_ref, w_ref, b_ref, o_ref):
    cond = _silu(c_ref[...])
    o_ref[0] = jnp.dot(cond, w_ref[0], preferred_element_type=F32,
                       precision=lax.Precision.HIGHEST) + b_ref[0]


def _modulation(cond, w_mod, b_mod):
    depth, d, n = w_mod.shape
    tn = 1536
    return pl.pallas_call(
        _mod_kernel,
        grid=(depth, n // tn),
        in_specs=[pl.BlockSpec((8, d), lambda l, j: (0, 0)),
                  pl.BlockSpec((1, d, tn), lambda l, j: (l, 0, j)),
                  pl.BlockSpec((1, 1, tn), lambda l, j: (l, 0, j))],
        out_specs=pl.BlockSpec((1, 8, tn), lambda l, j: (l, 0, j)),
        out_shape=jax.ShapeDtypeStruct((depth, 8, n), F32),
        compiler_params=_params("parallel", "parallel"),
        name="modulation",
    )(cond, w_mod, b_mod.reshape(depth, 1, n))


def _norm_mod_kernel(x_ref, g_ref, shift_ref, scale_ref, o_ref):
    for i in range(x_ref.shape[0] // TILE):
        rows = slice(i * TILE, (i + 1) * TILE)
        x = x_ref[rows, :]
        ms = jnp.mean(x * x, axis=-1, keepdims=True)
        y = x * lax.rsqrt(ms + NORM_EPS) * g_ref[...]
        o_ref[rows, :] = (y * (1.0 + scale_ref[i]) + shift_ref[i]).astype(o_ref.dtype)


def _norm_mod(x, g, shift_tab, scale_tab):
    m, d = x.shape
    tm = _row_tile(m)
    sub = tm // TILE
    return pl.pallas_call(
        _norm_mod_kernel,
        grid=(m // tm,),
        in_specs=[pl.BlockSpec((tm, d), lambda s: (s, 0)),
                  pl.BlockSpec((1, d), lambda s: (0, 0)),
                  pl.BlockSpec((sub, 1, d), lambda s: (s, 0, 0)),
                  pl.BlockSpec((sub, 1, d), lambda s: (s, 0, 0))],
        out_specs=pl.BlockSpec((tm, d), lambda s: (s, 0)),
        out_shape=jax.ShapeDtypeStruct((m, d), BF16),
        compiler_params=_params("parallel"),
        name="norm_mod",
    )(x, g.reshape(1, d), shift_tab, scale_tab)


def _mm_nn_kernel(a_ref, b_ref, o_ref):
    o_ref[...] = _dot(a_ref[...], b_ref[...]).astype(o_ref.dtype)


def _row_tile(m):
    for t in (1024, 512, 256):
        if m % t == 0:
            return t
    raise ValueError(m)


def _matmul_nn(a, b, tn, out_dtype):
    m, k = a.shape
    n = b.shape[1]
    tm = _row_tile(m)
    return pl.pallas_call(
        _mm_nn_kernel,
        grid=(m // tm, n // tn),
        in_specs=[pl.BlockSpec((tm, k), lambda i, j: (i, 0)),
                  pl.BlockSpec((k, tn), lambda i, j: (0, j))],
        out_specs=pl.BlockSpec((tm, tn), lambda i, j: (i, j)),
        out_shape=jax.ShapeDtypeStruct((m, n), out_dtype),
        compiler_params=_params("parallel", "parallel"),
        name="matmul_nn",
    )(a, b)


def _norm_rope_t(x, gain, cos, sin, scale):
    x = x.astype(F32)
    ms = jnp.mean(x * x, axis=0, keepdims=True)
    xn = x * lax.rsqrt(ms + NORM_EPS) * gain
    sw = jnp.concatenate([xn[16:32], xn[0:16], xn[48:64], xn[32:48]], axis=0)
    out = xn * cos + sw * sin
    return out * scale if scale != 1.0 else out


_ATTN_SECTIONS = ((0, DA_COLS), (DA_COLS, DA_COLS), (2 * DA_COLS, DA_COLS), (3 * DA_COLS, GQ_Q_COLS),
                  (3 * DA_COLS + GQ_Q_COLS, 2 * GQ_KV_COLS))


def _attn_prep_section(sec, y, i, cos, sin, gain_ref, qda_ref, kda_ref, vda_ref, qgq_ref, kgq_ref, vgq_ref):
    ones = jnp.ones((ONES_ROWS, TILE), F32)
    hd = DA_HEAD_DIM
    gd = GQ_HEAD_DIM
    if sec in (0, 1):
        scale = (hd ** -0.5) * LOG2E if sec == 0 else 1.0
        for h in range(DA_HEADS):
            halves = [_norm_rope_t(y[(2 * h + half) * hd:(2 * h + half + 1) * hd], gain_ref[sec], cos, sin, scale)
                      for half in range(2)]
            both = jnp.concatenate(halves, axis=0)
            if sec == 0:
                qda_ref[i, h] = both.astype(BF16)
            else:
                kda_ref[i, h] = both.T.astype(BF16)
    elif sec == 2:
        for h in range(DA_HEADS):
            vda_ref[i, h] = jnp.concatenate([y[2 * h * hd:2 * (h + 1) * hd], ones], axis=0).astype(BF16)
    elif sec == 3:
        for h in range(GQ_HEADS):
            qgq_ref[i, h] = _norm_rope_t(y[h * gd:(h + 1) * gd], gain_ref[2], cos, sin,
                                         (gd ** -0.5) * LOG2E).astype(BF16)
    else:
        ks = [_norm_rope_t(y[h * gd:(h + 1) * gd], gain_ref[3], cos, sin, 1.0) for h in range(GQ_KV_HEADS)]
        kgq_ref[i] = jnp.concatenate(ks, axis=0).T.astype(BF16)
        for h in range(GQ_KV_HEADS):
            r = GQ_KV_COLS + h * gd
            vgq_ref[i, h] = jnp.concatenate([y[r:r + gd], ones], axis=0).astype(BF16)


def _inproj_attn_kernel(wt_ref, h_ref, cos_ref, sin_ref, gain_ref,
                        qda_ref, kda_ref, vda_ref, qgq_ref, kgq_ref, vgq_ref):
    n_sub = h_ref.shape[0] // TILE
    pieces = [(i, sec) for i in range(n_sub) for sec in range(len(_ATTN_SECTIONS))]

    def project(piece):
        i, sec = piece
        r0, rows = _ATTN_SECTIONS[sec]
        return _dot_nt(wt_ref[r0:r0 + rows, :], h_ref[i * TILE:(i + 1) * TILE, :])

    y_next = project(pieces[0])
    for n, (i, sec) in enumerate(pieces):
        y = y_next
        if n + 1 < len(pieces):
            y_next = project(pieces[n + 1])
        lanes = slice(i * TILE, (i + 1) * TILE)
        _attn_prep_section(sec, y, i, cos_ref[:, lanes], sin_ref[:, lanes], gain_ref,
                           qda_ref, kda_ref, vda_ref, qgq_ref, kgq_ref, vgq_ref)


def _inproj_attn(wt, h, cos_m, sin_m, gains):
    m, k = h.shape
    tm = _row_tile(m)
    sub = tm // TILE
    s_tiles = m // TILE
    out_shape = (
        jax.ShapeDtypeStruct((s_tiles, DA_HEADS, 2 * DA_HEAD_DIM, TILE), BF16),
        jax.ShapeDtypeStruct((s_tiles, DA_HEADS, TILE, 2 * DA_HEAD_DIM), BF16),
        jax.ShapeDtypeStruct((s_tiles, DA_HEADS, 2 * DA_HEAD_DIM + ONES_ROWS, TILE), BF16),
        jax.ShapeDtypeStruct((s_tiles, GQ_HEADS, GQ_HEAD_DIM, TILE), BF16),
        jax.ShapeDtypeStruct((s_tiles, TILE, GQ_KV_COLS), BF16),
        jax.ShapeDtypeStruct((s_tiles, GQ_KV_HEADS, GQ_HEAD_DIM + ONES_ROWS, TILE), BF16),
    )
    blk = lambda sds: pl.BlockSpec((sub,) + sds.shape[1:], lambda s: (s,) + (0,) * (len(sds.shape) - 1))
    return pl.pallas_call(
        _inproj_attn_kernel,
        grid=(m // tm,),
        in_specs=[pl.BlockSpec((ATTN_ROWS, k), lambda s: (0, 0)),
                  pl.BlockSpec((tm, k), lambda s: (s, 0)),
                  pl.BlockSpec((GQ_HEAD_DIM, tm), lambda s: (0, s)),
                  pl.BlockSpec((GQ_HEAD_DIM, tm), lambda s: (0, s)),
                  pl.BlockSpec((4, GQ_HEAD_DIM, TILE), lambda s: (0, 0, 0))],
        out_specs=tuple(blk(s) for s in out_shape),
        out_shape=out_shape,
        compiler_params=_params("parallel"),
        name="inproj_attn",
    )(wt, h, cos_m, sin_m, gains)


def _softmax_update(s_ref, idx, m_tile, m_old):
    m_new = jnp.maximum(m_old, m_tile)
    alpha = jnp.exp2(m_old - m_new)
    p = jnp.exp2(s_ref[idx] - m_new)
    return p.astype(BF16), alpha, m_new


def _attention_program(n_t, n_mat, scores, consume, finalize):
    neg = (jnp.full((1, TILE), NEG_BIG, F32),) * n_mat
    consume(0, 0, scores(0, 0, 0), neg)
    finalize(0)
    half = (n_t - 1) // 2
    pairs_per_iter = next(u for u in (4, 2, 1) if half % u == 0)
    mt0 = scores(1, 0, 0)

    def query_pair(qp, mt):
        for par in (0, 1):
            qi = 1 + 2 * qp + par
            a, b = (0, 1) if par == 0 else (1, 0)

            def key_pairs(i, carry, qi=qi, a=a, b=b):
                mt_a, ms = carry[:n_mat], carry[n_mat:]
                for u in range(pairs_per_iter):
                    j = 2 * (pairs_per_iter * i + u)
                    mt_b = scores(qi, j + 1, b)
                    ms = consume(j, a, mt_a, ms)
                    mt_a = scores(qi, j + 2, a)
                    ms = consume(j + 1, b, mt_b, ms)
                return tuple(mt_a) + tuple(ms)

            carry = lax.fori_loop(0, half // pairs_per_iter, key_pairs, tuple(mt) + neg)
            mt = scores(jnp.minimum(qi + 1, n_t - 1), 0, b)
            consume(n_t - 1, a, carry[:n_mat], carry[n_mat:])
            finalize(qi)
        return tuple(mt)

    lax.fori_loop(0, half, query_pair, tuple(mt0))


DA_HEADS_PER_STEP = 2


def _flash_da_kernel(q_ref, k_ref, v_ref, lam_ref, g_ref, o_ref, qpad_ref, acc_ref, sa_ref, sb_ref,
                     *, n_t, lam_init):
    dv = 2 * DA_HEAD_DIM
    hps = DA_HEADS_PER_STEP
    bufs = (sa_ref, sb_ref)
    acc_ref[...] = jnp.zeros_like(acc_ref)
    top = lax.broadcasted_iota(jnp.int32, (dv, TILE), 0) < DA_HEAD_DIM

    def pad_queries(qi, carry):
        for hh in range(hps):
            q = q_ref[qi, hh]
            zero = jnp.zeros_like(q)
            qpad_ref[qi, 2 * hh] = jnp.where(top, q, zero)
            qpad_ref[qi, 2 * hh + 1] = jnp.where(top, zero, q)
        return carry

    lax.fori_loop(0, n_t, pad_queries, 0)
    lp = lam_ref[...]
    lam = (jnp.exp(jnp.sum(lp[0:1] * lp[1:2], keepdims=True))
           - jnp.exp(jnp.sum(lp[2:3] * lp[3:4], keepdims=True)) + lam_init)

    def scores(qi, j, buf):
        out = []
        for hh in range(hps):
            k = k_ref[j, hh]
            for i in range(2):
                s = _dot(k, qpad_ref[qi, 2 * hh + i])
                bufs[buf][2 * hh + i] = s
                out.append(jnp.max(s, axis=0, keepdims=True))
        return out

    def consume(j, buf, mts, ms):
        out = []
        for hh in range(hps):
            vt = v_ref[j, hh]
            for i in range(2):
                c = 2 * hh + i
                p, alpha, m_new = _softmax_update(bufs[buf], c, mts[c], ms[c])
                acc_ref[c] = acc_ref[c] * alpha + _dot(vt, p)
                out.append(m_new)
        return tuple(out)

    def finalize(qi):
        rows = pl.ds(pl.multiple_of(qi * TILE, TILE), TILE)
        for hh in range(hps):
            a1, a2 = acc_ref[2 * hh], acc_ref[2 * hh + 1]
            o = a1[0:dv] / a1[dv:dv + 1] - lam * (a2[0:dv] / a2[dv:dv + 1])
            ms = jnp.mean(o * o, axis=0, keepdims=True)
            o = o * lax.rsqrt(ms + NORM_EPS) * g_ref[...] * (1.0 - lam_init)
            o_ref[rows, hh * dv:(hh + 1) * dv] = o.T.astype(o_ref.dtype)
        acc_ref[...] = jnp.zeros_like(acc_ref)

    _attention_program(n_t, 2 * hps, scores, consume, finalize)


def _flash_da(qda, kda, vda, lam_p, subln_g, batch, n_t, lam_init):
    dv = 2 * DA_HEAD_DIM
    hps = DA_HEADS_PER_STEP
    m = batch * n_t * TILE
    kern = functools.partial(_flash_da_kernel, n_t=n_t, lam_init=lam_init)
    return pl.pallas_call(
        kern,
        grid=(batch, DA_HEADS // hps),
        in_specs=[pl.BlockSpec((n_t, hps, dv, TILE), lambda b, h: (b, h, 0, 0)),
                  pl.BlockSpec((n_t, hps, TILE, dv), lambda b, h: (b, h, 0, 0)),
                  pl.BlockSpec((n_t, hps, dv + ONES_ROWS, TILE), lambda b, h: (b, h, 0, 0)),
                  pl.BlockSpec((4, DA_HEAD_DIM), lambda b, h: (0, 0)),
                  pl.BlockSpec((dv, TILE), lambda b, h: (0, 0))],
        out_specs=pl.BlockSpec((n_t * TILE, hps * dv), lambda b, h: (b, h)),
        out_shape=jax.ShapeDtypeStruct((m, DA_HEADS * dv), BF16),
        scratch_shapes=[pltpu.VMEM((n_t, 2 * hps, dv, TILE), BF16),
                        pltpu.VMEM((2 * hps, dv + ONES_ROWS, TILE), F32),
                        pltpu.VMEM((2 * hps, TILE, TILE), F32), pltpu.VMEM((2 * hps, TILE, TILE), F32)],
        compiler_params=_params("parallel", "parallel"),
        name="flash_da",
    )(qda, kda, vda, lam_p, subln_g)


def _flash_gq_kernel(q_ref, k_ref, v_ref, o_ref, qpad_ref, acc_ref, sa_ref, sb_ref, *, n_t):
    first = pl.program_id(1) == 0
    group = GQ_HEADS // GQ_KV_HEADS
    dv = GQ_HEAD_DIM
    bufs = (sa_ref, sb_ref)
    acc_ref[...] = jnp.zeros_like(acc_ref)

    def pad_queries(qi, carry):
        for g in range(group):
            q = q_ref[qi, g]
            zero = jnp.zeros_like(q)
            qpad_ref[qi, g] = jnp.concatenate([jnp.where(first, q, zero), jnp.where(first, zero, q)], axis=0)
        return carry

    lax.fori_loop(0, n_t, pad_queries, 0)

    def scores(qi, j, buf):
        k = k_ref[j]
        out = []
        for g in range(group):
            s = _dot(k, qpad_ref[qi, g])
            bufs[buf][g] = s
            out.append(jnp.max(s, axis=0, keepdims=True))
        return out

    def consume(j, buf, mts, ms):
        vt = v_ref[j, 0]
        out = []
        for g in range(group):
            p, alpha, m_new = _softmax_update(bufs[buf], g, mts[g], ms[g])
            acc_ref[g] = acc_ref[g] * alpha + _dot(vt, p)
            out.append(m_new)
        return tuple(out)

    def finalize(qi):
        rows = pl.ds(pl.multiple_of(qi * TILE, TILE), TILE)
        o = jnp.concatenate([acc_ref[g, 0:dv, :] / acc_ref[g, dv:dv + 1, :] for g in range(group)], axis=0)
        o_ref[rows, :] = o.T.astype(o_ref.dtype)
        acc_ref[...] = jnp.zeros_like(acc_ref)

    _attention_program(n_t, group, scores, consume, finalize)


def _flash_gq(qgq, kgq, vgq, batch, n_t):
    group = GQ_HEADS // GQ_KV_HEADS
    m = batch * n_t * TILE
    kern = functools.partial(_flash_gq_kernel, n_t=n_t)
    return pl.pallas_call(
        kern,
        grid=(batch, GQ_KV_HEADS),
        in_specs=[pl.BlockSpec((n_t, group, GQ_HEAD_DIM, TILE), lambda b, h: (b, h, 0, 0)),
                  pl.BlockSpec((n_t, TILE, GQ_KV_COLS), lambda b, h: (b, 0, 0)),
                  pl.BlockSpec((n_t, 1, GQ_HEAD_DIM + ONES_ROWS, TILE), lambda b, h: (b, h, 0, 0))],
        out_specs=pl.BlockSpec((n_t * TILE, group * GQ_HEAD_DIM), lambda b, h: (b, h)),
        out_shape=jax.ShapeDtypeStruct((m, GQ_Q_COLS), BF16),
        scratch_shapes=[pltpu.VMEM((n_t, group, 2 * GQ_HEAD_DIM, TILE), BF16),
                        pltpu.VMEM((group, GQ_HEAD_DIM + ONES_ROWS, TILE), F32),
                        pltpu.VMEM((group, TILE, TILE), F32), pltpu.VMEM((group, TILE, TILE), F32)],
        compiler_params=_params("parallel", "parallel"),
        name="flash_gq",
    )(qgq, kgq, vgq)


DN_CHUNK_COLS = 256
HALO = 16


def _dn_proj_kernel(h_ref, prev_ref, next_ref, w_ref, wba_ref, cw_ref, alog_ref, dtb_ref,
                    q_ref, k_ref, v_ref, gb_ref, *, n_t):
    j = pl.program_id(0) % n_t
    has_prev = (j >= 2).astype(BF16)
    has_next = jnp.logical_and(j >= 1, j <= n_t - 2).astype(BF16)
    h = h_ref[...]
    h_ext = jnp.concatenate([prev_ref[...] * has_prev, h, next_ref[...] * has_next], axis=0)
    n_ext = TILE + 2 * HALO
    rows = slice(HALO, HALO + TILE)
    ch = DN_CHUNK_COLS
    n_chunks = 3 * DN_WIDTH // ch
    per_part = DN_WIDTH // ch
    project = lambda c: _dot(h_ext, w_ref[:, c * ch:(c + 1) * ch])
    y_next = project(0)
    for c in range(n_chunks):
        y = y_next
        y_next = project(c + 1) if c + 1 < n_chunks else _dot(h, wba_ref[...])
        cols = slice(c * ch, (c + 1) * ch)
        a = (pltpu.roll(y, 1, 0)[rows] * cw_ref[0:1, cols] + y[rows] * cw_ref[1:2, cols]
             + pltpu.roll(y, n_ext - 1, 0)[rows] * cw_ref[2:3, cols])
        act = _silu(a)
        part, sub = divmod(c, per_part)
        if part == 2:
            v_ref[:, sub * ch:(sub + 1) * ch] = act.astype(v_ref.dtype)
            continue
        out_ref, scale = (q_ref, DN_HEAD_DIM ** -0.5) if part == 0 else (k_ref, 1.0)
        for hh in range(ch // DN_HEAD_DIM):
            x = act[:, hh * DN_HEAD_DIM:(hh + 1) * DN_HEAD_DIM]
            xn = x * (lax.rsqrt(jnp.sum(x * x, axis=-1, keepdims=True) + NORM_EPS) * scale)
            c0 = sub * ch + hh * DN_HEAD_DIM
            out_ref[:, c0:c0 + DN_HEAD_DIM] = xn.astype(out_ref.dtype)
    ba = y_next
    beta = _sigmoid(ba)
    z = ba + dtb_ref[...]
    softplus = jnp.maximum(z, 0.0) + jnp.log(1.0 + jnp.exp(-jnp.abs(z)))
    g = -jnp.exp(alog_ref[...]) * softplus
    col = lax.broadcasted_iota(jnp.int32, ba.shape, 1)
    gb_ref[...] = jnp.where(col < 2 * DN_HEADS, beta, g)


def _dn_proj(h, w_qkv, w_ba, conv_w, alog_tab, dtb_tab, n_t):
    m, d = h.shape
    w3 = 3 * DN_WIDTH
    n_blk = m // HALO
    per_tile = TILE // HALO
    tok = lambda c: pl.BlockSpec((TILE, c), lambda s: (s, 0))
    const = lambda s: (0, 0)
    return pl.pallas_call(
        functools.partial(_dn_proj_kernel, n_t=n_t),
        grid=(m // TILE,),
        in_specs=[tok(d),
                  pl.BlockSpec((HALO, d), lambda s: (jnp.maximum(s * per_tile - 1, 0), 0)),
                  pl.BlockSpec((HALO, d), lambda s: (jnp.minimum((s + 1) * per_tile, n_blk - 1), 0)),
                  pl.BlockSpec((d, w3), const), pl.BlockSpec((d, BA_COLS), const),
                  pl.BlockSpec((3, w3), const),
                  pl.BlockSpec((1, BA_COLS), const), pl.BlockSpec((1, BA_COLS), const)],
        out_specs=(tok(DN_WIDTH), tok(DN_WIDTH), tok(DN_WIDTH), tok(BA_COLS)),
        out_shape=(jax.ShapeDtypeStruct((m, DN_WIDTH), BF16),) * 3 + (jax.ShapeDtypeStruct((m, BA_COLS), F32),),
        compiler_params=_params("parallel"),
        name="dn_proj",
    )(h, h, h, w_qkv, w_ba, conv_w, alog_tab, dtb_tab)


def _dn_scan_kernel(qf_ref, kf_ref, vf_ref, gf_ref, qb_ref, kb_ref, vb_ref, gb_ref, of_ref, ob_ref,
                    s_ref, a16_ref, tinv_ref, t16_ref, w16_ref, qk16_ref, rhs16_ref, u_ref, uw16_ref,
                    kt16_ref, qd16_ref, kwb_ref, sc16_ref, vnew16_ref, inter_ref):
    t = pl.program_id(1)

    @pl.when(t == 0)
    def _():
        s_ref[...] = jnp.zeros_like(s_ref)

    n_chunk = TILE // CHUNK
    shift = CHUNK.bit_length() - 1
    dh = DN_HEAD_DIM
    ii = lax.broadcasted_iota(jnp.int32, (TILE, TILE), 0)
    jj = lax.broadcasted_iota(jnp.int32, (TILE, TILE), 1)
    same = (ii >> shift) == (jj >> shift)
    eye = (ii == jj).astype(F32)
    pair0 = (ii >> 1) == (jj >> 1)
    ins = ((qf_ref, kf_ref, vf_ref, gf_ref, of_ref), (qb_ref, kb_ref, vb_ref, gb_ref, ob_ref))
    chains = [(d, h) for d in range(2) for h in range(DN_HEADS)]
    incls = (jnp.logical_and(same, ii >= jj), jnp.logical_and(same, ii <= jj))
    stricts = (jnp.logical_and(same, ii > jj), jnp.logical_and(same, ii < jj))
    as_bf16 = lambda mask: jnp.where(mask, 1.0, 0.0).astype(BF16)
    same16 = as_bf16(same)
    n_g = 4 * DN_HEADS
    zpad = jnp.zeros((BA_COLS - n_g, TILE), F32)

    gls = []
    for d in range(2):
        incl, strict = incls[d], stricts[d]
        gb = ins[d][3][...]
        g_t = gb.T[0:n_g]
        p0 = g_t.astype(BF16)
        r1 = g_t - p0.astype(F32)
        p1 = r1.astype(BF16)
        p2 = (r1 - p1.astype(F32)).astype(BF16)
        terms = jnp.concatenate([p0, p1, p2], axis=0)

        def masked_sum(mask16):
            r = _dot(terms, mask16)
            return r[0:n_g] + r[n_g:2 * n_g] + r[2 * n_g:3 * n_g]

        gcum_t = masked_sum(as_bf16(incls[1 - d]))
        gtot_t = masked_sum(same16)
        gcum = jnp.concatenate([gcum_t, zpad], axis=0).T
        gtot = jnp.concatenate([gtot_t, zpad], axis=0).T
        for h in range(DN_HEADS):
            ch = d * DN_HEADS + h
            cg = 2 * DN_HEADS + ch
            lanes = slice(h * dh, (h + 1) * dh)
            q16 = ins[d][0][:, lanes]
            k16 = ins[d][1][:, lanes]
            q = q16.astype(F32)
            k = k16.astype(F32)
            v = ins[d][2][:, lanes].astype(F32)
            beta = gb[:, ch:ch + 1]
            gc = gcum[:, cg:cg + 1]
            gr = gcum_t[cg:cg + 1, :]
            gl = gtot[:, cg:cg + 1]
            gls.append(gl)
            decay = jnp.where(incl, jnp.exp(jnp.where(incl, gc - gr, 0.0)), 0.0)
            kb = k * beta
            a = jnp.where(strict, _dot_nt(kb.astype(BF16), k16) * decay, 0.0)
            a16_ref[ch] = a.astype(BF16)
            t0 = eye - jnp.where(pair0, a, 0.0)
            tinv_ref[ch] = t0
            t16_ref[ch] = t0.astype(BF16)
            qk16_ref[ch] = jnp.where(incl, _dot_nt(q16, k16) * decay, 0.0).astype(BF16)
            egc = jnp.exp(gc)
            rhs16_ref[ch] = jnp.concatenate([v * beta, kb * egc], axis=1).astype(BF16)
            kt16_ref[ch] = (k * jnp.exp(gl - gc)).astype(BF16)
            qd16_ref[ch] = (q * egc).astype(BF16)

    lvl = 1
    while (2 << lvl) <= CHUNK:
        sib = jnp.logical_and((ii >> (lvl + 1)) == (jj >> (lvl + 1)), (ii >> lvl) != (jj >> lvl))
        for ch in range(len(chains)):
            w16_ref[ch] = _dot(a16_ref[ch], t16_ref[ch]).astype(BF16)
        for ch in range(len(chains)):
            tn = jnp.where(sib, -_dot(t16_ref[ch], w16_ref[ch]), tinv_ref[ch])
            tinv_ref[ch] = tn
            t16_ref[ch] = tn.astype(BF16)
        lvl += 1

    for ch in range(len(chains)):
        uw = _dot(t16_ref[ch], rhs16_ref[ch])
        u_ref[ch] = uw[:, 0:dh]
        uw16_ref[ch] = uw.astype(BF16)

    for ch in range(len(chains)):
        for c in range(n_chunk):
            rows = slice(c * CHUNK, (c + 1) * CHUNK)
            kwb_ref[ch, c] = _dot_tn(kt16_ref[ch, rows, :], uw16_ref[ch, rows, :])

    states = [s_ref[ch] for ch in range(len(chains))]
    for step in range(n_chunk):
        for ch, (d, h) in enumerate(chains):
            c = n_chunk - 1 - step if d else step
            s16 = states[ch].astype(BF16)
            sc16_ref[ch, c] = s16
            egl = jnp.exp(gls[ch][c * CHUNK:c * CHUNK + 1, :])
            states[ch] = (states[ch] * egl - _dot(kwb_ref[ch, c, :, dh:].astype(BF16), s16)
                          + kwb_ref[ch, c, :, 0:dh])
    for ch in range(len(chains)):
        s_ref[ch] = states[ch]

    for ch in range(len(chains)):
        for c in range(n_chunk):
            rows = slice(c * CHUNK, (c + 1) * CHUNK)
            s16 = sc16_ref[ch, c]
            vnew16_ref[ch, rows, :] = (u_ref[ch, rows, :] - _dot(uw16_ref[ch, rows, dh:], s16)).astype(BF16)
            inter_ref[ch, rows, :] = _dot(qd16_ref[ch, rows, :], s16)
    for ch, (d, h) in enumerate(chains):
        ins[d][4][:, h * dh:(h + 1) * dh] = inter_ref[ch] + _dot(qk16_ref[ch], vnew16_ref[ch])


def _dn_scan(q, k, v, gb, batch, n_t):
    m = q.shape[0]
    fwd = lambda b, t: (b * n_t + t, 0)
    bwd = lambda b, t: (b * n_t + jnp.where(t == 0, 0, n_t - t), 0)
    specs = lambda tile: [pl.BlockSpec((TILE, DN_WIDTH), tile)] * 3 + [pl.BlockSpec((TILE, 128), tile)]
    nc = 2 * DN_HEADS
    dh = DN_HEAD_DIM
    big = lambda dt: pltpu.VMEM((nc, TILE, TILE), dt)
    half = lambda dt: pltpu.VMEM((nc, TILE, dh), dt)
    return pl.pallas_call(
        _dn_scan_kernel,
        grid=(batch, n_t),
        in_specs=specs(fwd) + specs(bwd),
        out_specs=(pl.BlockSpec((TILE, DN_WIDTH), fwd), pl.BlockSpec((TILE, DN_WIDTH), bwd)),
        out_shape=(jax.ShapeDtypeStruct((m, DN_WIDTH), F32),) * 2,
        scratch_shapes=[pltpu.VMEM((nc, dh, dh), F32),
                        big(BF16), big(F32), big(BF16), big(BF16), big(BF16), big(BF16),
                        half(F32), big(BF16), half(BF16), half(BF16),
                        pltpu.VMEM((nc, TILE // CHUNK, dh, 2 * dh), F32),
                        pltpu.VMEM((nc, TILE // CHUNK, dh, dh), BF16),
                        half(BF16), half(F32)],
        compiler_params=_params("parallel", "arbitrary"),
        name="dn_scan",
    )(q, k, v, gb, q, k, v, gb)


MERGE_CHUNK = 256


def _merge_kernel(x_ref, oa_ref, ob_ref, ocf_ref, ocb_ref, z_ref, gate_ref, dng_ref, mod_ref,
                  wa_ref, wb_ref, wc_ref, wo_ref, o_ref):
    d = x_ref.shape[1]
    oa = oa_ref[...]
    ob = ob_ref[...]
    ch = MERGE_CHUNK
    n_chunks = d // ch
    first_ab = _dot(oa, wa_ref[:, 0:ch]), _dot(ob, wb_ref[:, 0:ch])

    oc = ocf_ref[...] + ocb_ref[...]
    z = z_ref[...].astype(F32)
    parts = []
    for h in range(DN_HEADS):
        lanes = slice(h * DN_HEAD_DIM, (h + 1) * DN_HEAD_DIM)
        o = oc[:, lanes]
        ms = jnp.mean(o * o, axis=-1, keepdims=True)
        parts.append(o * lax.rsqrt(ms + NORM_EPS) * dng_ref[...] * _silu(z[:, lanes]))
    oc_n = jnp.concatenate(parts, axis=1).astype(BF16)

    def branches(c):
        cols = slice(c * ch, (c + 1) * ch)
        return _dot(oa, wa_ref[:, cols]), _dot(ob, wb_ref[:, cols]), _dot(oc_n, wc_ref[:, cols])

    nxt = first_ab + (_dot(oc_n, wc_ref[:, 0:ch]),)
    acc = None
    for c in range(n_chunks):
        pa, pb, pc = nxt
        if c + 1 < n_chunks:
            nxt = branches(c + 1)
        gate = lambda i: _sigmoid(gate_ref[:, i * d + c * ch:i * d + (c + 1) * ch].astype(F32))
        mix = gate(0) * pa + gate(1) * pb + gate(2) * pc
        part = _dot(mix.astype(BF16), wo_ref[c * ch:(c + 1) * ch, :])
        acc = part if acc is None else acc + part
    o_ref[...] = x_ref[...] + mod_ref[0] * acc


def _merge(x, oa, ob, ocf, ocb, y_nat, dn_g, mod_tab, wa, wb, wc, wo):
    m, d = x.shape
    tok = lambda c, cb=0: pl.BlockSpec((TILE, c), lambda s: (s, cb))
    full = lambda a: pl.BlockSpec(a.shape, lambda s: (0,) * a.ndim)
    return pl.pallas_call(
        _merge_kernel,
        grid=(m // TILE,),
        in_specs=[tok(d), tok(DA_COLS), tok(GQ_Q_COLS), tok(DN_WIDTH), tok(DN_WIDTH),
                  tok(DN_WIDTH, NAT_Z_OFF // DN_WIDTH), tok(3 * d, NAT_GATE_OFF),
                  full(dn_g), pl.BlockSpec((1, 1, d), lambda s: (s, 0, 0)),
                  full(wa), full(wb), full(wc), full(wo)],
        out_specs=tok(d),
        out_shape=jax.ShapeDtypeStruct((m, d), F32),
        compiler_params=_params("parallel"),
        name="merge",
    )(x, oa, ob, ocf, ocb, y_nat, y_nat, dn_g, mod_tab, wa, wb, wc, wo)


FFN_CHUNK = 256


def _ffn_kernel(x_ref, prev_ref, next_ref, g_ref, shift_ref, scale_ref, w1_ref, cw_ref, cb_ref, mod_ref, w2_ref,
                o_ref, acc_ref, *, n_t):
    j = pl.program_id(0) % n_t
    has_prev = (j >= 2).astype(F32)
    has_next = jnp.logical_and(j >= 1, j <= n_t - 2).astype(F32)

    def norm(x):
        ms = jnp.mean(x * x, axis=-1, keepdims=True)
        return (x * lax.rsqrt(ms + NORM_EPS) * g_ref[...]) * (1.0 + scale_ref[0]) + shift_ref[0]

    x = x_ref[...]
    h_ext = jnp.concatenate([norm(prev_ref[...]) * has_prev, norm(x), norm(next_ref[...]) * has_next],
                            axis=0).astype(BF16)
    n_ext = TILE + 16
    ch = FFN_CHUNK
    dff = cw_ref.shape[1]
    n_chunks = dff // ch

    def project(c):
        return (_dot(h_ext, w1_ref[:, c * ch:(c + 1) * ch]),
                _dot(h_ext, w1_ref[:, dff + c * ch:dff + (c + 1) * ch]))

    au_next = project(0)
    for c in range(n_chunks):
        a_ext, u_ext = au_next
        if c + 1 < n_chunks:
            au_next = project(c + 1)
        cols = slice(c * ch, (c + 1) * ch)
        a = (pltpu.roll(a_ext, 1, 0)[8:8 + TILE] * cw_ref[0:1, cols] + a_ext[8:8 + TILE] * cw_ref[1:2, cols]
             + pltpu.roll(a_ext, n_ext - 1, 0)[8:8 + TILE] * cw_ref[2:3, cols] + cb_ref[:, cols])
        act = (_silu(a) * u_ext[8:8 + TILE]).astype(BF16)
        part = _dot(act, w2_ref[cols, :])
        if c == 0:
            acc_ref[...] = part
        else:
            acc_ref[...] += part
    o_ref[...] = x + mod_ref[0] * acc_ref[...]


def _ffn(x, g, shift_tab, scale_tab, w1, conv_w, conv_b, mod_tab, w2, n_t, latent_only=False):
    m, d = x.shape
    dff = w2.shape[0]
    n8 = m // 8
    row = lambda s: (s, 0, 0)
    const = lambda s: (0, 0)
    if latent_only:
        out_rows = m // n_t * (n_t - 1)
        out_map = lambda s: (s // n_t * (n_t - 1) + jnp.maximum(s % n_t - 1, 0), 0)
    else:
        out_rows, out_map = m, lambda s: (s, 0)
    return pl.pallas_call(
        functools.partial(_ffn_kernel, n_t=n_t),
        grid=(m // TILE,),
        in_specs=[pl.BlockSpec((TILE, d), lambda s: (s, 0)),
                  pl.BlockSpec((8, d), lambda s: (jnp.maximum(s * (TILE // 8) - 1, 0), 0)),
                  pl.BlockSpec((8, d), lambda s: (jnp.minimum((s + 1) * (TILE // 8), n8 - 1), 0)),
                  pl.BlockSpec((1, d), const),
                  pl.BlockSpec((1, 1, d), row), pl.BlockSpec((1, 1, d), row),
                  pl.BlockSpec((d, 2 * dff), const),
                  pl.BlockSpec((3, dff), const), pl.BlockSpec((1, dff), const),
                  pl.BlockSpec((1, 1, d), row),
                  pl.BlockSpec((dff, d), const)],
        out_specs=pl.BlockSpec((TILE, d), out_map),
        out_shape=jax.ShapeDtypeStruct((out_rows, d), F32),
        scratch_shapes=[pltpu.VMEM((TILE, d), F32)],
        compiler_params=_params("arbitrary"),
        name="ffn",
    )(x, x, x, g.reshape(1, d), shift_tab, scale_tab, w1, conv_w, conv_b.reshape(1, dff), mod_tab, w2)


def _rope_tables(n_lat):
    t = jnp.arange(n_lat, dtype=jnp.int32)
    row = (t // GRID_W).astype(F32)
    col = (t % GRID_W).astype(F32)
    d_axis = 32
    inv_freq = ROPE_THETA ** (-jnp.arange(0, d_axis, 2, dtype=F32) / d_axis)
    ang_r = row[None, :] * inv_freq[:, None]
    ang_c = col[None, :] * inv_freq[:, None]
    cr, sr, cc, sc = jnp.cos(ang_r), jnp.sin(ang_r), jnp.cos(ang_c), jnp.sin(ang_c)
    cos_lat = jnp.concatenate([cr, cr, cc, cc], axis=0)
    sin_lat = jnp.concatenate([-sr, sr, -sc, sc], axis=0)
    cos_t = jnp.concatenate([jnp.ones((64, TILE), F32), cos_lat], axis=1)
    sin_t = jnp.concatenate([jnp.zeros((64, TILE), F32), sin_lat], axis=1)
    return cos_t, sin_t


def kernel(x, c, ctx, c_ctx, w_mod, b_mod, norm1_g, w_in, da_qn_g, da_kn_g, da_lambda, da_subln_g, gq_qn_g, gq_kn_g, dn_conv_w, dn_a_log, dn_dt_bias, dn_norm_g, w_br_a, w_br_b, w_br_c, w_o, norm2_g, ffn_w1, ffn_conv_w, ffn_conv_b, ffn_w2):
    batch, n_lat, d = x.shape
    depth = w_mod.shape[0]
    dff = ffn_w2.shape[1]
    assert ctx.shape[1] == TILE and n_lat % TILE == 0 and n_lat % GRID_W == 0 and batch + 1 <= 8
    n_t = 1 + n_lat // TILE
    assert n_t % 2 == 1, "the attention key loop handles key tiles in pairs plus one"
    s_tiles = batch * n_t
    m = s_tiles * TILE

    xs = jnp.concatenate([ctx, x], axis=1).reshape(m, d)

    cond = jnp.zeros((8, d), F32).at[:batch].set(c).at[batch].set(c_ctx)
    mod = _modulation(cond, w_mod, b_mod)
    tile_id = jnp.arange(s_tiles)
    row_of_tile = jnp.where(tile_id % n_t == 0, batch, tile_id // n_t)
    mod_tabs = mod[:, row_of_tile, :].reshape(depth, s_tiles, 6, 1, d)

    cos_t, sin_t = _rope_tables(n_lat)
    cos_m, sin_m = jnp.tile(cos_t, (1, batch)), jnp.tile(sin_t, (1, batch))
    bcast = lambda g: jnp.broadcast_to(g[:, None], (g.shape[0], TILE))

    o_qkv = ATTN_ROWS
    o_b = o_qkv + 3 * DN_WIDTH
    o_z = o_b + 4 * DN_HEADS
    o_g = o_z + DN_WIDTH
    wt_attn = jnp.swapaxes(w_in[:, :, :ATTN_ROWS], 1, 2).astype(BF16)
    w_nat = jnp.concatenate([w_in[:, :, o_g:], w_in[:, :, o_z:o_g]], axis=2).astype(BF16)
    w_qkv = w_in[:, :, o_qkv:o_b].astype(BF16)
    w_ba = jnp.concatenate([w_in[:, :, o_b:o_z], jnp.zeros((depth, d, BA_COLS - 4 * DN_HEADS), F32)],
                           axis=2).astype(BF16)
    pad_tab = lambda a: jnp.zeros((1, BA_COLS), F32).at[0, 2 * DN_HEADS:4 * DN_HEADS].set(a.reshape(-1))

    for l in range(depth):
        lam_init = 0.8 - 0.6 * math.exp(-0.3 * l)
        tab = lambda i: mod_tabs[l, :, i]

        h1 = _norm_mod(xs, norm1_g[l], tab(0), tab(1))
        y_nat = _matmul_nn(h1, w_nat[l], NAT_COLS, BF16)

        gains = jnp.stack([bcast(da_qn_g[l]), bcast(da_kn_g[l]), bcast(gq_qn_g[l]), bcast(gq_kn_g[l])])
        qda, kda, vda, qgq, kgq, vgq = _inproj_attn(wt_attn[l], h1, cos_m, sin_m, gains)
        oa = _flash_da(qda, kda, vda, da_lambda[l], bcast(da_subln_g[l]), batch, n_t, lam_init)
        ob = _flash_gq(qgq, kgq, vgq, batch, n_t)

        dq, dk, dv, gb = _dn_proj(h1, w_qkv[l], w_ba[l], dn_conv_w[l], pad_tab(dn_a_log[l]),
                                  pad_tab(dn_dt_bias[l]), n_t)
        ocf, ocb = _dn_scan(dq, dk, dv, gb, batch, n_t)

        xs = _merge(xs, oa, ob, ocf, ocb, y_nat, dn_norm_g[l].reshape(1, -1), tab(2),
                    w_br_a[l].astype(BF16), w_br_b[l].astype(BF16), w_br_c[l].astype(BF16), w_o[l].astype(BF16))

        xs = _ffn(xs, norm2_g[l], tab(3), tab(4), ffn_w1[l].astype(BF16), ffn_conv_w[l], ffn_conv_b[l], tab(5),
                  ffn_w2[l].astype(BF16), n_t, latent_only=(l == depth - 1))

    return xs.reshape(batch, n_lat, d)
```

```python
import functools
import math

import jax
import jax.numpy as jnp
from jax import lax
from jax.experimental import pallas as pl
from jax.experimental.pallas import tpu as pltpu

F32 = jnp.float32
BF16 = jnp.bfloat16

TILE = 256
CHUNK = 64
GRID_W = 64
ROPE_THETA = 10000.0
NORM_EPS = 1e-6
DA_HEADS, DA_HEAD_DIM = 4, 64
GQ_HEADS, GQ_KV_HEADS, GQ_HEAD_DIM = 8, 2, 64
DN_HEADS, DN_HEAD_DIM = 4, 128
LOG2E = 1.4426950408889634
NEG_BIG = -1e30
ONES_ROWS = 16
VMEM_LIMIT = 48 * 1024 * 1024

DA_COLS = DA_HEADS * 2 * DA_HEAD_DIM
GQ_Q_COLS = GQ_HEADS * GQ_HEAD_DIM
GQ_KV_COLS = GQ_KV_HEADS * GQ_HEAD_DIM
DN_WIDTH = DN_HEADS * DN_HEAD_DIM
ATTN_ROWS = 3 * DA_COLS + GQ_Q_COLS + 2 * GQ_KV_COLS
NAT_GATE_OFF, NAT_Z_OFF = 0, 3072
NAT_COLS = 3584
BA_COLS = 128


def _params(*sem):
    return pltpu.CompilerParams(dimension_semantics=sem, vmem_limit_bytes=VMEM_LIMIT)


def _sigmoid(x):
    return 1.0 / (1.0 + jnp.exp(-x))


def _silu(x):
    return x * _sigmoid(x)


def _dot(a, b):
    return jnp.dot(a, b, preferred_element_type=F32)


def _dot_nt(a, b):
    return lax.dot_general(a, b, (((1,), (1,)), ((), ())), preferred_element_type=F32)


def _dot_tn(a, b):
    return lax.dot_general(a, b, (((0,), (0,)), ((), ())), preferred_element_type=F32)


def _mod_kernel(c_ref, w_ref, b_ref, o_ref):
    cond = _silu(c_ref[...])
    o_ref[0] = jnp.dot(cond, w_ref[0], preferred_element_type=F32,
                       precision=lax.Precision.HIGHEST) + b_ref[0]


def _modulation(cond, w_mod, b_mod):
    depth, d, n = w_mod.shape
    tn = 1536
    return pl.pallas_call(
        _mod_kernel,
        grid=(depth, n // tn),
        in_specs=[pl.BlockSpec((8, d), lambda l, j: (0, 0)),
                  pl.BlockSpec((1, d, tn), lambda l, j: (l, 0, j)),
                  pl.BlockSpec((1, 1, tn), lambda l, j: (l, 0, j))],
        out_specs=pl.BlockSpec((1, 8, tn), lambda l, j: (l, 0, j)),
        out_shape=jax.ShapeDtypeStruct((depth, 8, n), F32),
        compiler_params=_params("parallel", "parallel"),
        name="modulation",
    )(cond, w_mod, b_mod.reshape(depth, 1, n))


def _norm_mod_kernel(x_ref, g_ref, shift_ref, scale_ref, o_ref):
    for i in range(x_ref.shape[0] // TILE):
        rows = slice(i * TILE, (i + 1) * TILE)
        x = x_ref[rows, :]
        ms = jnp.mean(x * x, axis=-1, keepdims=True)
        y = x * lax.rsqrt(ms + NORM_EPS) * g_ref[...]
        o_ref[rows, :] = (y * (1.0 + scale_ref[i]) + shift_ref[i]).astype(o_ref.dtype)


def _norm_mod(x, g, shift_tab, scale_tab):
    m, d = x.shape
    tm = _row_tile(m)
    sub = tm // TILE
    return pl.pallas_call(
        _norm_mod_kernel,
        grid=(m // tm,),
        in_specs=[pl.BlockSpec((tm, d), lambda s: (s, 0)),
                  pl.BlockSpec((1, d), lambda s: (0, 0)),
                  pl.BlockSpec((sub, 1, d), lambda s: (s, 0, 0)),
                  pl.BlockSpec((sub, 1, d), lambda s: (s, 0, 0))],
        out_specs=pl.BlockSpec((tm, d), lambda s: (s, 0)),
        out_shape=jax.ShapeDtypeStruct((m, d), BF16),
        compiler_params=_params("parallel"),
        name="norm_mod",
    )(x, g.reshape(1, d), shift_tab, scale_tab)


def _mm_nn_kernel(a_ref, b_ref, o_ref):
    o_ref[...] = _dot(a_ref[...], b_ref[...]).astype(o_ref.dtype)


def _row_tile(m):
    for t in (1024, 512, 256):
        if m % t == 0:
            return t
    raise ValueError(m)


def _matmul_nn(a, b, tn, out_dtype):
    m, k = a.shape
    n = b.shape[1]
    tm = _row_tile(m)
    return pl.pallas_call(
        _mm_nn_kernel,
        grid=(m // tm, n // tn),
        in_specs=[pl.BlockSpec((tm, k), lambda i, j: (i, 0)),
                  pl.BlockSpec((k, tn), lambda i, j: (0, j))],
        out_specs=pl.BlockSpec((tm, tn), lambda i, j: (i, j)),
        out_shape=jax.ShapeDtypeStruct((m, n), out_dtype),
        compiler_params=_params("parallel", "parallel"),
        name="matmul_nn",
    )(a, b)


def _norm_rope_t(x, gain, cos, sin, scale):
    x = x.astype(F32)
    ms = jnp.mean(x * x, axis=0, keepdims=True)
    xn = x * lax.rsqrt(ms + NORM_EPS) * gain
    sw = jnp.concatenate([xn[16:32], xn[0:16], xn[48:64], xn[32:48]], axis=0)
    out = xn * cos + sw * sin
    return out * scale if scale != 1.0 else out


_ATTN_SECTIONS = ((0, DA_COLS), (DA_COLS, DA_COLS), (2 * DA_COLS, DA_COLS), (3 * DA_COLS, GQ_Q_COLS),
                  (3 * DA_COLS + GQ_Q_COLS, 2 * GQ_KV_COLS))


def _attn_prep_section(sec, y, i, cos, sin, gain_ref, qda_ref, kda_ref, vda_ref, qgq_ref, kgq_ref, vgq_ref):
    ones = jnp.ones((ONES_ROWS, TILE), F32)
    hd = DA_HEAD_DIM
    gd = GQ_HEAD_DIM
    if sec in (0, 1):
        scale = (hd ** -0.5) * LOG2E if sec == 0 else 1.0
        for h in range(DA_HEADS):
            halves = [_norm_rope_t(y[(2 * h + half) * hd:(2 * h + half + 1) * hd], gain_ref[sec], cos, sin, scale)
                      for half in range(2)]
            both = jnp.concatenate(halves, axis=0)
            if sec == 0:
                qda_ref[i, h] = both.astype(BF16)
            else:
                kda_ref[i, h] = both.T.astype(BF16)
    elif sec == 2:
        for h in range(DA_HEADS):
            vda_ref[i, h] = jnp.concatenate([y[2 * h * hd:2 * (h + 1) * hd], ones], axis=0).astype(BF16)
    elif sec == 3:
        for h in range(GQ_HEADS):
            qgq_ref[i, h] = _norm_rope_t(y[h * gd:(h + 1) * gd], gain_ref[2], cos, sin,
                                         (gd ** -0.5) * LOG2E).astype(BF16)
    else:
        ks = [_norm_rope_t(y[h * gd:(h + 1) * gd], gain_ref[3], cos, sin, 1.0) for h in range(GQ_KV_HEADS)]
        kgq_ref[i] = jnp.concatenate(ks, axis=0).T.astype(BF16)
        for h in range(GQ_KV_HEADS):
            r = GQ_KV_COLS + h * gd
            vgq_ref[i, h] = jnp.concatenate([y[r:r + gd], ones], axis=0).astype(BF16)


def _inproj_attn_kernel(wt_ref, h_ref, cos_ref, sin_ref, gain_ref,
                        qda_ref, kda_ref, vda_ref, qgq_ref, kgq_ref, vgq_ref):
    n_sub = h_ref.shape[0] // TILE
    pieces = [(i, sec) for i in range(n_sub) for sec in range(len(_ATTN_SECTIONS))]

    def project(piece):
        i, sec = piece
        r0, rows = _ATTN_SECTIONS[sec]
        return _dot_nt(wt_ref[r0:r0 + rows, :], h_ref[i * TILE:(i + 1) * TILE, :])

    y_next = project(pieces[0])
    for n, (i, sec) in enumerate(pieces):
        y = y_next
        if n + 1 < len(pieces):
            y_next = project(pieces[n + 1])
        lanes = slice(i * TILE, (i + 1) * TILE)
        _attn_prep_section(sec, y, i, cos_ref[:, lanes], sin_ref[:, lanes], gain_ref,
                           qda_ref, kda_ref, vda_ref, qgq_ref, kgq_ref, vgq_ref)


def _inproj_attn(wt, h, cos_m, sin_m, gains):
    m, k = h.shape
    tm = _row_tile(m)
    sub = tm // TILE
    s_tiles = m // TILE
    out_shape = (
        jax.ShapeDtypeStruct((s_tiles, DA_HEADS, 2 * DA_HEAD_DIM, TILE), BF16),
        jax.ShapeDtypeStruct((s_tiles, DA_HEADS, TILE, 2 * DA_HEAD_DIM), BF16),
        jax.ShapeDtypeStruct((s_tiles, DA_HEADS, 2 * DA_HEAD_DIM + ONES_ROWS, TILE), BF16),
        jax.ShapeDtypeStruct((s_tiles, GQ_HEADS, GQ_HEAD_DIM, TILE), BF16),
        jax.ShapeDtypeStruct((s_tiles, TILE, GQ_KV_COLS), BF16),
        jax.ShapeDtypeStruct((s_tiles, GQ_KV_HEADS, GQ_HEAD_DIM + ONES_ROWS, TILE), BF16),
    )
    blk = lambda sds: pl.BlockSpec((sub,) + sds.shape[1:], lambda s: (s,) + (0,) * (len(sds.shape) - 1))
    return pl.pallas_call(
        _inproj_attn_kernel,
        grid=(m // tm,),
        in_specs=[pl.BlockSpec((ATTN_ROWS, k), lambda s: (0, 0)),
                  pl.BlockSpec((tm, k), lambda s: (s, 0)),
                  pl.BlockSpec((GQ_HEAD_DIM, tm), lambda s: (0, s)),
                  pl.BlockSpec((GQ_HEAD_DIM, tm), lambda s: (0, s)),
                  pl.BlockSpec((4, GQ_HEAD_DIM, TILE), lambda s: (0, 0, 0))],
        out_specs=tuple(blk(s) for s in out_shape),
        out_shape=out_shape,
        compiler_params=_params("parallel"),
        name="inproj_attn",
    )(wt, h, cos_m, sin_m, gains)


def _softmax_update(s_ref, idx, m_tile, m_old):
    m_new = jnp.maximum(m_old, m_tile)
    alpha = jnp.exp2(m_old - m_new)
    p = jnp.exp2(s_ref[idx] - m_new)
    return p.astype(BF16), alpha, m_new


def _attention_program(n_t, n_mat, scores, consume, finalize):
    neg = (jnp.full((1, TILE), NEG_BIG, F32),) * n_mat
    consume(0, 0, scores(0, 0, 0), neg)
    finalize(0)
    half = (n_t - 1) // 2
    pairs_per_iter = next(u for u in (4, 2, 1) if half % u == 0)
    mt0 = scores(1, 0, 0)

    def query_pair(qp, mt):
        for par in (0, 1):
            qi = 1 + 2 * qp + par
            a, b = (0, 1) if par == 0 else (1, 0)

            def key_pairs(i, carry, qi=qi, a=a, b=b):
                mt_a, ms = carry[:n_mat], carry[n_mat:]
                for u in range(pairs_per_iter):
                    j = 2 * (pairs_per_iter * i + u)
                    mt_b = scores(qi, j + 1, b)
                    ms = consume(j, a, mt_a, ms)
                    mt_a = scores(qi, j + 2, a)
                    ms = consume(j + 1, b, mt_b, ms)
                return tuple(mt_a) + tuple(ms)

            carry = lax.fori_loop(0, half // pairs_per_iter, key_pairs, tuple(mt) + neg)
            mt = scores(jnp.minimum(qi + 1, n_t - 1), 0, b)
            consume(n_t - 1, a, carry[:n_mat], carry[n_mat:])
            finalize(qi)
        return tuple(mt)

    lax.fori_loop(0, half, query_pair, tuple(mt0))


DA_HEADS_PER_STEP = 2


def _flash_da_kernel(q_ref, k_ref, v_ref, lam_ref, g_ref, o_ref, qpad_ref, acc_ref, sa_ref, sb_ref,
                     *, n_t, lam_init):
    dv = 2 * DA_HEAD_DIM
    hps = DA_HEADS_PER_STEP
    bufs = (sa_ref, sb_ref)
    acc_ref[...] = jnp.zeros_like(acc_ref)
    top = lax.broadcasted_iota(jnp.int32, (dv, TILE), 0) < DA_HEAD_DIM

    def pad_queries(qi, carry):
        for hh in range(hps):
            q = q_ref[qi, hh]
            zero = jnp.zeros_like(q)
            qpad_ref[qi, 2 * hh] = jnp.where(top, q, zero)
            qpad_ref[qi, 2 * hh + 1] = jnp.where(top, zero, q)
        return carry

    lax.fori_loop(0, n_t, pad_queries, 0)
    lp = lam_ref[...]
    lam = (jnp.exp(jnp.sum(lp[0:1] * lp[1:2], keepdims=True))
           - jnp.exp(jnp.sum(lp[2:3] * lp[3:4], keepdims=True)) + lam_init)

    def scores(qi, j, buf):
        out = []
        for hh in range(hps):
            k = k_ref[j, hh]
            for i in range(2):
                s = _dot(k, qpad_ref[qi, 2 * hh + i])
                bufs[buf][2 * hh + i] = s
                out.append(jnp.max(s, axis=0, keepdims=True))
        return out

    def consume(j, buf, mts, ms):
        out = []
        for hh in range(hps):
            vt = v_ref[j, hh]
            for i in range(2):
                c = 2 * hh + i
                p, alpha, m_new = _softmax_update(bufs[buf], c, mts[c], ms[c])
                acc_ref[c] = acc_ref[c] * alpha + _dot(vt, p)
                out.append(m_new)
        return tuple(out)

    def finalize(qi):
        rows = pl.ds(pl.multiple_of(qi * TILE, TILE), TILE)
        for hh in range(hps):
            a1, a2 = acc_ref[2 * hh], acc_ref[2 * hh + 1]
            o = a1[0:dv] / a1[dv:dv + 1] - lam * (a2[0:dv] / a2[dv:dv + 1])
            ms = jnp.mean(o * o, axis=0, keepdims=True)
            o = o * lax.rsqrt(ms + NORM_EPS) * g_ref[...] * (1.0 - lam_init)
            o_ref[rows, hh * dv:(hh + 1) * dv] = o.T.astype(o_ref.dtype)
        acc_ref[...] = jnp.zeros_like(acc_ref)

    _attention_program(n_t, 2 * hps, scores, consume, finalize)


def _flash_da(qda, kda, vda, lam_p, subln_g, batch, n_t, lam_init):
    dv = 2 * DA_HEAD_DIM
    hps = DA_HEADS_PER_STEP
    m = batch * n_t * TILE
    kern = functools.partial(_flash_da_kernel, n_t=n_t, lam_init=lam_init)
    return pl.pallas_call(
        kern,
        grid=(batch, DA_HEADS // hps),
        in_specs=[pl.BlockSpec((n_t, hps, dv, TILE), lambda b, h: (b, h, 0, 0)),
                  pl.BlockSpec((n_t, hps, TILE, dv), lambda b, h: (b, h, 0, 0)),
                  pl.BlockSpec((n_t, hps, dv + ONES_ROWS, TILE), lambda b, h: (b, h, 0, 0)),
                  pl.BlockSpec((4, DA_HEAD_DIM), lambda b, h: (0, 0)),
                  pl.BlockSpec((dv, TILE), lambda b, h: (0, 0))],
        out_specs=pl.BlockSpec((n_t * TILE, hps * dv), lambda b, h: (b, h)),
        out_shape=jax.ShapeDtypeStruct((m, DA_HEADS * dv), BF16),
        scratch_shapes=[pltpu.VMEM((n_t, 2 * hps, dv, TILE), BF16),
                        pltpu.VMEM((2 * hps, dv + ONES_ROWS, TILE), F32),
                        pltpu.VMEM((2 * hps, TILE, TILE), F32), pltpu.VMEM((2 * hps, TILE, TILE), F32)],
        compiler_params=_params("parallel", "parallel"),
        name="flash_da",
    )(qda, kda, vda, lam_p, subln_g)


def _flash_gq_kernel(q_ref, k_ref, v_ref, o_ref, qpad_ref, acc_ref, sa_ref, sb_ref, *, n_t):
    first = pl.program_id(1) == 0
    group = GQ_HEADS // GQ_KV_HEADS
    dv = GQ_HEAD_DIM
    bufs = (sa_ref, sb_ref)
    acc_ref[...] = jnp.zeros_like(acc_ref)

    def pad_queries(qi, carry):
        for g in range(group):
            q = q_ref[qi, g]
            zero = jnp.zeros_like(q)
            qpad_ref[qi, g] = jnp.concatenate([jnp.where(first, q, zero), jnp.where(first, zero, q)], axis=0)
        return carry

    lax.fori_loop(0, n_t, pad_queries, 0)

    def scores(qi, j, buf):
        k = k_ref[j]
        out = []
        for g in range(group):
            s = _dot(k, qpad_ref[qi, g])
            bufs[buf][g] = s
            out.append(jnp.max(s, axis=0, keepdims=True))
        return out

    def consume(j, buf, mts, ms):
        vt = v_ref[j, 0]
        out = []
        for g in range(group):
            p, alpha, m_new = _softmax_update(bufs[buf], g, mts[g], ms[g])
            acc_ref[g] = acc_ref[g] * alpha + _dot(vt, p)
            out.append(m_new)
        return tuple(out)

    def finalize(qi):
        rows = pl.ds(pl.multiple_of(qi * TILE, TILE), TILE)
        o = jnp.concatenate([acc_ref[g, 0:dv, :] / acc_ref[g, dv:dv + 1, :] for g in range(group)], axis=0)
        o_ref[rows, :] = o.T.astype(o_ref.dtype)
        acc_ref[...] = jnp.zeros_like(acc_ref)

    _attention_program(n_t, group, scores, consume, finalize)


def _flash_gq(qgq, kgq, vgq, batch, n_t):
    group = GQ_HEADS // GQ_KV_HEADS
    m = batch * n_t * TILE
    kern = functools.partial(_flash_gq_kernel, n_t=n_t)
    return pl.pallas_call(
        kern,
        grid=(batch, GQ_KV_HEADS),
        in_specs=[pl.BlockSpec((n_t, group, GQ_HEAD_DIM, TILE), lambda b, h: (b, h, 0, 0)),
                  pl.BlockSpec((n_t, TILE, GQ_KV_COLS), lambda b, h: (b, 0, 0)),
                  pl.BlockSpec((n_t, 1, GQ_HEAD_DIM + ONES_ROWS, TILE), lambda b, h: (b, h, 0, 0))],
        out_specs=pl.BlockSpec((n_t * TILE, group * GQ_HEAD_DIM), lambda b, h: (b, h)),
        out_shape=jax.ShapeDtypeStruct((m, GQ_Q_COLS), BF16),
        scratch_shapes=[pltpu.VMEM((n_t, group, 2 * GQ_HEAD_DIM, TILE), BF16),
                        pltpu.VMEM((group, GQ_HEAD_DIM + ONES_ROWS, TILE), F32),
                        pltpu.VMEM((group, TILE, TILE), F32), pltpu.VMEM((group, TILE, TILE), F32)],
        compiler_params=_params("parallel", "parallel"),
        name="flash_gq",
    )(qgq, kgq, vgq)


DN_CHUNK_COLS = 256
HALO = 16


def _dn_proj_kernel(h_ref, prev_ref, next_ref, w_ref, wba_ref, cw_ref, alog_ref, dtb_ref,
                    q_ref, k_ref, v_ref, gb_ref, *, n_t):
    j = pl.program_id(0) % n_t
    has_prev = (j >= 2).astype(BF16)
    has_next = jnp.logical_and(j >= 1, j <= n_t - 2).astype(BF16)
    h = h_ref[...]
    h_ext = jnp.concatenate([prev_ref[...] * has_prev, h, next_ref[...] * has_next], axis=0)
    n_ext = TILE + 2 * HALO
    rows = slice(HALO, HALO + TILE)
    ch = DN_CHUNK_COLS
    n_chunks = 3 * DN_WIDTH // ch
    per_part = DN_WIDTH // ch
    project = lambda c: _dot(h_ext, w_ref[:, c * ch:(c + 1) * ch])
    y_next = project(0)
    for c in range(n_chunks):
        y = y_next
        y_next = project(c + 1) if c + 1 < n_chunks else _dot(h, wba_ref[...])
        cols = slice(c * ch, (c + 1) * ch)
        a = (pltpu.roll(y, 1, 0)[rows] * cw_ref[0:1, cols] + y[rows] * cw_ref[1:2, cols]
             + pltpu.roll(y, n_ext - 1, 0)[rows] * cw_ref[2:3, cols])
        act = _silu(a)
        part, sub = divmod(c, per_part)
        if part == 2:
            v_ref[:, sub * ch:(sub + 1) * ch] = act.astype(v_ref.dtype)
            continue
        out_ref, scale = (q_ref, DN_HEAD_DIM ** -0.5) if part == 0 else (k_ref, 1.0)
        for hh in range(ch // DN_HEAD_DIM):
            x = act[:, hh * DN_HEAD_DIM:(hh + 1) * DN_HEAD_DIM]
            xn = x * (lax.rsqrt(jnp.sum(x * x, axis=-1, keepdims=True) + NORM_EPS) * scale)
            c0 = sub * ch + hh * DN_HEAD_DIM
            out_ref[:, c0:c0 + DN_HEAD_DIM] = xn.astype(out_ref.dtype)
    ba = y_next
    beta = _sigmoid(ba)
    z = ba + dtb_ref[...]
    softplus = jnp.maximum(z, 0.0) + jnp.log(1.0 + jnp.exp(-jnp.abs(z)))
    g = -jnp.exp(alog_ref[...]) * softplus
    col = lax.broadcasted_iota(jnp.int32, ba.shape, 1)
    gb_ref[...] = jnp.where(col < 2 * DN_HEADS, beta, g)


def _dn_proj(h, w_qkv, w_ba, conv_w, alog_tab, dtb_tab, n_t):
    m, d = h.shape
    w3 = 3 * DN_WIDTH
    n_blk = m // HALO
    per_tile = TILE // HALO
    tok = lambda c: pl.BlockSpec((TILE, c), lambda s: (s, 0))
    const = lambda s: (0, 0)
    return pl.pallas_call(
        functools.partial(_dn_proj_kernel, n_t=n_t),
        grid=(m // TILE,),
        in_specs=[tok(d),
                  pl.BlockSpec((HALO, d), lambda s: (jnp.maximum(s * per_tile - 1, 0), 0)),
                  pl.BlockSpec((HALO, d), lambda s: (jnp.minimum((s + 1) * per_tile, n_blk - 1), 0)),
                  pl.BlockSpec((d, w3), const), pl.BlockSpec((d, BA_COLS), const),
                  pl.BlockSpec((3, w3), const),
                  pl.BlockSpec((1, BA_COLS), const), pl.BlockSpec((1, BA_COLS), const)],
        out_specs=(tok(DN_WIDTH), tok(DN_WIDTH), tok(DN_WIDTH), tok(BA_COLS)),
        out_shape=(jax.ShapeDtypeStruct((m, DN_WIDTH), BF16),) * 3 + (jax.ShapeDtypeStruct((m, BA_COLS), F32),),
        compiler_params=_params("parallel"),
        name="dn_proj",
    )(h, h, h, w_qkv, w_ba, conv_w, alog_tab, dtb_tab)


def _dn_scan_kernel(qf_ref, kf_ref, vf_ref, gf_ref, qb_ref, kb_ref, vb_ref, gb_ref, of_ref, ob_ref,
                    s_ref, a16_ref, t16_ref, w16_ref, qk16_ref, rhs16_ref, u_ref, uw16_ref,
                    kt16_ref, qd16_ref, kwb_ref, sc16_ref, vnew16_ref, inter_ref):
    t = pl.program_id(1)

    @pl.when(t == 0)
    def _():
        s_ref[...] = jnp.zeros_like(s_ref)

    n_chunk = TILE // CHUNK
    shift = CHUNK.bit_length() - 1
    dh = DN_HEAD_DIM
    ii = lax.broadcasted_iota(jnp.int32, (TILE, TILE), 0)
    jj = lax.broadcasted_iota(jnp.int32, (TILE, TILE), 1)
    same = (ii >> shift) == (jj >> shift)
    eye = (ii == jj).astype(F32)
    pair0 = (ii >> 1) == (jj >> 1)
    ins = ((qf_ref, kf_ref, vf_ref, gf_ref, of_ref), (qb_ref, kb_ref, vb_ref, gb_ref, ob_ref))
    chains = [(d, h) for d in range(2) for h in range(DN_HEADS)]
    incls = (jnp.logical_and(same, ii >= jj), jnp.logical_and(same, ii <= jj))
    stricts = (jnp.logical_and(same, ii > jj), jnp.logical_and(same, ii < jj))
    as_bf16 = lambda mask: jnp.where(mask, 1.0, 0.0).astype(BF16)
    same16 = as_bf16(same)
    n_g = 4 * DN_HEADS
    zpad = jnp.zeros((BA_COLS - n_g, TILE), F32)

    gls = []
    for d in range(2):
        incl, strict = incls[d], stricts[d]
        gb = ins[d][3][...]
        g_t = gb.T[0:n_g]
        p0 = g_t.astype(BF16)
        r1 = g_t - p0.astype(F32)
        p1 = r1.astype(BF16)
        p2 = (r1 - p1.astype(F32)).astype(BF16)
        terms = jnp.concatenate([p0, p1, p2], axis=0)

        def masked_sum(mask16):
            r = _dot(terms, mask16)
            return r[0:n_g] + r[n_g:2 * n_g] + r[2 * n_g:3 * n_g]

        gcum_t = masked_sum(as_bf16(incls[1 - d]))
        gtot_t = masked_sum(same16)
        gcum = jnp.concatenate([gcum_t, zpad], axis=0).T
        gtot = jnp.concatenate([gtot_t, zpad], axis=0).T
        for h in range(DN_HEADS):
            ch = d * DN_HEADS + h
            cg = 2 * DN_HEADS + ch
            lanes = slice(h * dh, (h + 1) * dh)
            q16 = ins[d][0][:, lanes]
            k16 = ins[d][1][:, lanes]
            q = q16.astype(F32)
            k = k16.astype(F32)
            v = ins[d][2][:, lanes].astype(F32)
            beta = gb[:, ch:ch + 1]
            gc = gcum[:, cg:cg + 1]
            gr = gcum_t[cg:cg + 1, :]
            gl = gtot[:, cg:cg + 1]
            gls.append(gl)
            decay = jnp.where(incl, jnp.exp(jnp.where(incl, gc - gr, 0.0)), 0.0)
            kb = k * beta
            a = jnp.where(strict, _dot_nt(kb.astype(BF16), k16) * decay, 0.0)
            a16_ref[ch] = a.astype(BF16)
            t0 = eye - jnp.where(pair0, a, 0.0)
            t16_ref[ch] = t0.astype(BF16)
            qk16_ref[ch] = jnp.where(incl, _dot_nt(q16, k16) * decay, 0.0).astype(BF16)
            egc = jnp.exp(gc)
            rhs16_ref[ch] = jnp.concatenate([v * beta, kb * egc], axis=1).astype(BF16)
            kt16_ref[ch] = (k * jnp.exp(gl - gc)).astype(BF16)
            qd16_ref[ch] = (q * egc).astype(BF16)

    lvl = 1
    while (2 << lvl) <= CHUNK:
        sib = jnp.logical_and((ii >> (lvl + 1)) == (jj >> (lvl + 1)), (ii >> lvl) != (jj >> lvl))
        skew = len(chains) // 2
        for n in range(len(chains) + skew):
            if n < len(chains):
                w16_ref[n] = _dot(a16_ref[n], t16_ref[n]).astype(BF16)
            if n >= skew:
                ch = n - skew
                new = jnp.where(sib, -_dot(t16_ref[ch], w16_ref[ch]), 0.0)
                t16_ref[ch] = t16_ref[ch] + new.astype(BF16)
        lvl += 1

    for ch in range(len(chains)):
        uw = _dot(t16_ref[ch], rhs16_ref[ch])
        u_ref[ch] = uw[:, 0:dh]
        uw16_ref[ch] = uw.astype(BF16)

    for ch in range(len(chains)):
        for c in range(n_chunk):
            rows = slice(c * CHUNK, (c + 1) * CHUNK)
            kwb_ref[ch, c] = _dot_tn(kt16_ref[ch, rows, :], uw16_ref[ch, rows, :])

    states = [s_ref[ch] for ch in range(len(chains))]
    for step in range(n_chunk):
        for ch, (d, h) in enumerate(chains):
            c = n_chunk - 1 - step if d else step
            s16 = states[ch].astype(BF16)
            sc16_ref[ch, c] = s16
            egl = jnp.exp(gls[ch][c * CHUNK:c * CHUNK + 1, :])
            states[ch] = (states[ch] * egl - _dot(kwb_ref[ch, c, :, dh:].astype(BF16), s16)
                          + kwb_ref[ch, c, :, 0:dh])
    for ch in range(len(chains)):
        s_ref[ch] = states[ch]

    for ch in range(len(chains)):
        for c in range(n_chunk):
            rows = slice(c * CHUNK, (c + 1) * CHUNK)
            s16 = sc16_ref[ch, c]
            vnew16_ref[ch, rows, :] = (u_ref[ch, rows, :] - _dot(uw16_ref[ch, rows, dh:], s16)).astype(BF16)
            inter_ref[ch, rows, :] = _dot(qd16_ref[ch, rows, :], s16)
    for ch, (d, h) in enumerate(chains):
        ins[d][4][:, h * dh:(h + 1) * dh] = inter_ref[ch] + _dot(qk16_ref[ch], vnew16_ref[ch])


def _dn_scan(q, k, v, gb, batch, n_t):
    m = q.shape[0]
    fwd = lambda b, t: (b * n_t + t, 0)
    bwd = lambda b, t: (b * n_t + jnp.where(t == 0, 0, n_t - t), 0)
    specs = lambda tile: [pl.BlockSpec((TILE, DN_WIDTH), tile)] * 3 + [pl.BlockSpec((TILE, 128), tile)]
    nc = 2 * DN_HEADS
    dh = DN_HEAD_DIM
    big = lambda dt: pltpu.VMEM((nc, TILE, TILE), dt)
    half = lambda dt: pltpu.VMEM((nc, TILE, dh), dt)
    return pl.pallas_call(
        _dn_scan_kernel,
        grid=(batch, n_t),
        in_specs=specs(fwd) + specs(bwd),
        out_specs=(pl.BlockSpec((TILE, DN_WIDTH), fwd), pl.BlockSpec((TILE, DN_WIDTH), bwd)),
        out_shape=(jax.ShapeDtypeStruct((m, DN_WIDTH), F32),) * 2,
        scratch_shapes=[pltpu.VMEM((nc, dh, dh), F32),
                        big(BF16), big(BF16), big(BF16), big(BF16), big(BF16),
                        half(F32), big(BF16), half(BF16), half(BF16),
                        pltpu.VMEM((nc, TILE // CHUNK, dh, 2 * dh), F32),
                        pltpu.VMEM((nc, TILE // CHUNK, dh, dh), BF16),
                        half(BF16), half(F32)],
        compiler_params=_params("parallel", "arbitrary"),
        name="dn_scan",
    )(q, k, v, gb, q, k, v, gb)


MERGE_CHUNK = 256


def _merge_kernel(x_ref, oa_ref, ob_ref, ocf_ref, ocb_ref, z_ref, gate_ref, dng_ref, mod_ref,
                  wa_ref, wb_ref, wc_ref, wo_ref, o_ref):
    d = x_ref.shape[1]
    oa = oa_ref[...]
    ob = ob_ref[...]
    ch = MERGE_CHUNK
    n_chunks = d // ch
    first_ab = _dot(oa, wa_ref[:, 0:ch]), _dot(ob, wb_ref[:, 0:ch])

    oc = ocf_ref[...] + ocb_ref[...]
    z = z_ref[...].astype(F32)
    parts = []
    for h in range(DN_HEADS):
        lanes = slice(h * DN_HEAD_DIM, (h + 1) * DN_HEAD_DIM)
        o = oc[:, lanes]
        ms = jnp.mean(o * o, axis=-1, keepdims=True)
        parts.append(o * lax.rsqrt(ms + NORM_EPS) * dng_ref[...] * _silu(z[:, lanes]))
    oc_n = jnp.concatenate(parts, axis=1).astype(BF16)

    def branches(c):
        cols = slice(c * ch, (c + 1) * ch)
        return _dot(oa, wa_ref[:, cols]), _dot(ob, wb_ref[:, cols]), _dot(oc_n, wc_ref[:, cols])

    nxt = first_ab + (_dot(oc_n, wc_ref[:, 0:ch]),)
    acc = None
    for c in range(n_chunks):
        pa, pb, pc = nxt
        if c + 1 < n_chunks:
            nxt = branches(c + 1)
        gate = lambda i: _sigmoid(gate_ref[:, i * d + c * ch:i * d + (c + 1) * ch].astype(F32))
        mix = gate(0) * pa + gate(1) * pb + gate(2) * pc
        part = _dot(mix.astype(BF16), wo_ref[c * ch:(c + 1) * ch, :])
        acc = part if acc is None else acc + part
    o_ref[...] = x_ref[...] + mod_ref[0] * acc


def _merge(x, oa, ob, ocf, ocb, y_nat, dn_g, mod_tab, wa, wb, wc, wo):
    m, d = x.shape
    tok = lambda c, cb=0: pl.BlockSpec((TILE, c), lambda s: (s, cb))
    full = lambda a: pl.BlockSpec(a.shape, lambda s: (0,) * a.ndim)
    return pl.pallas_call(
        _merge_kernel,
        grid=(m // TILE,),
        in_specs=[tok(d), tok(DA_COLS), tok(GQ_Q_COLS), tok(DN_WIDTH), tok(DN_WIDTH),
                  tok(DN_WIDTH, NAT_Z_OFF // DN_WIDTH), tok(3 * d, NAT_GATE_OFF),
                  full(dn_g), pl.BlockSpec((1, 1, d), lambda s: (s, 0, 0)),
                  full(wa), full(wb), full(wc), full(wo)],
        out_specs=tok(d),
        out_shape=jax.ShapeDtypeStruct((m, d), F32),
        compiler_params=_params("parallel"),
        name="merge",
    )(x, oa, ob, ocf, ocb, y_nat, y_nat, dn_g, mod_tab, wa, wb, wc, wo)


FFN_CHUNK = 256


def _ffn_kernel(x_ref, prev_ref, next_ref, g_ref, shift_ref, scale_ref, w1_ref, cw_ref, cb_ref, mod_ref, w2_ref,
                o_ref, acc_ref, *, n_t):
    j = pl.program_id(0) % n_t
    has_prev = (j >= 2).astype(F32)
    has_next = jnp.logical_and(j >= 1, j <= n_t - 2).astype(F32)

    def norm(x):
        ms = jnp.mean(x * x, axis=-1, keepdims=True)
        return (x * lax.rsqrt(ms + NORM_EPS) * g_ref[...]) * (1.0 + scale_ref[0]) + shift_ref[0]

    x = x_ref[...]
    h_ext = jnp.concatenate([norm(prev_ref[...]) * has_prev, norm(x), norm(next_ref[...]) * has_next],
                            axis=0).astype(BF16)
    n_ext = TILE + 16
    ch = FFN_CHUNK
    dff = cw_ref.shape[1]
    n_chunks = dff // ch

    def project(c):
        return (_dot(h_ext, w1_ref[:, c * ch:(c + 1) * ch]),
                _dot(h_ext, w1_ref[:, dff + c * ch:dff + (c + 1) * ch]))

    au_next = project(0)
    for c in range(n_chunks):
        a_ext, u_ext = au_next
        if c + 1 < n_chunks:
            au_next = project(c + 1)
        cols = slice(c * ch, (c + 1) * ch)
        a = (pltpu.roll(a_ext, 1, 0)[8:8 + TILE] * cw_ref[0:1, cols] + a_ext[8:8 + TILE] * cw_ref[1:2, cols]
             + pltpu.roll(a_ext, n_ext - 1, 0)[8:8 + TILE] * cw_ref[2:3, cols] + cb_ref[:, cols])
        act = (_silu(a) * u_ext[8:8 + TILE]).astype(BF16)
        part = _dot(act, w2_ref[cols, :])
        if c == 0:
            acc_ref[...] = part
        else:
            acc_ref[...] += part
    o_ref[...] = x + mod_ref[0] * acc_ref[...]


def _ffn(x, g, shift_tab, scale_tab, w1, conv_w, conv_b, mod_tab, w2, n_t, latent_only=False):
    m, d = x.shape
    dff = w2.shape[0]
    n8 = m // 8
    row = lambda s: (s, 0, 0)
    const = lambda s: (0, 0)
    if latent_only:
        out_rows = m // n_t * (n_t - 1)
        out_map = lambda s: (s // n_t * (n_t - 1) + jnp.maximum(s % n_t - 1, 0), 0)
    else:
        out_rows, out_map = m, lambda s: (s, 0)
    return pl.pallas_call(
        functools.partial(_ffn_kernel, n_t=n_t),
        grid=(m // TILE,),
        in_specs=[pl.BlockSpec((TILE, d), lambda s: (s, 0)),
                  pl.BlockSpec((8, d), lambda s: (jnp.maximum(s * (TILE // 8) - 1, 0), 0)),
                  pl.BlockSpec((8, d), lambda s: (jnp.minimum((s + 1) * (TILE // 8), n8 - 1), 0)),
                  pl.BlockSpec((1, d), const),
                  pl.BlockSpec((1, 1, d), row), pl.BlockSpec((1, 1, d), row),
                  pl.BlockSpec((d, 2 * dff), const),
                  pl.BlockSpec((3, dff), const), pl.BlockSpec((1, dff), const),
                  pl.BlockSpec((1, 1, d), row),
                  pl.BlockSpec((dff, d), const)],
        out_specs=pl.BlockSpec((TILE, d), out_map),
        out_shape=jax.ShapeDtypeStruct((out_rows, d), F32),
        scratch_shapes=[pltpu.VMEM((TILE, d), F32)],
        compiler_params=_params("arbitrary"),
        name="ffn",
    )(x, x, x, g.reshape(1, d), shift_tab, scale_tab, w1, conv_w, conv_b.reshape(1, dff), mod_tab, w2)


def _rope_tables(n_lat):
    t = jnp.arange(n_lat, dtype=jnp.int32)
    row = (t // GRID_W).astype(F32)
    col = (t % GRID_W).astype(F32)
    d_axis = 32
    inv_freq = ROPE_THETA ** (-jnp.arange(0, d_axis, 2, dtype=F32) / d_axis)
    ang_r = row[None, :] * inv_freq[:, None]
    ang_c = col[None, :] * inv_freq[:, None]
    cr, sr, cc, sc = jnp.cos(ang_r), jnp.sin(ang_r), jnp.cos(ang_c), jnp.sin(ang_c)
    cos_lat = jnp.concatenate([cr, cr, cc, cc], axis=0)
    sin_lat = jnp.concatenate([-sr, sr, -sc, sc], axis=0)
    cos_t = jnp.concatenate([jnp.ones((64, TILE), F32), cos_lat], axis=1)
    sin_t = jnp.concatenate([jnp.zeros((64, TILE), F32), sin_lat], axis=1)
    return cos_t, sin_t


def kernel(x, c, ctx, c_ctx, w_mod, b_mod, norm1_g, w_in, da_qn_g, da_kn_g, da_lambda, da_subln_g, gq_qn_g, gq_kn_g, dn_conv_w, dn_a_log, dn_dt_bias, dn_norm_g, w_br_a, w_br_b, w_br_c, w_o, norm2_g, ffn_w1, ffn_conv_w, ffn_conv_b, ffn_w2):
    batch, n_lat, d = x.shape
    depth = w_mod.shape[0]
    dff = ffn_w2.shape[1]
    assert ctx.shape[1] == TILE and n_lat % TILE == 0 and n_lat % GRID_W == 0 and batch + 1 <= 8
    n_t = 1 + n_lat // TILE
    assert n_t % 2 == 1, "the attention key loop handles key tiles in pairs plus one"
    s_tiles = batch * n_t
    m = s_tiles * TILE

    xs = jnp.concatenate([ctx, x], axis=1).reshape(m, d)

    cond = jnp.zeros((8, d), F32).at[:batch].set(c).at[batch].set(c_ctx)
    mod = _modulation(cond, w_mod, b_mod)
    tile_id = jnp.arange(s_tiles)
    row_of_tile = jnp.where(tile_id % n_t == 0, batch, tile_id // n_t)
    mod_tabs = mod[:, row_of_tile, :].reshape(depth, s_tiles, 6, 1, d)

    cos_t, sin_t = _rope_tables(n_lat)
    cos_m, sin_m = jnp.tile(cos_t, (1, batch)), jnp.tile(sin_t, (1, batch))
    bcast = lambda g: jnp.broadcast_to(g[:, None], (g.shape[0], TILE))

    o_qkv = ATTN_ROWS
    o_b = o_qkv + 3 * DN_WIDTH
    o_z = o_b + 4 * DN_HEADS
    o_g = o_z + DN_WIDTH
    wt_attn = jnp.swapaxes(w_in[:, :, :ATTN_ROWS], 1, 2).astype(BF16)
    w_nat = jnp.concatenate([w_in[:, :, o_g:], w_in[:, :, o_z:o_g]], axis=2).astype(BF16)
    w_qkv = w_in[:, :, o_qkv:o_b].astype(BF16)
    w_ba = jnp.concatenate([w_in[:, :, o_b:o_z], jnp.zeros((depth, d, BA_COLS - 4 * DN_HEADS), F32)],
                           axis=2).astype(BF16)
    pad_tab = lambda a: jnp.zeros((1, BA_COLS), F32).at[0, 2 * DN_HEADS:4 * DN_HEADS].set(a.reshape(-1))

    for l in range(depth):
        lam_init = 0.8 - 0.6 * math.exp(-0.3 * l)
        tab = lambda i: mod_tabs[l, :, i]

        h1 = _norm_mod(xs, norm1_g[l], tab(0), tab(1))
        y_nat = _matmul_nn(h1, w_nat[l], NAT_COLS, BF16)

        gains = jnp.stack([bcast(da_qn_g[l]), bcast(da_kn_g[l]), bcast(gq_qn_g[l]), bcast(gq_kn_g[l])])
        qda, kda, vda, qgq, kgq, vgq = _inproj_attn(wt_attn[l], h1, cos_m, sin_m, gains)
        oa = _flash_da(qda, kda, vda, da_lambda[l], bcast(da_subln_g[l]), batch, n_t, lam_init)
        ob = _flash_gq(qgq, kgq, vgq, batch, n_t)

        dq, dk, dv, gb = _dn_proj(h1, w_qkv[l], w_ba[l], dn_conv_w[l], pad_tab(dn_a_log[l]),
                                  pad_tab(dn_dt_bias[l]), n_t)
        ocf, ocb = _dn_scan(dq, dk, dv, gb, batch, n_t)

        xs = _merge(xs, oa, ob, ocf, ocb, y_nat, dn_norm_g[l].reshape(1, -1), tab(2),
                    w_br_a[l].astype(BF16), w_br_b[l].astype(BF16), w_br_c[l].astype(BF16), w_o[l].astype(BF16))

        xs = _ffn(xs, norm2_g[l], tab(3), tab(4), ffn_w1[l].astype(BF16), ffn_conv_w[l], ffn_conv_b[l], tab(5),
                  ffn_w2[l].astype(BF16), n_t, latent_only=(l == depth - 1))

    return xs.reshape(batch, n_lat, d)
```

```python
import functools
import math

import jax
import jax.numpy as jnp
from jax import lax
from jax.experimental import pallas as pl
from jax.experimental.pallas import tpu as pltpu

F32 = jnp.float32
BF16 = jnp.bfloat16

TILE = 256
CHUNK = 64
GRID_W = 64
ROPE_THETA = 10000.0
NORM_EPS = 1e-6
DA_HEADS, DA_HEAD_DIM = 4, 64
GQ_HEADS, GQ_KV_HEADS, GQ_HEAD_DIM = 8, 2, 64
DN_HEADS, DN_HEAD_DIM = 4, 128
LOG2E = 1.4426950408889634
NEG_BIG = -1e30
ONES_ROWS = 16
VMEM_LIMIT = 48 * 1024 * 1024

DA_COLS = DA_HEADS * 2 * DA_HEAD_DIM
GQ_Q_COLS = GQ_HEADS * GQ_HEAD_DIM
GQ_KV_COLS = GQ_KV_HEADS * GQ_HEAD_DIM
DN_WIDTH = DN_HEADS * DN_HEAD_DIM
ATTN_ROWS = 3 * DA_COLS + GQ_Q_COLS + 2 * GQ_KV_COLS
NAT_GATE_OFF, NAT_Z_OFF = 0, 3072
NAT_COLS = 3584
BA_COLS = 128


def _params(*sem):
    return pltpu.CompilerParams(dimension_semantics=sem, vmem_limit_bytes=VMEM_LIMIT)


def _sigmoid(x):
    return 1.0 / (1.0 + jnp.exp(-x))


def _silu(x):
    return x * _sigmoid(x)


def _dot(a, b):
    return jnp.dot(a, b, preferred_element_type=F32)


def _dot_nt(a, b):
    return lax.dot_general(a, b, (((1,), (1,)), ((), ())), preferred_element_type=F32)


def _dot_tn(a, b):
    return lax.dot_general(a, b, (((0,), (0,)), ((), ())), preferred_element_type=F32)


def _mod_kernel(c_ref, w_ref, b_ref, o_ref):
    cond = _silu(c_ref[...])
    o_ref[0] = jnp.dot(cond, w_ref[0], preferred_element_type=F32,
                       precision=lax.Precision.HIGHEST) + b_ref[0]


def _modulation(cond, w_mod, b_mod):
    depth, d, n = w_mod.shape
    tn = 1536
    return pl.pallas_call(
        _mod_kernel,
        grid=(depth, n // tn),
        in_specs=[pl.BlockSpec((8, d), lambda l, j: (0, 0)),
                  pl.BlockSpec((1, d, tn), lambda l, j: (l, 0, j)),
                  pl.BlockSpec((1, 1, tn), lambda l, j: (l, 0, j))],
        out_specs=pl.BlockSpec((1, 8, tn), lambda l, j: (l, 0, j)),
        out_shape=jax.ShapeDtypeStruct((depth, 8, n), F32),
        compiler_params=_params("parallel", "parallel"),
        name="modulation",
    )(cond, w_mod, b_mod.reshape(depth, 1, n))


def _norm_mod_kernel(x_ref, g_ref, shift_ref, scale_ref, o_ref):
    for i in range(x_ref.shape[0] // TILE):
        rows = slice(i * TILE, (i + 1) * TILE)
        x = x_ref[rows, :]
        ms = jnp.mean(x * x, axis=-1, keepdims=True)
        y = x * lax.rsqrt(ms + NORM_EPS) * g_ref[...]
        o_ref[rows, :] = (y * (1.0 + scale_ref[i]) + shift_ref[i]).astype(o_ref.dtype)


def _norm_mod(x, g, shift_tab, scale_tab):
    m, d = x.shape
    tm = _row_tile(m)
    sub = tm // TILE
    return pl.pallas_call(
        _norm_mod_kernel,
        grid=(m // tm,),
        in_specs=[pl.BlockSpec((tm, d), lambda s: (s, 0)),
                  pl.BlockSpec((1, d), lambda s: (0, 0)),
                  pl.BlockSpec((sub, 1, d), lambda s: (s, 0, 0)),
                  pl.BlockSpec((sub, 1, d), lambda s: (s, 0, 0))],
        out_specs=pl.BlockSpec((tm, d), lambda s: (s, 0)),
        out_shape=jax.ShapeDtypeStruct((m, d), BF16),
        compiler_params=_params("parallel"),
        name="norm_mod",
    )(x, g.reshape(1, d), shift_tab, scale_tab)


def _mm_nn_kernel(a_ref, b_ref, o_ref):
    o_ref[...] = _dot(a_ref[...], b_ref[...]).astype(o_ref.dtype)


def _row_tile(m):
    for t in (1024, 512, 256):
        if m % t == 0:
            return t
    raise ValueError(m)


def _matmul_nn(a, b, tn, out_dtype):
    m, k = a.shape
    n = b.shape[1]
    tm = _row_tile(m)
    return pl.pallas_call(
        _mm_nn_kernel,
        grid=(m // tm, n // tn),
        in_specs=[pl.BlockSpec((tm, k), lambda i, j: (i, 0)),
                  pl.BlockSpec((k, tn), lambda i, j: (0, j))],
        out_specs=pl.BlockSpec((tm, tn), lambda i, j: (i, j)),
        out_shape=jax.ShapeDtypeStruct((m, n), out_dtype),
        compiler_params=_params("parallel", "parallel"),
        name="matmul_nn",
    )(a, b)


def _norm_rope_t(x, gain, cos, sin, scale):
    x = x.astype(F32)
    ms = jnp.mean(x * x, axis=0, keepdims=True)
    xn = x * lax.rsqrt(ms + NORM_EPS) * gain
    sw = jnp.concatenate([xn[16:32], xn[0:16], xn[48:64], xn[32:48]], axis=0)
    out = xn * cos + sw * sin
    return out * scale if scale != 1.0 else out


_ATTN_SECTIONS = ((0, DA_COLS), (DA_COLS, DA_COLS), (2 * DA_COLS, DA_COLS), (3 * DA_COLS, GQ_Q_COLS),
                  (3 * DA_COLS + GQ_Q_COLS, 2 * GQ_KV_COLS))


def _attn_prep_section(sec, y, i, cos, sin, gain_ref, qda_ref, kda_ref, vda_ref, qgq_ref, kgq_ref, vgq_ref):
    ones = jnp.ones((ONES_ROWS, TILE), F32)
    hd = DA_HEAD_DIM
    gd = GQ_HEAD_DIM
    if sec in (0, 1):
        scale = (hd ** -0.5) * LOG2E if sec == 0 else 1.0
        for h in range(DA_HEADS):
            halves = [_norm_rope_t(y[(2 * h + half) * hd:(2 * h + half + 1) * hd], gain_ref[sec], cos, sin, scale)
                      for half in range(2)]
            both = jnp.concatenate(halves, axis=0)
            if sec == 0:
                qda_ref[i, h] = both.astype(BF16)
            else:
                kda_ref[i, h] = both.T.astype(BF16)
    elif sec == 2:
        for h in range(DA_HEADS):
            vda_ref[i, h] = jnp.concatenate([y[2 * h * hd:2 * (h + 1) * hd], ones], axis=0).astype(BF16)
    elif sec == 3:
        for h in range(GQ_HEADS):
            qgq_ref[i, h] = _norm_rope_t(y[h * gd:(h + 1) * gd], gain_ref[2], cos, sin,
                                         (gd ** -0.5) * LOG2E).astype(BF16)
    else:
        ks = [_norm_rope_t(y[h * gd:(h + 1) * gd], gain_ref[3], cos, sin, 1.0) for h in range(GQ_KV_HEADS)]
        kgq_ref[i] = jnp.concatenate(ks, axis=0).T.astype(BF16)
        for h in range(GQ_KV_HEADS):
            r = GQ_KV_COLS + h * gd
            vgq_ref[i, h] = jnp.concatenate([y[r:r + gd], ones], axis=0).astype(BF16)


def _inproj_attn_kernel(wt_ref, h_ref, cos_ref, sin_ref, gain_ref,
                        qda_ref, kda_ref, vda_ref, qgq_ref, kgq_ref, vgq_ref):
    n_sub = h_ref.shape[0] // TILE
    pieces = [(i, sec) for i in range(n_sub) for sec in range(len(_ATTN_SECTIONS))]

    def project(piece):
        i, sec = piece
        r0, rows = _ATTN_SECTIONS[sec]
        return _dot_nt(wt_ref[r0:r0 + rows, :], h_ref[i * TILE:(i + 1) * TILE, :])

    y_next = project(pieces[0])
    for n, (i, sec) in enumerate(pieces):
        y = y_next
        if n + 1 < len(pieces):
            y_next = project(pieces[n + 1])
        lanes = slice(i * TILE, (i + 1) * TILE)
        _attn_prep_section(sec, y, i, cos_ref[:, lanes], sin_ref[:, lanes], gain_ref,
                           qda_ref, kda_ref, vda_ref, qgq_ref, kgq_ref, vgq_ref)


def _inproj_attn(wt, h, cos_m, sin_m, gains):
    m, k = h.shape
    tm = _row_tile(m)
    sub = tm // TILE
    s_tiles = m // TILE
    out_shape = (
        jax.ShapeDtypeStruct((s_tiles, DA_HEADS, 2 * DA_HEAD_DIM, TILE), BF16),
        jax.ShapeDtypeStruct((s_tiles, DA_HEADS, TILE, 2 * DA_HEAD_DIM), BF16),
        jax.ShapeDtypeStruct((s_tiles, DA_HEADS, 2 * DA_HEAD_DIM + ONES_ROWS, TILE), BF16),
        jax.ShapeDtypeStruct((s_tiles, GQ_HEADS, GQ_HEAD_DIM, TILE), BF16),
        jax.ShapeDtypeStruct((s_tiles, TILE, GQ_KV_COLS), BF16),
        jax.ShapeDtypeStruct((s_tiles, GQ_KV_HEADS, GQ_HEAD_DIM + ONES_ROWS, TILE), BF16),
    )
    blk = lambda sds: pl.BlockSpec((sub,) + sds.shape[1:], lambda s: (s,) + (0,) * (len(sds.shape) - 1))
    return pl.pallas_call(
        _inproj_attn_kernel,
        grid=(m // tm,),
        in_specs=[pl.BlockSpec((ATTN_ROWS, k), lambda s: (0, 0)),
                  pl.BlockSpec((tm, k), lambda s: (s, 0)),
                  pl.BlockSpec((GQ_HEAD_DIM, tm), lambda s: (0, s)),
                  pl.BlockSpec((GQ_HEAD_DIM, tm), lambda s: (0, s)),
                  pl.BlockSpec((4, GQ_HEAD_DIM, TILE), lambda s: (0, 0, 0))],
        out_specs=tuple(blk(s) for s in out_shape),
        out_shape=out_shape,
        compiler_params=_params("parallel"),
        name="inproj_attn",
    )(wt, h, cos_m, sin_m, gains)


def _softmax_update(s_ref, idx, m_tile, m_old):
    m_new = jnp.maximum(m_old, m_tile)
    alpha = jnp.exp2(m_old - m_new)
    p = jnp.exp2(s_ref[idx] - m_new)
    return p.astype(BF16), alpha, m_new


def _attention_program(n_t, n_mat, score_one, consume_one, finalize):
    neg = (jnp.full((1, TILE), NEG_BIG, F32),) * n_mat

    def overlapped(q_next, j_next, buf_next, j_cur, buf_cur, mts, ms):
        nxt, new = [], []
        for c in range(n_mat):
            nxt.append(score_one(q_next, j_next, buf_next, c))
            new.append(consume_one(j_cur, buf_cur, c, mts[c], ms[c]))
        return tuple(nxt), tuple(new)

    mts = [score_one(0, 0, 0, c) for c in range(n_mat)]
    for c in range(n_mat):
        consume_one(0, 0, c, mts[c], neg[c])
    finalize(0)
    half = (n_t - 1) // 2
    pairs_per_iter = next(u for u in (4, 2, 1) if half % u == 0)
    mt0 = tuple(score_one(1, 0, 0, c) for c in range(n_mat))

    def query_pair(qp, mt):
        for par in (0, 1):
            qi = 1 + 2 * qp + par
            a, b = (0, 1) if par == 0 else (1, 0)

            def key_pairs(i, carry, qi=qi, a=a, b=b):
                mt_a, ms = carry[:n_mat], carry[n_mat:]
                for u in range(pairs_per_iter):
                    j = 2 * (pairs_per_iter * i + u)
                    mt_b, ms = overlapped(qi, j + 1, b, j, a, mt_a, ms)
                    mt_a, ms = overlapped(qi, j + 2, a, j + 1, b, mt_b, ms)
                return tuple(mt_a) + tuple(ms)

            carry = lax.fori_loop(0, half // pairs_per_iter, key_pairs, tuple(mt) + neg)
            mt, _ = overlapped(jnp.minimum(qi + 1, n_t - 1), 0, b, n_t - 1, a, carry[:n_mat], carry[n_mat:])
            finalize(qi)
        return tuple(mt)

    lax.fori_loop(0, half, query_pair, tuple(mt0))


DA_HEADS_PER_STEP = 2


def _flash_da_kernel(q_ref, k_ref, v_ref, lam_ref, g_ref, o_ref, qpad_ref, acc_ref, sa_ref, sb_ref,
                     *, n_t, lam_init):
    dv = 2 * DA_HEAD_DIM
    hps = DA_HEADS_PER_STEP
    bufs = (sa_ref, sb_ref)
    acc_ref[...] = jnp.zeros_like(acc_ref)
    top = lax.broadcasted_iota(jnp.int32, (dv, TILE), 0) < DA_HEAD_DIM

    def pad_queries(qi, carry):
        for hh in range(hps):
            q = q_ref[qi, hh]
            zero = jnp.zeros_like(q)
            qpad_ref[qi, 2 * hh] = jnp.where(top, q, zero)
            qpad_ref[qi, 2 * hh + 1] = jnp.where(top, zero, q)
        return carry

    lax.fori_loop(0, n_t, pad_queries, 0)
    lp = lam_ref[...]
    lam = (jnp.exp(jnp.sum(lp[0:1] * lp[1:2], keepdims=True))
           - jnp.exp(jnp.sum(lp[2:3] * lp[3:4], keepdims=True)) + lam_init)

    def score_one(qi, j, buf, c):
        s = _dot(k_ref[j, c // 2], qpad_ref[qi, c])
        bufs[buf][c] = s
        return jnp.max(s, axis=0, keepdims=True)

    def consume_one(j, buf, c, mt, m_old):
        p, alpha, m_new = _softmax_update(bufs[buf], c, mt, m_old)
        acc_ref[c] = acc_ref[c] * alpha + _dot(v_ref[j, c // 2], p)
        return m_new

    def finalize(qi):
        rows = pl.ds(pl.multiple_of(qi * TILE, TILE), TILE)
        for hh in range(hps):
            a1, a2 = acc_ref[2 * hh], acc_ref[2 * hh + 1]
            o = a1[0:dv] / a1[dv:dv + 1] - lam * (a2[0:dv] / a2[dv:dv + 1])
            ms = jnp.mean(o * o, axis=0, keepdims=True)
            o = o * lax.rsqrt(ms + NORM_EPS) * g_ref[...] * (1.0 - lam_init)
            o_ref[rows, hh * dv:(hh + 1) * dv] = o.T.astype(o_ref.dtype)
        acc_ref[...] = jnp.zeros_like(acc_ref)

    _attention_program(n_t, 2 * hps, score_one, consume_one, finalize)


def _flash_da(qda, kda, vda, lam_p, subln_g, batch, n_t, lam_init):
    dv = 2 * DA_HEAD_DIM
    hps = DA_HEADS_PER_STEP
    m = batch * n_t * TILE
    kern = functools.partial(_flash_da_kernel, n_t=n_t, lam_init=lam_init)
    return pl.pallas_call(
        kern,
        grid=(batch, DA_HEADS // hps),
        in_specs=[pl.BlockSpec((n_t, hps, dv, TILE), lambda b, h: (b, h, 0, 0)),
                  pl.BlockSpec((n_t, hps, TILE, dv), lambda b, h: (b, h, 0, 0)),
                  pl.BlockSpec((n_t, hps, dv + ONES_ROWS, TILE), lambda b, h: (b, h, 0, 0)),
                  pl.BlockSpec((4, DA_HEAD_DIM), lambda b, h: (0, 0)),
                  pl.BlockSpec((dv, TILE), lambda b, h: (0, 0))],
        out_specs=pl.BlockSpec((n_t * TILE, hps * dv), lambda b, h: (b, h)),
        out_shape=jax.ShapeDtypeStruct((m, DA_HEADS * dv), BF16),
        scratch_shapes=[pltpu.VMEM((n_t, 2 * hps, dv, TILE), BF16),
                        pltpu.VMEM((2 * hps, dv + ONES_ROWS, TILE), F32),
                        pltpu.VMEM((2 * hps, TILE, TILE), F32), pltpu.VMEM((2 * hps, TILE, TILE), F32)],
        compiler_params=_params("parallel", "parallel"),
        name="flash_da",
    )(qda, kda, vda, lam_p, subln_g)


def _flash_gq_kernel(q_ref, k_ref, v_ref, o_ref, qpad_ref, acc_ref, sa_ref, sb_ref, *, n_t):
    first = pl.program_id(1) == 0
    group = GQ_HEADS // GQ_KV_HEADS
    dv = GQ_HEAD_DIM
    bufs = (sa_ref, sb_ref)
    acc_ref[...] = jnp.zeros_like(acc_ref)

    def pad_queries(qi, carry):
        for g in range(group):
            q = q_ref[qi, g]
            zero = jnp.zeros_like(q)
            qpad_ref[qi, g] = jnp.concatenate([jnp.where(first, q, zero), jnp.where(first, zero, q)], axis=0)
        return carry

    lax.fori_loop(0, n_t, pad_queries, 0)

    def score_one(qi, j, buf, g):
        s = _dot(k_ref[j], qpad_ref[qi, g])
        bufs[buf][g] = s
        return jnp.max(s, axis=0, keepdims=True)

    def consume_one(j, buf, g, mt, m_old):
        p, alpha, m_new = _softmax_update(bufs[buf], g, mt, m_old)
        acc_ref[g] = acc_ref[g] * alpha + _dot(v_ref[j, 0], p)
        return m_new

    def finalize(qi):
        rows = pl.ds(pl.multiple_of(qi * TILE, TILE), TILE)
        o = jnp.concatenate([acc_ref[g, 0:dv, :] / acc_ref[g, dv:dv + 1, :] for g in range(group)], axis=0)
        o_ref[rows, :] = o.T.astype(o_ref.dtype)
        acc_ref[...] = jnp.zeros_like(acc_ref)

    _attention_program(n_t, group, score_one, consume_one, finalize)


def _flash_gq(qgq, kgq, vgq, batch, n_t):
    group = GQ_HEADS // GQ_KV_HEADS
    m = batch * n_t * TILE
    kern = functools.partial(_flash_gq_kernel, n_t=n_t)
    return pl.pallas_call(
        kern,
        grid=(batch, GQ_KV_HEADS),
        in_specs=[pl.BlockSpec((n_t, group, GQ_HEAD_DIM, TILE), lambda b, h: (b, h, 0, 0)),
                  pl.BlockSpec((n_t, TILE, GQ_KV_COLS), lambda b, h: (b, 0, 0)),
                  pl.BlockSpec((n_t, 1, GQ_HEAD_DIM + ONES_ROWS, TILE), lambda b, h: (b, h, 0, 0))],
        out_specs=pl.BlockSpec((n_t * TILE, group * GQ_HEAD_DIM), lambda b, h: (b, h)),
        out_shape=jax.ShapeDtypeStruct((m, GQ_Q_COLS), BF16),
        scratch_shapes=[pltpu.VMEM((n_t, group, 2 * GQ_HEAD_DIM, TILE), BF16),
                        pltpu.VMEM((group, GQ_HEAD_DIM + ONES_ROWS, TILE), F32),
                        pltpu.VMEM((group, TILE, TILE), F32), pltpu.VMEM((group, TILE, TILE), F32)],
        compiler_params=_params("parallel", "parallel"),
        name="flash_gq",
    )(qgq, kgq, vgq)


DN_CHUNK_COLS = 256
HALO = 16


def _dn_proj_kernel(h_ref, prev_ref, next_ref, w_ref, wba_ref, cw_ref, alog_ref, dtb_ref,
                    q_ref, k_ref, v_ref, gb_ref, *, n_t):
    j = pl.program_id(0) % n_t
    has_prev = (j >= 2).astype(BF16)
    has_next = jnp.logical_and(j >= 1, j <= n_t - 2).astype(BF16)
    h = h_ref[...]
    h_ext = jnp.concatenate([prev_ref[...] * has_prev, h, next_ref[...] * has_next], axis=0)
    n_ext = TILE + 2 * HALO
    rows = slice(HALO, HALO + TILE)
    ch = DN_CHUNK_COLS
    n_chunks = 3 * DN_WIDTH // ch
    per_part = DN_WIDTH // ch
    project = lambda c: _dot(h_ext, w_ref[:, c * ch:(c + 1) * ch])
    y_next = project(0)
    for c in range(n_chunks):
        y = y_next
        y_next = project(c + 1) if c + 1 < n_chunks else _dot(h, wba_ref[...])
        cols = slice(c * ch, (c + 1) * ch)
        a = (pltpu.roll(y, 1, 0)[rows] * cw_ref[0:1, cols] + y[rows] * cw_ref[1:2, cols]
             + pltpu.roll(y, n_ext - 1, 0)[rows] * cw_ref[2:3, cols])
        act = _silu(a)
        part, sub = divmod(c, per_part)
        if part == 2:
            v_ref[:, sub * ch:(sub + 1) * ch] = act.astype(v_ref.dtype)
            continue
        out_ref, scale = (q_ref, DN_HEAD_DIM ** -0.5) if part == 0 else (k_ref, 1.0)
        for hh in range(ch // DN_HEAD_DIM):
            x = act[:, hh * DN_HEAD_DIM:(hh + 1) * DN_HEAD_DIM]
            xn = x * (lax.rsqrt(jnp.sum(x * x, axis=-1, keepdims=True) + NORM_EPS) * scale)
            c0 = sub * ch + hh * DN_HEAD_DIM
            out_ref[:, c0:c0 + DN_HEAD_DIM] = xn.astype(out_ref.dtype)
    ba = y_next
    beta = _sigmoid(ba)
    z = ba + dtb_ref[...]
    softplus = jnp.maximum(z, 0.0) + jnp.log(1.0 + jnp.exp(-jnp.abs(z)))
    g = -jnp.exp(alog_ref[...]) * softplus
    col = lax.broadcasted_iota(jnp.int32, ba.shape, 1)
    gb_ref[...] = jnp.where(col < 2 * DN_HEADS, beta, g)


def _dn_proj(h, w_qkv, w_ba, conv_w, alog_tab, dtb_tab, n_t):
    m, d = h.shape
    w3 = 3 * DN_WIDTH
    n_blk = m // HALO
    per_tile = TILE // HALO
    tok = lambda c: pl.BlockSpec((TILE, c), lambda s: (s, 0))
    const = lambda s: (0, 0)
    return pl.pallas_call(
        functools.partial(_dn_proj_kernel, n_t=n_t),
        grid=(m // TILE,),
        in_specs=[tok(d),
                  pl.BlockSpec((HALO, d), lambda s: (jnp.maximum(s * per_tile - 1, 0), 0)),
                  pl.BlockSpec((HALO, d), lambda s: (jnp.minimum((s + 1) * per_tile, n_blk - 1), 0)),
                  pl.BlockSpec((d, w3), const), pl.BlockSpec((d, BA_COLS), const),
                  pl.BlockSpec((3, w3), const),
                  pl.BlockSpec((1, BA_COLS), const), pl.BlockSpec((1, BA_COLS), const)],
        out_specs=(tok(DN_WIDTH), tok(DN_WIDTH), tok(DN_WIDTH), tok(BA_COLS)),
        out_shape=(jax.ShapeDtypeStruct((m, DN_WIDTH), BF16),) * 3 + (jax.ShapeDtypeStruct((m, BA_COLS), F32),),
        compiler_params=_params("parallel"),
        name="dn_proj",
    )(h, h, h, w_qkv, w_ba, conv_w, alog_tab, dtb_tab)


def _dn_scan_kernel(qf_ref, kf_ref, vf_ref, gf_ref, qb_ref, kb_ref, vb_ref, gb_ref, of_ref, ob_ref,
                    s_ref, a16_ref, t16_ref, w16_ref, qk16_ref, rhs16_ref, u_ref, uw16_ref,
                    kt16_ref, qd16_ref, kwb_ref, sc16_ref, vnew16_ref, inter_ref):
    t = pl.program_id(1)

    @pl.when(t == 0)
    def _():
        s_ref[...] = jnp.zeros_like(s_ref)

    n_chunk = TILE // CHUNK
    shift = CHUNK.bit_length() - 1
    dh = DN_HEAD_DIM
    ii = lax.broadcasted_iota(jnp.int32, (TILE, TILE), 0)
    jj = lax.broadcasted_iota(jnp.int32, (TILE, TILE), 1)
    same = (ii >> shift) == (jj >> shift)
    eye = (ii == jj).astype(F32)
    pair0 = (ii >> 1) == (jj >> 1)
    ins = ((qf_ref, kf_ref, vf_ref, gf_ref, of_ref), (qb_ref, kb_ref, vb_ref, gb_ref, ob_ref))
    chains = [(d, h) for d in range(2) for h in range(DN_HEADS)]
    incls = (jnp.logical_and(same, ii >= jj), jnp.logical_and(same, ii <= jj))
    stricts = (jnp.logical_and(same, ii > jj), jnp.logical_and(same, ii < jj))
    as_bf16 = lambda mask: jnp.where(mask, 1.0, 0.0).astype(BF16)
    same16 = as_bf16(same)
    n_g = 4 * DN_HEADS
    zpad = jnp.zeros((BA_COLS - n_g, TILE), F32)

    gls = []
    for d in range(2):
        incl, strict = incls[d], stricts[d]
        gb = ins[d][3][...]
        g_t = gb.T[0:n_g]
        p0 = g_t.astype(BF16)
        r1 = g_t - p0.astype(F32)
        p1 = r1.astype(BF16)
        p2 = (r1 - p1.astype(F32)).astype(BF16)
        terms = jnp.concatenate([p0, p1, p2], axis=0)

        def masked_sum(mask16):
            r = _dot(terms, mask16)
            return r[0:n_g] + r[n_g:2 * n_g] + r[2 * n_g:3 * n_g]

        gcum_t = masked_sum(as_bf16(incls[1 - d]))
        gtot_t = masked_sum(same16)
        gcum = jnp.concatenate([gcum_t, zpad], axis=0).T
        gtot = jnp.concatenate([gtot_t, zpad], axis=0).T
        for h in range(DN_HEADS):
            ch = d * DN_HEADS + h
            cg = 2 * DN_HEADS + ch
            lanes = slice(h * dh, (h + 1) * dh)
            q16 = ins[d][0][:, lanes]
            k16 = ins[d][1][:, lanes]
            q = q16.astype(F32)
            k = k16.astype(F32)
            v = ins[d][2][:, lanes].astype(F32)
            beta = gb[:, ch:ch + 1]
            gc = gcum[:, cg:cg + 1]
            gr = gcum_t[cg:cg + 1, :]
            gl = gtot[:, cg:cg + 1]
            gls.append(gl)
            decay = jnp.where(incl, jnp.exp(jnp.where(incl, gc - gr, 0.0)), 0.0)
            kb = k * beta
            a = jnp.where(strict, _dot_nt(kb.astype(BF16), k16) * decay, 0.0)
            a16_ref[ch] = a.astype(BF16)
            t0 = eye - jnp.where(pair0, a, 0.0)
            t16_ref[ch] = t0.astype(BF16)
            qk16_ref[ch] = jnp.where(incl, _dot_nt(q16, k16) * decay, 0.0).astype(BF16)
            egc = jnp.exp(gc)
            rhs16_ref[ch] = jnp.concatenate([v * beta, kb * egc], axis=1).astype(BF16)
            kt16_ref[ch] = (k * jnp.exp(gl - gc)).astype(BF16)
            qd16_ref[ch] = (q * egc).astype(BF16)

    lvl = 1
    while (2 << lvl) <= CHUNK:
        sib = jnp.logical_and((ii >> (lvl + 1)) == (jj >> (lvl + 1)), (ii >> lvl) != (jj >> lvl))
        skew = len(chains) // 2
        for n in range(len(chains) + skew):
            if n < len(chains):
                w16_ref[n] = _dot(a16_ref[n], t16_ref[n]).astype(BF16)
            if n >= skew:
                ch = n - skew
                new = jnp.where(sib, -_dot(t16_ref[ch], w16_ref[ch]), 0.0)
                t16_ref[ch] = t16_ref[ch] + new.astype(BF16)
        lvl += 1

    for ch in range(len(chains)):
        uw = _dot(t16_ref[ch], rhs16_ref[ch])
        u_ref[ch] = uw[:, 0:dh]
        uw16_ref[ch] = uw.astype(BF16)

    for ch in range(len(chains)):
        for c in range(n_chunk):
            rows = slice(c * CHUNK, (c + 1) * CHUNK)
            kwb_ref[ch, c] = _dot_tn(kt16_ref[ch, rows, :], uw16_ref[ch, rows, :])

    states = [s_ref[ch] for ch in range(len(chains))]
    for step in range(n_chunk):
        for ch, (d, h) in enumerate(chains):
            c = n_chunk - 1 - step if d else step
            s16 = states[ch].astype(BF16)
            sc16_ref[ch, c] = s16
            egl = jnp.exp(gls[ch][c * CHUNK:c * CHUNK + 1, :])
            states[ch] = (states[ch] * egl - _dot(kwb_ref[ch, c, :, dh:].astype(BF16), s16)
                          + kwb_ref[ch, c, :, 0:dh])
    for ch in range(len(chains)):
        s_ref[ch] = states[ch]

    for ch in range(len(chains)):
        for c in range(n_chunk):
            rows = slice(c * CHUNK, (c + 1) * CHUNK)
            s16 = sc16_ref[ch, c]
            vnew16_ref[ch, rows, :] = (u_ref[ch, rows, :] - _dot(uw16_ref[ch, rows, dh:], s16)).astype(BF16)
            inter_ref[ch, rows, :] = _dot(qd16_ref[ch, rows, :], s16)
    for ch, (d, h) in enumerate(chains):
        ins[d][4][:, h * dh:(h + 1) * dh] = inter_ref[ch] + _dot(qk16_ref[ch], vnew16_ref[ch])


def _dn_scan(q, k, v, gb, batch, n_t):
    m = q.shape[0]
    fwd = lambda b, t: (b * n_t + t, 0)
    bwd = lambda b, t: (b * n_t + jnp.where(t == 0, 0, n_t - t), 0)
    specs = lambda tile: [pl.BlockSpec((TILE, DN_WIDTH), tile)] * 3 + [pl.BlockSpec((TILE, 128), tile)]
    nc = 2 * DN_HEADS
    dh = DN_HEAD_DIM
    big = lambda dt: pltpu.VMEM((nc, TILE, TILE), dt)
    half = lambda dt: pltpu.VMEM((nc, TILE, dh), dt)
    return pl.pallas_call(
        _dn_scan_kernel,
        grid=(batch, n_t),
        in_specs=specs(fwd) + specs(bwd),
        out_specs=(pl.BlockSpec((TILE, DN_WIDTH), fwd), pl.BlockSpec((TILE, DN_WIDTH), bwd)),
        out_shape=(jax.ShapeDtypeStruct((m, DN_WIDTH), F32),) * 2,
        scratch_shapes=[pltpu.VMEM((nc, dh, dh), F32),
                        big(BF16), big(BF16), big(BF16), big(BF16), big(BF16),
                        half(F32), big(BF16), half(BF16), half(BF16),
                        pltpu.VMEM((nc, TILE // CHUNK, dh, 2 * dh), F32),
                        pltpu.VMEM((nc, TILE // CHUNK, dh, dh), BF16),
                        half(BF16), half(F32)],
        compiler_params=_params("parallel", "arbitrary"),
        name="dn_scan",
    )(q, k, v, gb, q, k, v, gb)


MERGE_CHUNK = 256


def _merge_kernel(x_ref, oa_ref, ob_ref, ocf_ref, ocb_ref, z_ref, gate_ref, dng_ref, mod_ref,
                  wa_ref, wb_ref, wc_ref, wo_ref, o_ref):
    d = x_ref.shape[1]
    oa = oa_ref[...]
    ob = ob_ref[...]
    ch = MERGE_CHUNK
    n_chunks = d // ch
    first_ab = _dot(oa, wa_ref[:, 0:ch]), _dot(ob, wb_ref[:, 0:ch])

    oc = ocf_ref[...] + ocb_ref[...]
    z = z_ref[...].astype(F32)
    parts = []
    for h in range(DN_HEADS):
        lanes = slice(h * DN_HEAD_DIM, (h + 1) * DN_HEAD_DIM)
        o = oc[:, lanes]
        ms = jnp.mean(o * o, axis=-1, keepdims=True)
        parts.append(o * lax.rsqrt(ms + NORM_EPS) * dng_ref[...] * _silu(z[:, lanes]))
    oc_n = jnp.concatenate(parts, axis=1).astype(BF16)

    def branches(c):
        cols = slice(c * ch, (c + 1) * ch)
        return _dot(oa, wa_ref[:, cols]), _dot(ob, wb_ref[:, cols]), _dot(oc_n, wc_ref[:, cols])

    nxt = first_ab + (_dot(oc_n, wc_ref[:, 0:ch]),)
    acc = None
    for c in range(n_chunks):
        pa, pb, pc = nxt
        if c + 1 < n_chunks:
            nxt = branches(c + 1)
        gate = lambda i: _sigmoid(gate_ref[:, i * d + c * ch:i * d + (c + 1) * ch].astype(F32))
        mix = gate(0) * pa + gate(1) * pb + gate(2) * pc
        part = _dot(mix.astype(BF16), wo_ref[c * ch:(c + 1) * ch, :])
        acc = part if acc is None else acc + part
    o_ref[...] = x_ref[...] + mod_ref[0] * acc


def _merge(x, oa, ob, ocf, ocb, y_nat, dn_g, mod_tab, wa, wb, wc, wo):
    m, d = x.shape
    tok = lambda c, cb=0: pl.BlockSpec((TILE, c), lambda s: (s, cb))
    full = lambda a: pl.BlockSpec(a.shape, lambda s: (0,) * a.ndim)
    return pl.pallas_call(
        _merge_kernel,
        grid=(m // TILE,),
        in_specs=[tok(d), tok(DA_COLS), tok(GQ_Q_COLS), tok(DN_WIDTH), tok(DN_WIDTH),
                  tok(DN_WIDTH, NAT_Z_OFF // DN_WIDTH), tok(3 * d, NAT_GATE_OFF),
                  full(dn_g), pl.BlockSpec((1, 1, d), lambda s: (s, 0, 0)),
                  full(wa), full(wb), full(wc), full(wo)],
        out_specs=tok(d),
        out_shape=jax.ShapeDtypeStruct((m, d), F32),
        compiler_params=_params("parallel"),
        name="merge",
    )(x, oa, ob, ocf, ocb, y_nat, y_nat, dn_g, mod_tab, wa, wb, wc, wo)


FFN_CHUNK = 256


def _ffn_kernel(x_ref, prev_ref, next_ref, g_ref, shift_ref, scale_ref, w1_ref, cw_ref, cb_ref, mod_ref, w2_ref,
                o_ref, acc_ref, *, n_t):
    j = pl.program_id(0) % n_t
    has_prev = (j >= 2).astype(F32)
    has_next = jnp.logical_and(j >= 1, j <= n_t - 2).astype(F32)

    def norm(x):
        ms = jnp.mean(x * x, axis=-1, keepdims=True)
        return (x * lax.rsqrt(ms + NORM_EPS) * g_ref[...]) * (1.0 + scale_ref[0]) + shift_ref[0]

    x = x_ref[...]
    h_ext = jnp.concatenate([norm(prev_ref[...]) * has_prev, norm(x), norm(next_ref[...]) * has_next],
                            axis=0).astype(BF16)
    n_ext = TILE + 16
    ch = FFN_CHUNK
    dff = cw_ref.shape[1]
    n_chunks = dff // ch

    def project(c):
        return (_dot(h_ext, w1_ref[:, c * ch:(c + 1) * ch]),
                _dot(h_ext, w1_ref[:, dff + c * ch:dff + (c + 1) * ch]))

    au_next = project(0)
    for c in range(n_chunks):
        a_ext, u_ext = au_next
        if c + 1 < n_chunks:
            au_next = project(c + 1)
        cols = slice(c * ch, (c + 1) * ch)
        a = (pltpu.roll(a_ext, 1, 0)[8:8 + TILE] * cw_ref[0:1, cols] + a_ext[8:8 + TILE] * cw_ref[1:2, cols]
             + pltpu.roll(a_ext, n_ext - 1, 0)[8:8 + TILE] * cw_ref[2:3, cols] + cb_ref[:, cols])
        act = (_silu(a) * u_ext[8:8 + TILE]).astype(BF16)
        part = _dot(act, w2_ref[cols, :])
        if c == 0:
            acc_ref[...] = part
        else:
            acc_ref[...] += part
    o_ref[...] = x + mod_ref[0] * acc_ref[...]


def _ffn(x, g, shift_tab, scale_tab, w1, conv_w, conv_b, mod_tab, w2, n_t, latent_only=False):
    m, d = x.shape
    dff = w2.shape[0]
    n8 = m // 8
    row = lambda s: (s, 0, 0)
    const = lambda s: (0, 0)
    if latent_only:
        out_rows = m // n_t * (n_t - 1)
        out_map = lambda s: (s // n_t * (n_t - 1) + jnp.maximum(s % n_t - 1, 0), 0)
    else:
        out_rows, out_map = m, lambda s: (s, 0)
    return pl.pallas_call(
        functools.partial(_ffn_kernel, n_t=n_t),
        grid=(m // TILE,),
        in_specs=[pl.BlockSpec((TILE, d), lambda s: (s, 0)),
                  pl.BlockSpec((8, d), lambda s: (jnp.maximum(s * (TILE // 8) - 1, 0), 0)),
                  pl.BlockSpec((8, d), lambda s: (jnp.minimum((s + 1) * (TILE // 8), n8 - 1), 0)),
                  pl.BlockSpec((1, d), const),
                  pl.BlockSpec((1, 1, d), row), pl.BlockSpec((1, 1, d), row),
                  pl.BlockSpec((d, 2 * dff), const),
                  pl.BlockSpec((3, dff), const), pl.BlockSpec((1, dff), const),
                  pl.BlockSpec((1, 1, d), row),
                  pl.BlockSpec((dff, d), const)],
        out_specs=pl.BlockSpec((TILE, d), out_map),
        out_shape=jax.ShapeDtypeStruct((out_rows, d), F32),
        scratch_shapes=[pltpu.VMEM((TILE, d), F32)],
        compiler_params=_params("arbitrary"),
        name="ffn",
    )(x, x, x, g.reshape(1, d), shift_tab, scale_tab, w1, conv_w, conv_b.reshape(1, dff), mod_tab, w2)


def _rope_tables(n_lat):
    t = jnp.arange(n_lat, dtype=jnp.int32)
    row = (t // GRID_W).astype(F32)
    col = (t % GRID_W).astype(F32)
    d_axis = 32
    inv_freq = ROPE_THETA ** (-jnp.arange(0, d_axis, 2, dtype=F32) / d_axis)
    ang_r = row[None, :] * inv_freq[:, None]
    ang_c = col[None, :] * inv_freq[:, None]
    cr, sr, cc, sc = jnp.cos(ang_r), jnp.sin(ang_r), jnp.cos(ang_c), jnp.sin(ang_c)
    cos_lat = jnp.concatenate([cr, cr, cc, cc], axis=0)
    sin_lat = jnp.concatenate([-sr, sr, -sc, sc], axis=0)
    cos_t = jnp.concatenate([jnp.ones((64, TILE), F32), cos_lat], axis=1)
    sin_t = jnp.concatenate([jnp.zeros((64, TILE), F32), sin_lat], axis=1)
    return cos_t, sin_t


def kernel(x, c, ctx, c_ctx, w_mod, b_mod, norm1_g, w_in, da_qn_g, da_kn_g, da_lambda, da_subln_g, gq_qn_g, gq_kn_g, dn_conv_w, dn_a_log, dn_dt_bias, dn_norm_g, w_br_a, w_br_b, w_br_c, w_o, norm2_g, ffn_w1, ffn_conv_w, ffn_conv_b, ffn_w2):
    batch, n_lat, d = x.shape
    depth = w_mod.shape[0]
    dff = ffn_w2.shape[1]
    assert ctx.shape[1] == TILE and n_lat % TILE == 0 and n_lat % GRID_W == 0 and batch + 1 <= 8
    n_t = 1 + n_lat // TILE
    assert n_t % 2 == 1, "the attention key loop handles key tiles in pairs plus one"
    s_tiles = batch * n_t
    m = s_tiles * TILE

    xs = jnp.concatenate([ctx, x], axis=1).reshape(m, d)

    cond = jnp.zeros((8, d), F32).at[:batch].set(c).at[batch].set(c_ctx)
    mod = _modulation(cond, w_mod, b_mod)
    tile_id = jnp.arange(s_tiles)
    row_of_tile = jnp.where(tile_id % n_t == 0, batch, tile_id // n_t)
    mod_tabs = mod[:, row_of_tile, :].reshape(depth, s_tiles, 6, 1, d)

    cos_t, sin_t = _rope_tables(n_lat)
    cos_m, sin_m = jnp.tile(cos_t, (1, batch)), jnp.tile(sin_t, (1, batch))
    bcast = lambda g: jnp.broadcast_to(g[:, None], (g.shape[0], TILE))

    o_qkv = ATTN_ROWS
    o_b = o_qkv + 3 * DN_WIDTH
    o_z = o_b + 4 * DN_HEADS
    o_g = o_z + DN_WIDTH
    wt_attn = jnp.swapaxes(w_in[:, :, :ATTN_ROWS], 1, 2).astype(BF16)
    w_nat = jnp.concatenate([w_in[:, :, o_g:], w_in[:, :, o_z:o_g]], axis=2).astype(BF16)
    w_qkv = w_in[:, :, o_qkv:o_b].astype(BF16)
    w_ba = jnp.concatenate([w_in[:, :, o_b:o_z], jnp.zeros((depth, d, BA_COLS - 4 * DN_HEADS), F32)],
                           axis=2).astype(BF16)
    pad_tab = lambda a: jnp.zeros((1, BA_COLS), F32).at[0, 2 * DN_HEADS:4 * DN_HEADS].set(a.reshape(-1))

    for l in range(depth):
        lam_init = 0.8 - 0.6 * math.exp(-0.3 * l)
        tab = lambda i: mod_tabs[l, :, i]

        h1 = _norm_mod(xs, norm1_g[l], tab(0), tab(1))
        y_nat = _matmul_nn(h1, w_nat[l], NAT_COLS, BF16)

        gains = jnp.stack([bcast(da_qn_g[l]), bcast(da_kn_g[l]), bcast(gq_qn_g[l]), bcast(gq_kn_g[l])])
        qda, kda, vda, qgq, kgq, vgq = _inproj_attn(wt_attn[l], h1, cos_m, sin_m, gains)
        oa = _flash_da(qda, kda, vda, da_lambda[l], bcast(da_subln_g[l]), batch, n_t, lam_init)
        ob = _flash_gq(qgq, kgq, vgq, batch, n_t)

        dq, dk, dv, gb = _dn_proj(h1, w_qkv[l], w_ba[l], dn_conv_w[l], pad_tab(dn_a_log[l]),
                                  pad_tab(dn_dt_bias[l]), n_t)
        ocf, ocb = _dn_scan(dq, dk, dv, gb, batch, n_t)

        xs = _merge(xs, oa, ob, ocf, ocb, y_nat, dn_norm_g[l].reshape(1, -1), tab(2),
                    w_br_a[l].astype(BF16), w_br_b[l].astype(BF16), w_br_c[l].astype(BF16), w_o[l].astype(BF16))

        xs = _ffn(xs, norm2_g[l], tab(3), tab(4), ffn_w1[l].astype(BF16), ffn_conv_w[l], ffn_conv_b[l], tab(5),
                  ffn_w2[l].astype(BF16), n_t, latent_only=(l == depth - 1))

    return xs.reshape(batch, n_lat, d)
```
